```python
import math
import jax, jax.numpy as jnp
from jax import lax
import numpy as np

D_MODEL = 1024
BATCH = 4
SEQ = 4096
DEPTH = 1

GRID_W = 64
HEAD_DIM = 64
NA_HEADS = 8
NA_KH_MAX = 8
NA_KW = 16
NA_QBLK = 16
NA_BAND = 2 * NA_KW
NA_W = NA_HEADS * HEAD_DIM
GQA_HEADS = 8
GQA_KV_HEADS = 2
GQA_QBLK = 128
ROPE_THETA = 10000.0
GQA_Q_W = GQA_HEADS * HEAD_DIM
GQA_KV_W = GQA_KV_HEADS * HEAD_DIM
MEM_TOKENS = 256
MEM_HEADS = 4
MEM_HEAD_DIM = 128
MEM_W = MEM_HEADS * MEM_HEAD_DIM
N_BRANCH = 3
BRANCH_WIDTH = 512
GATE_W = N_BRANCH * D_MODEL
IN_W = 3 * NA_W + GQA_Q_W + 2 * GQA_KV_W + MEM_W + GATE_W
N_EXPERTS = 16
EC_CAPACITY_FACTOR = 2
D_FF_EXPERT = 1024
LN_EPS = 1e-5
RMS_EPS = 1e-6
DN_ALPHA = (2 * DEPTH) ** 0.25
DN_BETA = (8 * DEPTH) ** -0.25
NEG_INF = -1e30

kernel_name = "hybrid_natten_gqa_mem_ec_moe_deepnorm"


def _split_points():
    sizes = (NA_W, NA_W, NA_W, GQA_Q_W, GQA_KV_W, GQA_KV_W, MEM_W)
    return tuple(int(v) for v in np.cumsum(sizes))


def layer_norm(x, g, b):
    xf = x.astype(jnp.float32)
    mu = jnp.mean(xf, axis=-1, keepdims=True)
    var = jnp.mean(jnp.square(xf - mu), axis=-1, keepdims=True)
    return ((xf - mu) * lax.rsqrt(var + LN_EPS) * g.astype(jnp.float32) + b.astype(jnp.float32)).astype(x.dtype)


def rms_norm(x, g):
    xf = x.astype(jnp.float32)
    ms = jnp.mean(jnp.square(xf), axis=-1, keepdims=True)
    return (xf * lax.rsqrt(ms + RMS_EPS) * g.astype(jnp.float32)).astype(x.dtype)


def axial_rope_tables(seq_len):
    t = jnp.arange(seq_len)
    row = (t // GRID_W).astype(jnp.float32)
    col = (t % GRID_W).astype(jnp.float32)
    half = HEAD_DIM // 2
    inv = ROPE_THETA ** (-jnp.arange(0, half, 2, dtype=jnp.float32) / half)
    ang = jnp.concatenate([row[:, None] * inv, col[:, None] * inv], axis=-1)
    return jnp.cos(ang), jnp.sin(ang)


def apply_rope(x, cos, sin):
    xf = x.astype(jnp.float32)
    x1, x2 = xf[..., 0::2], xf[..., 1::2]
    c, s = cos[None, :, None, :], sin[None, :, None, :]
    out = jnp.stack([x1 * c - x2 * s, x1 * s + x2 * c], axis=-1).reshape(x.shape)
    return out.astype(x.dtype)


def neighbourhood_attention(q, k, v, rpb):
    S, H, dh = q.shape
    rows = S // GRID_W
    kh = min(NA_KH_MAX, rows)
    ncb = GRID_W // NA_QBLK
    qg = q.reshape(rows, ncb, NA_QBLK, H, dh)
    kg = k.reshape(rows, GRID_W, H, dh)
    vg = v.reshape(rows, GRID_W, H, dh)
    r = jnp.arange(rows)
    row_start = jnp.clip(r - kh // 2, 0, rows - kh)
    key_rows = row_start[:, None] + jnp.arange(kh)
    cb = jnp.arange(ncb)
    band_start = jnp.clip(cb * NA_QBLK - NA_KW // 2, 0, GRID_W - NA_BAND)
    key_cols = band_start[:, None] + jnp.arange(NA_BAND)
    ri = key_rows[:, None, :, None]
    ci = key_cols[None, :, None, :]
    k_blk = kg[ri, ci]
    v_blk = vg[ri, ci]
    qcol = cb[:, None] * NA_QBLK + jnp.arange(NA_QBLK)
    col_start = jnp.clip(qcol - NA_KW // 2, 0, GRID_W - NA_KW)
    kc = key_cols[:, None, :]
    in_win = (kc >= col_start[:, :, None]) & (kc < col_start[:, :, None] + NA_KW)
    dr = key_rows - r[:, None]
    dc = jnp.clip(kc - qcol[:, :, None] + NA_KW - 1, 0, 2 * NA_KW - 2)
    bias = rpb[:, (dr + NA_KH_MAX - 1)[:, None, None, :, None], dc[None, :, :, None, :]]
    scores = jnp.einsum('rcqhd,rckwhd->hrcqkw', qg, k_blk,
                        preferred_element_type=jnp.float32) * (1.0 / math.sqrt(dh))
    scores = jnp.where(in_win[None, None, :, :, None, :], scores + bias.astype(jnp.float32), NEG_INF)
    p = jax.nn.softmax(scores, axis=(-2, -1))
    out = jnp.einsum('hrcqkw,rckwhd->rcqhd', p.astype(v.dtype), v_blk)
    return out.reshape(S, H * dh)


def gqa_attention(q, k, v, q_gain, k_gain, cos, sin):
    q = apply_rope(rms_norm(q, q_gain), cos, sin)
    k = apply_rope(rms_norm(k, k_gain), cos, sin)
    B, S, Hq, dh = q.shape
    hkv = k.shape[2]
    grp = Hq // hkv
    nblk = S // GQA_QBLK
    qb = q.reshape(B, nblk, GQA_QBLK, hkv, grp, dh).transpose(1, 0, 3, 4, 2, 5)
    scale = 1.0 / math.sqrt(dh)

    def block(qi):
        s = jnp.einsum('bkgqd,bskd->bkgqs', qi, k, preferred_element_type=jnp.float32) * scale
        p = jax.nn.softmax(s, axis=-1)
        return jnp.einsum('bkgqs,bskd->bqkgd', p.astype(v.dtype), v)

    out = lax.map(block, qb)
    return out.transpose(1, 0, 2, 3, 4, 5).reshape(B, S, Hq * dh)


def memory_attention(q, mk, mv):
    B, S, H, dm = q.shape
    s = jnp.einsum('bshd,bmhd->bhsm', q, mk, preferred_element_type=jnp.float32) * (1.0 / math.sqrt(dm))
    p = jax.nn.softmax(s, axis=-1)
    return jnp.einsum('bhsm,bmhd->bshd', p.astype(mv.dtype), mv).reshape(B, S, H * dm)


def expert_choice_ffn(x, w_router, w_gate_up, w_down):
    B, T, D = x.shape
    cap = EC_CAPACITY_FACTOR * T // N_EXPERTS
    logits = jnp.einsum('btd,de->bte', x, w_router, preferred_element_type=jnp.float32)
    aff = jax.nn.softmax(logits, axis=-1)
    gate, idx = lax.top_k(jnp.swapaxes(aff, 1, 2), cap)
    bidx = jnp.arange(B)[:, None, None]
    xs = x[bidx, idx]
    h = jnp.einsum('becd,edf->becf', xs, w_gate_up)
    hg, hu = jnp.split(h, 2, axis=-1)
    ye = jnp.einsum('becf,efd->becd', jax.nn.silu(hg) * hu, w_down)
    ye = ye * gate[..., None].astype(ye.dtype)
    return jnp.zeros_like(x).at[bidx, idx].add(ye)


def setup_inputs(seed: int = 0) -> dict:
    key = jax.random.key(seed)
    ks = jax.random.split(key, 20)

    def nrm(k, shape, scale):
        return jax.random.normal(k, shape, jnp.float32) * scale

    return {
        "x": nrm(ks[0], (BATCH, SEQ, D_MODEL), 1.0),
        "mem": nrm(ks[1], (BATCH, MEM_TOKENS, D_MODEL), 1.0),
        "w_in": nrm(ks[2], (DEPTH, D_MODEL, IN_W), D_MODEL ** -0.5),
        "b_gate": nrm(ks[3], (DEPTH, GATE_W), 0.1),
        "na_rpb": nrm(ks[4], (DEPTH, NA_HEADS, 2 * NA_KH_MAX - 1, 2 * NA_KW - 1), 0.05),
        "gqa_q_gain": 1.0 + nrm(ks[5], (DEPTH, HEAD_DIM), 0.1),
        "gqa_k_gain": 1.0 + nrm(ks[6], (DEPTH, HEAD_DIM), 0.1),
        "w_mem_kv": nrm(ks[7], (DEPTH, D_MODEL, 2 * MEM_W), D_MODEL ** -0.5),
        "w_branch": nrm(ks[8], (DEPTH, N_BRANCH, BRANCH_WIDTH, D_MODEL), BRANCH_WIDTH ** -0.5),
        "w_out": nrm(ks[9], (DEPTH, D_MODEL, D_MODEL), D_MODEL ** -0.5 * DN_BETA),
        "ln1_g": 1.0 + nrm(ks[10], (DEPTH, D_MODEL), 0.1),
        "ln1_b": nrm(ks[11], (DEPTH, D_MODEL), 0.02),
        "w_router": nrm(ks[12], (DEPTH, D_MODEL, N_EXPERTS), D_MODEL ** -0.5),
        "w_gate_up": nrm(ks[13], (DEPTH, N_EXPERTS, D_MODEL, 2 * D_FF_EXPERT), D_MODEL ** -0.5),
        "w_down": nrm(ks[14], (DEPTH, N_EXPERTS, D_FF_EXPERT, D_MODEL), D_FF_EXPERT ** -0.5 * DN_BETA),
        "ln2_g": 1.0 + nrm(ks[15], (DEPTH, D_MODEL), 0.1),
        "ln2_b": nrm(ks[16], (DEPTH, D_MODEL), 0.02),
    }


def reference(x, mem, w_in, b_gate, na_rpb, gqa_q_gain, gqa_k_gain, w_mem_kv, w_branch, w_out,
              ln1_g, ln1_b, w_router, w_gate_up, w_down, ln2_g, ln2_b):
    B, S, D = x.shape
    M = mem.shape[1]
    cos, sin = axial_rope_tables(S)
    for l in range(DEPTH):
        z = jnp.einsum('bsd,df->bsf', x, w_in[l])
        q_na, k_na, v_na, q_g, k_g, v_g, q_m, gate_logits = jnp.split(z, _split_points(), axis=-1)
        rpb = na_rpb[l]
        y_na = lax.map(lambda qkv: neighbourhood_attention(qkv[0], qkv[1], qkv[2], rpb),
                       (q_na.reshape(B, S, NA_HEADS, HEAD_DIM),
                        k_na.reshape(B, S, NA_HEADS, HEAD_DIM),
                        v_na.reshape(B, S, NA_HEADS, HEAD_DIM)))
        y_gqa = gqa_attention(q_g.reshape(B, S, GQA_HEADS, HEAD_DIM),
                              k_g.reshape(B, S, GQA_KV_HEADS, HEAD_DIM),
                              v_g.reshape(B, S, GQA_KV_HEADS, HEAD_DIM),
                              gqa_q_gain[l], gqa_k_gain[l], cos, sin)
        mkv = jnp.einsum('bmd,df->bmf', mem, w_mem_kv[l])
        mk, mv = jnp.split(mkv, 2, axis=-1)
        y_mem = memory_attention(q_m.reshape(B, S, MEM_HEADS, MEM_HEAD_DIM),
                                 mk.reshape(B, M, MEM_HEADS, MEM_HEAD_DIM),
                                 mv.reshape(B, M, MEM_HEADS, MEM_HEAD_DIM))
        branches = jnp.stack([y_na, y_gqa, y_mem], axis=2)
        proj = jnp.einsum('bsgk,gkd->bsgd', branches, w_branch[l])
        gates = jax.nn.sigmoid(gate_logits + b_gate[l]).reshape(B, S, N_BRANCH, D)
        merged = jnp.sum(gates * proj, axis=2)
        mix = jnp.einsum('bsd,de->bse', merged, w_out[l])
        x = layer_norm(DN_ALPHA * x + mix, ln1_g[l], ln1_b[l])
        ffn = expert_choice_ffn(x, w_router[l], w_gate_up[l], w_down[l])
        x = layer_norm(DN_ALPHA * x + ffn, ln2_g[l], ln2_b[l])
    return x
```

```python
import functools
import math

import jax
import jax.numpy as jnp
import numpy as np
from jax import lax
from jax.experimental import pallas as pl
from jax.experimental.pallas import tpu as pltpu

F32 = jnp.float32
BF16 = jnp.bfloat16

D_MODEL = 1024
GRID_W = 64
HEAD_DIM = 64
LANES = 128
NA_HEADS = 8
NA_KH = 8
NA_KW = 16
NA_W = NA_HEADS * HEAD_DIM
NA_QROWS = 4
NA_KROWS = 12
GQA_HEADS = 8
GQA_KV_HEADS = 2
GQA_GROUP = GQA_HEADS // GQA_KV_HEADS
GQA_Q_W = GQA_HEADS * HEAD_DIM
GQA_KV_W = GQA_KV_HEADS * HEAD_DIM
ROPE_THETA = 10000.0
MEM_HEADS = 4
MEM_HEAD_DIM = 128
MEM_W = MEM_HEADS * MEM_HEAD_DIM
N_BRANCH = 3
N_EXPERTS = 16
EC_CAPACITY_FACTOR = 2
LN_EPS = 1e-5
RMS_EPS = 1e-6
NEG_INF = -1e30
QKV_W = 3 * NA_W + GQA_Q_W + 2 * GQA_KV_W + MEM_W

VMEM_LIMIT = 56 * 1024 * 1024


def _cparams(sem, vmem=VMEM_LIMIT):
    return pltpu.CompilerParams(dimension_semantics=sem, vmem_limit_bytes=vmem)


def _dot(a, b):
    return jnp.dot(a, b, preferred_element_type=F32)


def _dot_nt(a, b):
    return lax.dot_general(a, b, (((1,), (1,)), ((), ())), preferred_element_type=F32)


def _layer_norm(v, g, b):
    mu = jnp.mean(v, axis=-1, keepdims=True)
    var = jnp.mean(jnp.square(v - mu), axis=-1, keepdims=True)
    return (v - mu) * lax.rsqrt(var + LN_EPS) * g + b


def _na_bias_kernel(rpb_ref, out_ref):
    n = GRID_W * GRID_W
    col = lax.broadcasted_iota(jnp.int32, (32, n), 1)
    d = lax.broadcasted_iota(jnp.int32, (32, n), 0)
    c, kc = col // GRID_W, col % GRID_W
    onehot = (kc - c + (NA_KW - 1) == d).astype(F32)
    t = jnp.dot(rpb_ref[...], onehot, precision=lax.Precision.HIGHEST, preferred_element_type=F32)
    col1 = lax.broadcasted_iota(jnp.int32, (1, n), 1)
    c1, kc1 = col1 // GRID_W, col1 % GRID_W
    start = jnp.clip(c1 - NA_KW // 2, 0, GRID_W - NA_KW)
    in_win = (kc1 >= start) & (kc1 < start + NA_KW)
    out_ref[...] = jnp.where(in_win, t, NEG_INF)


def _na_bias_tiles(rpb):
    h = rpb.shape[0]
    nd = 2 * NA_KH - 1
    rpb2 = jnp.pad(rpb.reshape(h * nd, 2 * NA_KW - 1), ((0, 0), (0, 1)))
    t = pl.pallas_call(
        _na_bias_kernel,
        out_shape=jax.ShapeDtypeStruct((h * nd, GRID_W * GRID_W), F32),
        name="na_bias",
    )(rpb2).reshape(h, nd, GRID_W, GRID_W)
    masked = jnp.full((h, GRID_W, GRID_W), NEG_INF, F32)
    variants = []
    for variant in range(3):
        rows = []
        for i in range(NA_QROWS):
            blocks = []
            for j in range(NA_KROWS):
                if variant == 0:
                    dr, ok = j - i, j < NA_KH
                elif variant == 1:
                    dr = j - i - NA_KH // 2
                    ok = -(NA_KH // 2) <= dr < NA_KH // 2
                else:
                    dr, ok = j - i - NA_KH, j >= NA_KROWS - NA_KH
                blocks.append(t[:, dr + NA_KH - 1] if ok else masked)
            rows.append(jnp.concatenate(blocks, axis=-1))
        variants.append(jnp.concatenate(rows, axis=-2))
    return jnp.stack(variants, axis=0)


PROJ_TM = 512


def _rope_tables(seq_len):
    t = jnp.arange(seq_len)
    row = (t // GRID_W).astype(F32)
    col = (t % GRID_W).astype(F32)
    half = HEAD_DIM // 2
    inv = ROPE_THETA ** (-jnp.arange(0, half, 2, dtype=F32) / half)
    ang = jnp.concatenate([row[:, None] * inv, col[:, None] * inv], axis=-1)
    cos, sin = jnp.cos(ang), jnp.sin(ang)
    cos_l = jnp.repeat(cos, 2, axis=-1)
    sin_l = jnp.stack([-sin, sin], axis=-1).reshape(seq_len, HEAD_DIM)
    return jnp.tile(cos_l, (1, 2)), jnp.tile(sin_l, (1, 2))


def _head_sums(sq, width):
    r = lax.broadcasted_iota(jnp.int32, (width, width), 0) // HEAD_DIM
    c = lax.broadcasted_iota(jnp.int32, (width, width), 1) // HEAD_DIM
    ones = (r == c).astype(BF16)
    hi = sq.astype(BF16)
    lo = (sq - hi.astype(F32)).astype(BF16)
    return _dot(hi, ones) + _dot(lo, ones)


def _rope(v, cos, sin_signed):
    lane = lax.broadcasted_iota(jnp.int32, v.shape, 1)
    swapped = jnp.where(lane % 2 == 0, pltpu.roll(v, LANES - 1, 1), pltpu.roll(v, 1, 1))
    return v * cos + swapped * sin_signed


def _proj_kernel(x_ref, w_ref, qgain_ref, kgain_ref, cos_ref, sin_ref,
                 qna_ref, kna_ref, vna_ref, qg_ref, kg_ref, vg_ref, qm_ref):
    xb = x_ref[...].astype(BF16)
    scale = 1.0 / math.sqrt(HEAD_DIM)
    o = 0
    qna_ref[...] = (_dot(xb, w_ref[:, o:o + NA_W]) * scale).astype(BF16)
    o += NA_W
    kna_ref[...] = _dot(xb, w_ref[:, o:o + NA_W]).astype(BF16)
    o += NA_W
    vna_ref[...] = _dot(xb, w_ref[:, o:o + NA_W]).astype(BF16)
    o += NA_W
    cos, sin = cos_ref[...], sin_ref[...]
    lane = lax.broadcasted_iota(jnp.int32, (PROJ_TM, LANES), 1)
    low = lane < HEAD_DIM

    zq = _dot(xb, w_ref[:, o:o + GQA_Q_W])
    o += GQA_Q_W
    ms = _head_sums(zq * zq, GQA_Q_W) * (1.0 / HEAD_DIM)
    zq = zq * lax.rsqrt(ms + RMS_EPS) * qgain_ref[...]
    for p in range(GQA_HEADS // 2):
        r = _rope(zq[:, p * LANES:(p + 1) * LANES], cos, sin) * scale
        rolled = pltpu.roll(r, HEAD_DIM, 1)
        if (2 * p) // GQA_GROUP == 0:
            even, odd = jnp.where(low, r, 0.0), jnp.where(low, rolled, 0.0)
        else:
            even, odd = jnp.where(low, 0.0, rolled), jnp.where(low, 0.0, r)
        qg_ref[0, 2 * p] = even.astype(BF16)
        qg_ref[0, 2 * p + 1] = odd.astype(BF16)

    zk = _dot(xb, w_ref[:, o:o + GQA_KV_W])
    o += GQA_KV_W
    ms = _head_sums(zk * zk, GQA_KV_W) * (1.0 / HEAD_DIM)
    zk = zk * lax.rsqrt(ms + RMS_EPS) * kgain_ref[...]
    kg_ref[...] = _rope(zk, cos, sin).astype(BF16)
    vg_ref[...] = _dot(xb, w_ref[:, o:o + GQA_KV_W]).astype(BF16)
    o += GQA_KV_W
    qm_ref[...] = _dot(xb, w_ref[:, o:o + MEM_W]).astype(BF16)


def _in_proj(x2, w_qkv, qgain, kgain, batch, seq):
    tokens = x2.shape[0]
    tm = PROJ_TM
    spb = seq // tm
    cos, sin = _rope_tables(seq)
    row = lambda w: pl.BlockSpec((tm, w), lambda i: (i, 0))
    const = lambda shape: pl.BlockSpec(shape, lambda i: (0,) * len(shape))
    tbl = pl.BlockSpec((tm, LANES), lambda i: (i % spb, 0))
    sds = lambda w: jax.ShapeDtypeStruct((tokens, w), BF16)
    return pl.pallas_call(
        _proj_kernel,
        grid=(tokens // tm,),
        in_specs=[row(D_MODEL), const((D_MODEL, QKV_W)), const((1, GQA_Q_W)), const((1, GQA_KV_W)), tbl, tbl],
        out_specs=[row(NA_W), row(NA_W), row(NA_W),
                   pl.BlockSpec((1, GQA_HEADS, tm, LANES), lambda i: (i // spb, 0, i % spb, 0)),
                   row(GQA_KV_W), row(GQA_KV_W), row(MEM_W)],
        out_shape=[sds(NA_W), sds(NA_W), sds(NA_W),
                   jax.ShapeDtypeStruct((batch, GQA_HEADS, seq, LANES), BF16),
                   sds(GQA_KV_W), sds(GQA_KV_W), sds(MEM_W)],
        compiler_params=_cparams(("parallel",)),
        name="in_proj",
    )(x2, w_qkv, qgain, kgain, cos, sin)


def _memkv_kernel(m_ref, w_ref, k_ref, v_ref):
    z = _dot(m_ref[...].astype(BF16), w_ref[...])
    k_ref[...] = z[:, :MEM_W].astype(BF16)
    v_ref[...] = z[:, MEM_W:].astype(BF16)


def _mem_kv(mem2, w_kv):
    rows = mem2.shape[0]
    tm = 256
    return pl.pallas_call(
        _memkv_kernel,
        grid=(rows // tm,),
        in_specs=[pl.BlockSpec((tm, D_MODEL), lambda i: (i, 0)),
                  pl.BlockSpec((D_MODEL, 2 * MEM_W), lambda i: (0, 0))],
        out_specs=[pl.BlockSpec((tm, MEM_W), lambda i: (i, 0))] * 2,
        out_shape=[jax.ShapeDtypeStruct((rows, MEM_W), BF16)] * 2,
        compiler_params=_cparams(("parallel",)),
        name="mem_kv",
    )(mem2, w_kv)


NA_TQ = NA_QROWS * GRID_W
NA_TK = NA_KROWS * GRID_W


def _na_kernel(q_ref, k_ref, v_ref, bias_ref, o_ref, *, n_groups):
    rg = pl.program_id(2)
    key_row0 = jnp.clip(NA_QROWS * rg - NA_KH // 2, 0, GRID_W - NA_KROWS)
    start = pl.multiple_of(key_row0 * GRID_W, GRID_W)
    variant = jnp.where(rg == 0, 0, jnp.where(rg == n_groups - 1, 2, 1))
    k = k_ref[pl.ds(start, NA_TK), :]
    v = v_ref[pl.ds(start, NA_TK), :]
    q = q_ref[...]
    lane = lax.broadcasted_iota(jnp.int32, q.shape, 1)
    low = lane < HEAD_DIM
    outs = []
    for hh in range(2):
        qh = jnp.where(low if hh == 0 else jnp.logical_not(low), q, jnp.zeros_like(q))
        s = _dot_nt(qh, k) + bias_ref[variant, hh]
        m = jnp.max(s, axis=-1, keepdims=True)
        p = jnp.exp(s - m)
        l = jnp.sum(p, axis=-1, keepdims=True)
        outs.append(_dot(p.astype(BF16), v) / l)
    o_ref[...] = jnp.where(low, outs[0], outs[1]).astype(BF16)


def _na_attention(qna, kna, vna, bias, batch, seq):
    n_groups = seq // NA_TQ
    pairs = NA_HEADS // 2
    kv = pl.BlockSpec((seq, LANES), lambda p, b, g: (b, p))
    qo = pl.BlockSpec((NA_TQ, LANES), lambda p, b, g: (b * n_groups + g, p))
    return pl.pallas_call(
        functools.partial(_na_kernel, n_groups=n_groups),
        grid=(pairs, batch, n_groups),
        in_specs=[qo, kv, kv, pl.BlockSpec((3, 2, NA_TQ, NA_TK), lambda p, b, g: (0, p, 0, 0))],
        out_specs=qo,
        out_shape=jax.ShapeDtypeStruct(qna.shape, BF16),
        compiler_params=_cparams(("parallel", "parallel", "parallel")),
        name="na_attn",
    )(qna, kna, vna, bias)


GQA_TQ = 256
GQA_TK = 512


def _gqa_kernel(q_ref, k_ref, v_ref, o_ref, m_sc, l_sc, acc_sc, *, seq):
    rows = GQA_GROUP * GQA_TQ
    lane = lax.broadcasted_iota(jnp.int32, (GQA_TQ, LANES), 1)
    low = lane < HEAD_DIM
    for g in range(GQA_KV_HEADS):
        q = q_ref[0, g * GQA_GROUP:(g + 1) * GQA_GROUP].reshape(rows, LANES)
        m_sc[...] = jnp.full((rows, 1), -jnp.inf, F32)
        l_sc[...] = jnp.zeros((rows, 1), F32)
        acc_sc[...] = jnp.zeros((rows, LANES), F32)

        def chunk(c, carry):
            off = pl.multiple_of(c * GQA_TK, GQA_TK)
            k = k_ref[pl.ds(off, GQA_TK), :]
            v = v_ref[pl.ds(off, GQA_TK), :]
            s = _dot_nt(q, k)
            m_prev = m_sc[...]
            m_new = jnp.maximum(m_prev, jnp.max(s, axis=-1, keepdims=True))
            alpha = jnp.exp(m_prev - m_new)
            p = jnp.exp(s - m_new)
            l_sc[...] = alpha * l_sc[...] + jnp.sum(p, axis=-1, keepdims=True)
            acc_sc[...] = alpha * acc_sc[...] + _dot(p.astype(BF16), v)
            m_sc[...] = m_new
            return carry

        lax.fori_loop(0, seq // GQA_TK, chunk, 0)
        out = acc_sc[...] / l_sc[...]
        for pp in range(GQA_GROUP // 2):
            a = out[(2 * pp) * GQA_TQ:(2 * pp + 1) * GQA_TQ]
            b = out[(2 * pp + 1) * GQA_TQ:(2 * pp + 2) * GQA_TQ]
            if g == 0:
                pair = jnp.where(low, a, pltpu.roll(b, HEAD_DIM, 1))
            else:
                pair = jnp.where(low, pltpu.roll(a, HEAD_DIM, 1), b)
            col = (g * (GQA_GROUP // 2) + pp) * LANES
            o_ref[:, col:col + LANES] = pair.astype(BF16)


def _gqa_attention(qg, kg, vg, batch, seq):
    nq = seq // GQA_TQ
    rows = GQA_GROUP * GQA_TQ
    kv = pl.BlockSpec((seq, LANES), lambda b, i: (b, 0))
    return pl.pallas_call(
        functools.partial(_gqa_kernel, seq=seq),
        grid=(batch, nq),
        in_specs=[pl.BlockSpec((1, GQA_HEADS, GQA_TQ, LANES), lambda b, i: (b, 0, i, 0)), kv, kv],
        out_specs=pl.BlockSpec((GQA_TQ, GQA_Q_W), lambda b, i: (b * nq + i, 0)),
        out_shape=jax.ShapeDtypeStruct((batch * seq, GQA_Q_W), BF16),
        scratch_shapes=[pltpu.VMEM((rows, 1), F32), pltpu.VMEM((rows, 1), F32), pltpu.VMEM((rows, LANES), F32)],
        compiler_params=_cparams(("parallel", "parallel")),
        name="gqa_attn",
    )(qg, kg, vg)


MERGE_TM = 256


def _merge_kernel(x_ref, yna_ref, ygqa_ref, qm_ref, mk_ref, mv_ref, wg_ref, bg_ref, wb_ref, wo_ref,
                  g1_ref, b1_ref, wr_ref, x1_ref, aff_ref, *, alpha):
    x = x_ref[...]
    xb = x.astype(BF16)
    mem_scale = 1.0 / math.sqrt(MEM_HEAD_DIM)
    ymem = []
    for h in range(MEM_HEADS):
        sl = slice(h * MEM_HEAD_DIM, (h + 1) * MEM_HEAD_DIM)
        s = _dot_nt(qm_ref[:, sl], mk_ref[:, sl]) * mem_scale
        m = jnp.max(s, axis=-1, keepdims=True)
        p = jnp.exp(s - m)
        l = jnp.sum(p, axis=-1, keepdims=True)
        ymem.append((_dot(p.astype(BF16), mv_ref[:, sl]) / l).astype(BF16))
    branches = (yna_ref[...], ygqa_ref[...], jnp.concatenate(ymem, axis=-1))
    merged = jnp.zeros((MERGE_TM, D_MODEL), F32)
    for g in range(N_BRANCH):
        cols = slice(g * D_MODEL, (g + 1) * D_MODEL)
        gate = jax.nn.sigmoid(_dot(xb, wg_ref[:, cols]) + bg_ref[:, cols])
        merged = merged + gate * _dot(branches[g], wb_ref[g])
    mix = _dot(merged.astype(BF16), wo_ref[...])
    x1 = _layer_norm(alpha * x + mix, g1_ref[...], b1_ref[...])
    x1_ref[...] = x1
    logits = jnp.dot(x1, wr_ref[...], precision=lax.Precision.HIGHEST, preferred_element_type=F32)
    lane = lax.broadcasted_iota(jnp.int32, logits.shape, 1)
    logits = jnp.where(lane < N_EXPERTS, logits, -jnp.inf)
    m = jnp.max(logits, axis=-1, keepdims=True)
    e = jnp.exp(logits - m)
    aff_ref[...] = e / jnp.sum(e, axis=-1, keepdims=True)


def _merge(x2, yna, ygqa, qm, mk, mv, w_gate, b_gate, w_branch, w_out, g1, b1, w_router, seq, mem_tokens, alpha):
    tokens = x2.shape[0]
    tm = MERGE_TM
    tpb = seq // tm
    row = lambda w: pl.BlockSpec((tm, w), lambda i: (i, 0))
    const = lambda shape: pl.BlockSpec(shape, lambda i: (0,) * len(shape))
    memb = pl.BlockSpec((mem_tokens, MEM_W), lambda i: (i // tpb, 0))
    return pl.pallas_call(
        functools.partial(_merge_kernel, alpha=alpha),
        grid=(tokens // tm,),
        in_specs=[row(D_MODEL), row(NA_W), row(GQA_Q_W), row(MEM_W), memb, memb,
                  const(w_gate.shape), const(b_gate.shape), const(w_branch.shape), const(w_out.shape),
                  const(g1.shape), const(b1.shape), const(w_router.shape)],
        out_specs=[row(D_MODEL), row(LANES)],
        out_shape=[jax.ShapeDtypeStruct((tokens, D_MODEL), F32), jax.ShapeDtypeStruct((tokens, LANES), F32)],
        compiler_params=_cparams(("parallel",)),
        name="merge",
    )(x2, yna, ygqa, qm, mk, mv, w_gate, b_gate, w_branch, w_out, g1, b1, w_router)


SEL_BLK = 256
EC_QUARTERS = 4


def _prefix_counts(mask_ref, out_ref, seq):
    r = lax.broadcasted_iota(jnp.int32, (SEL_BLK, SEL_BLK), 0)
    c = lax.broadcasted_iota(jnp.int32, (SEL_BLK, SEL_BLK), 1)
    tri = (c <= r).astype(BF16)
    run = jnp.zeros((1, LANES), F32)
    for j in range(seq // SEL_BLK):
        blk = mask_ref[j * SEL_BLK:(j + 1) * SEL_BLK, :].astype(BF16)
        cs = _dot(tri, blk) + run
        out_ref[j * SEL_BLK:(j + 1) * SEL_BLK, :] = cs
        run = cs[SEL_BLK - 1:SEL_BLK, :]


def _select_kernel(aff_ref, idx_ref, gate_ref, bnd_ref, mask_sc, cnt_sc, idx_sc, gate_sc, *, seq, cap):
    a = aff_ref[...]
    bits = pltpu.bitcast(a, jnp.int32)
    thr = jnp.zeros((1, LANES), jnp.int32)
    for bit in range(30, -1, -1):
        cand = thr | (1 << bit)
        cnt = jnp.sum((bits >= cand).astype(F32), axis=0, keepdims=True)
        thr = jnp.where(cnt >= cap, cand, thr)
    gt = bits > thr
    eq = bits == thr
    need = cap - jnp.sum(gt.astype(F32), axis=0, keepdims=True)
    mask_sc[...] = eq.astype(F32)
    _prefix_counts(mask_sc, cnt_sc, seq)
    sel = gt | (eq & (cnt_sc[...] <= need))
    mask_sc[...] = sel.astype(F32)
    _prefix_counts(mask_sc, cnt_sc, seq)

    qrows = seq // EC_QUARTERS
    zero = jnp.zeros((1, LANES), jnp.int32)
    ends = [cnt_sc[(q + 1) * qrows - 1:(q + 1) * qrows, :].astype(jnp.int32) for q in range(EC_QUARTERS)]
    bnd_ref[0] = jnp.concatenate([zero] + ends + [zero] * (8 - 1 - EC_QUARTERS), axis=0)

    idx_sc[...] = jnp.zeros_like(idx_sc)
    gate_sc[...] = jnp.zeros_like(gate_sc)
    slot = lax.broadcasted_iota(jnp.int32, (1, cap), 1).astype(F32)

    def chunk(j, carry):
        off = pl.multiple_of(j * SEL_BLK, SEL_BLK)
        cnt = cnt_sc[pl.ds(off, SEL_BLK), :]
        picked = mask_sc[pl.ds(off, SEL_BLK), :] * aff_ref[pl.ds(off, SEL_BLK), :]
        for e in range(N_EXPERTS):
            ce = cnt[:, e:e + 1]
            idx_sc[e:e + 1, :] += jnp.sum((ce <= slot).astype(F32), axis=0, keepdims=True)
            hit = jnp.where(ce == slot + 1.0, picked[:, e:e + 1], 0.0)
            gate_sc[e:e + 1, :] += jnp.sum(hit, axis=0, keepdims=True)
        return carry

    lax.fori_loop(0, seq // SEL_BLK, chunk, 0)
    idx_ref[0] = idx_sc[...].astype(jnp.int32)
    gate_ref[0] = gate_sc[...]


def _ec_select(aff, batch, seq, cap):
    return pl.pallas_call(
        functools.partial(_select_kernel, seq=seq, cap=cap),
        grid=(batch,),
        in_specs=[pl.BlockSpec((seq, LANES), lambda b: (b, 0))],
        out_specs=[pl.BlockSpec((1, N_EXPERTS, cap), lambda b: (b, 0, 0)),
                   pl.BlockSpec((1, N_EXPERTS, cap), lambda b: (b, 0, 0)),
                   pl.BlockSpec((1, 8, LANES), lambda b: (b, 0, 0))],
        out_shape=[jax.ShapeDtypeStruct((batch, N_EXPERTS, cap), jnp.int32),
                   jax.ShapeDtypeStruct((batch, N_EXPERTS, cap), F32),
                   jax.ShapeDtypeStruct((batch, 8, LANES), jnp.int32)],
        scratch_shapes=[pltpu.VMEM((seq, LANES), F32), pltpu.VMEM((seq, LANES), F32),
                        pltpu.VMEM((N_EXPERTS, cap), F32), pltpu.VMEM((N_EXPERTS, cap), F32)],
        compiler_params=_cparams(("parallel",)),
        name="ec_select",
    )(aff)


def _ffn_kernel(idx_ref, x_ref, wgu_ref, wd_ref, y_ref, xs_sc, *, cap, d_ff):
    def gather(i, carry):
        t = idx_ref[0, 0, i]
        xs_sc[pl.ds(i, 1), :] = x_ref[0, pl.ds(t, 1), :]
        return carry

    lax.fori_loop(0, cap, gather, 0, unroll=8)
    xs = xs_sc[...].astype(BF16)
    h = _dot(xs, wgu_ref[0])
    hg, hu = h[:, :d_ff], h[:, d_ff:]
    act = (hg * jax.nn.sigmoid(hg) * hu).astype(BF16)
    y_ref[0] = _dot(act, wd_ref[0])


def _ec_ffn(idx, x1, wgu, wd, batch, seq, cap):
    d_ff = wd.shape[1]
    idx3 = idx.reshape(batch * N_EXPERTS, 1, cap)
    return pl.pallas_call(
        functools.partial(_ffn_kernel, cap=cap, d_ff=d_ff),
        grid=(batch, N_EXPERTS),
        in_specs=[pl.BlockSpec((1, 1, cap), lambda b, e: (b * N_EXPERTS + e, 0, 0), memory_space=pltpu.SMEM),
                  pl.BlockSpec((1, seq, D_MODEL), lambda b, e: (b, 0, 0), pipeline_mode=pl.Buffered(1)),
                  pl.BlockSpec((1, D_MODEL, 2 * d_ff), lambda b, e: (e, 0, 0)),
                  pl.BlockSpec((1, d_ff, D_MODEL), lambda b, e: (e, 0, 0))],
        out_specs=pl.BlockSpec((1, cap, D_MODEL), lambda b, e: (b * N_EXPERTS + e, 0, 0)),
        out_shape=jax.ShapeDtypeStruct((batch * N_EXPERTS, cap, D_MODEL), F32),
        scratch_shapes=[pltpu.VMEM((cap, D_MODEL), F32)],
        compiler_params=_cparams(("parallel", "arbitrary")),
        name="ec_ffn",
    )(idx3, x1, wgu, wd)


def _combine_kernel(idx_ref, gate_ref, bnd_ref, x_ref, y_ref, g_ref, b_ref, o_ref, *, alpha, qrows):
    q = pl.program_id(1)
    e = pl.program_id(2)

    @pl.when(e == 0)
    def _():
        o_ref[0] = alpha * x_ref[0]

    lo = bnd_ref[0, q, e]
    hi = bnd_ref[0, q + 1, e]
    base = q * qrows

    def scatter(p, carry):
        t = idx_ref[0, 0, p] - base
        o_ref[0, pl.ds(t, 1), :] += y_ref[0, pl.ds(p, 1), :] * gate_ref[0, 0, p]
        return carry

    lax.fori_loop(lo, hi, scatter, 0)

    @pl.when(e == N_EXPERTS - 1)
    def _():
        o_ref[0] = _layer_norm(o_ref[0], g_ref[...], b_ref[...])


def _ec_combine(idx, gate, bnd, x1, y, g2, b2, batch, seq, cap, alpha):
    qrows = seq // EC_QUARTERS
    idx3 = idx.reshape(batch * N_EXPERTS, 1, cap)
    gate3 = gate.reshape(batch * N_EXPERTS, 1, cap)
    slot = pl.BlockSpec((1, 1, cap), lambda b, q, e: (b * N_EXPERTS + e, 0, 0), memory_space=pltpu.SMEM)
    tile = pl.BlockSpec((1, qrows, D_MODEL), lambda b, q, e: (b, q, 0))
    const = lambda shape: pl.BlockSpec(shape, lambda b, q, e: (0,) * len(shape))
    return pl.pallas_call(
        functools.partial(_combine_kernel, alpha=alpha, qrows=qrows),
        grid=(batch, EC_QUARTERS, N_EXPERTS),
        in_specs=[slot, slot,
                  pl.BlockSpec((1, 8, LANES), lambda b, q, e: (b, 0, 0), memory_space=pltpu.SMEM),
                  tile,
                  pl.BlockSpec((1, cap, D_MODEL), lambda b, q, e: (b * N_EXPERTS + e, 0, 0)),
                  const(g2.shape), const(b2.shape)],
        out_specs=tile,
        out_shape=jax.ShapeDtypeStruct((batch, seq, D_MODEL), F32),
        compiler_params=_cparams(("parallel", "parallel", "arbitrary")),
        name="ec_combine",
    )(idx3, gate3, bnd, x1, y, g2, b2)


def kernel(x, mem, w_in, b_gate, na_rpb, gqa_q_gain, gqa_k_gain, w_mem_kv, w_branch, w_out,
           ln1_g, ln1_b, w_router, w_gate_up, w_down, ln2_g, ln2_b):
    batch, seq, d = x.shape
    mem_tokens = mem.shape[1]
    depth = w_in.shape[0]
    alpha = (2 * depth) ** 0.25
    cap = EC_CAPACITY_FACTOR * seq // N_EXPERTS
    assert d == D_MODEL and seq == GRID_W * GRID_W and w_in.shape[2] == QKV_W + N_BRANCH * D_MODEL
    row = lambda v: v.reshape(1, -1)
    for l in range(depth):
        x2 = x.reshape(batch * seq, d)
        w_qkv = w_in[l, :, :QKV_W].astype(BF16)
        w_gate = w_in[l, :, QKV_W:].astype(BF16)
        qna, kna, vna, qg, kg, vg, qm = _in_proj(
            x2, w_qkv, row(jnp.tile(gqa_q_gain[l], GQA_HEADS)), row(jnp.tile(gqa_k_gain[l], GQA_KV_HEADS)),
            batch, seq)
        mk, mv = _mem_kv(mem.reshape(batch * mem_tokens, d), w_mem_kv[l].astype(BF16))
        yna = _na_attention(qna, kna, vna, _na_bias_tiles(na_rpb[l]), batch, seq)
        ygqa = _gqa_attention(qg, kg, vg, batch, seq)
        w_router_p = jnp.pad(w_router[l], ((0, 0), (0, LANES - N_EXPERTS)))
        x1, aff = _merge(x2, yna, ygqa, qm, mk, mv, w_gate, row(b_gate[l]), w_branch[l].astype(BF16),
                         w_out[l].astype(BF16), row(ln1_g[l]), row(ln1_b[l]), w_router_p, seq, mem_tokens, alpha)
        idx, gate, bnd = _ec_select(aff, batch, seq, cap)
        x1 = x1.reshape(batch, seq, d)
        y = _ec_ffn(idx, x1, w_gate_up[l].astype(BF16), w_down[l].astype(BF16), batch, seq, cap)
        x = _ec_combine(idx, gate, bnd, x1, y, row(ln2_g[l]), row(ln2_b[l]), batch, seq, cap, alpha)
    return x
```

```python
import functools
import math

import jax
import jax.numpy as jnp
from jax import lax
from jax.experimental import pallas as pl
from jax.experimental.pallas import tpu as pltpu

F32 = jnp.float32
BF16 = jnp.bfloat16

D_MODEL = 1024
GRID_W = 64
HEAD_DIM = 64
LANES = 128
NA_HEADS = 8
NA_KH = 8
NA_KW = 16
NA_W = NA_HEADS * HEAD_DIM
NA_QROWS = 4
NA_KROWS = 12
GQA_HEADS = 8
GQA_KV_HEADS = 2
GQA_GROUP = GQA_HEADS // GQA_KV_HEADS
GQA_Q_W = GQA_HEADS * HEAD_DIM
GQA_KV_W = GQA_KV_HEADS * HEAD_DIM
ROPE_THETA = 10000.0
MEM_HEADS = 4
MEM_HEAD_DIM = 128
MEM_W = MEM_HEADS * MEM_HEAD_DIM
N_BRANCH = 3
N_EXPERTS = 16
EC_CAPACITY_FACTOR = 2
LN_EPS = 1e-5
RMS_EPS = 1e-6
NEG_INF = -1e30
QKV_W = 3 * NA_W + GQA_Q_W + 2 * GQA_KV_W + MEM_W

VMEM_LIMIT = 56 * 1024 * 1024


def _cparams(sem, vmem=VMEM_LIMIT):
    return pltpu.CompilerParams(dimension_semantics=sem, vmem_limit_bytes=vmem)


def _dot(a, b):
    return jnp.dot(a, b, preferred_element_type=F32)


def _dot_nt(a, b):
    return lax.dot_general(a, b, (((1,), (1,)), ((), ())), preferred_element_type=F32)


def _layer_norm(v, g, b):
    mu = jnp.mean(v, axis=-1, keepdims=True)
    var = jnp.mean(jnp.square(v - mu), axis=-1, keepdims=True)
    return (v - mu) * lax.rsqrt(var + LN_EPS) * g + b


def _na_bias_kernel(rpb_ref, out_ref):
    n = GRID_W * GRID_W
    col = lax.broadcasted_iota(jnp.int32, (32, n), 1)
    d = lax.broadcasted_iota(jnp.int32, (32, n), 0)
    c, kc = col // GRID_W, col % GRID_W
    onehot = (kc - c + (NA_KW - 1) == d).astype(F32)
    t = jnp.dot(rpb_ref[...], onehot, precision=lax.Precision.HIGHEST, preferred_element_type=F32)
    col1 = lax.broadcasted_iota(jnp.int32, (1, n), 1)
    c1, kc1 = col1 // GRID_W, col1 % GRID_W
    start = jnp.clip(c1 - NA_KW // 2, 0, GRID_W - NA_KW)
    in_win = (kc1 >= start) & (kc1 < start + NA_KW)
    out_ref[...] = jnp.where(in_win, t, NEG_INF)


def _na_bias_tiles(rpb):
    h = rpb.shape[0]
    nd = 2 * NA_KH - 1
    rpb2 = jnp.pad(rpb.reshape(h * nd, 2 * NA_KW - 1), ((0, 0), (0, 1)))
    t = pl.pallas_call(
        _na_bias_kernel,
        out_shape=jax.ShapeDtypeStruct((h * nd, GRID_W * GRID_W), F32),
        name="na_bias",
    )(rpb2).reshape(h, nd, GRID_W, GRID_W)
    masked = jnp.full((h, GRID_W, GRID_W), NEG_INF, F32)
    variants = []
    for variant in range(3):
        rows = []
        for i in range(NA_QROWS):
            blocks = []
            for j in range(NA_KROWS):
                if variant == 0:
                    dr, ok = j - i, j < NA_KH
                elif variant == 1:
                    dr = j - i - NA_KH // 2
                    ok = -(NA_KH // 2) <= dr < NA_KH // 2
                else:
                    dr, ok = j - i - NA_KH, j >= NA_KROWS - NA_KH
                blocks.append(t[:, dr + NA_KH - 1] if ok else masked)
            rows.append(jnp.concatenate(blocks, axis=-1))
        variants.append(jnp.concatenate(rows, axis=-2))
    return jnp.stack(variants, axis=0)


PROJ_TM = 512


def _rope_tables(seq_len):
    t = jnp.arange(seq_len)
    row = (t // GRID_W).astype(F32)
    col = (t % GRID_W).astype(F32)
    half = HEAD_DIM // 2
    inv = ROPE_THETA ** (-jnp.arange(0, half, 2, dtype=F32) / half)
    ang = jnp.concatenate([row[:, None] * inv, col[:, None] * inv], axis=-1)
    cos, sin = jnp.cos(ang), jnp.sin(ang)
    cos_l = jnp.repeat(cos, 2, axis=-1)
    sin_l = jnp.stack([-sin, sin], axis=-1).reshape(seq_len, HEAD_DIM)
    return jnp.tile(cos_l, (1, 2)), jnp.tile(sin_l, (1, 2))


def _head_sums(sq, width):
    r = lax.broadcasted_iota(jnp.int32, (width, width), 0) // HEAD_DIM
    c = lax.broadcasted_iota(jnp.int32, (width, width), 1) // HEAD_DIM
    ones = (r == c).astype(BF16)
    hi = sq.astype(BF16)
    lo = (sq - hi.astype(F32)).astype(BF16)
    return _dot(hi, ones) + _dot(lo, ones)


def _rope(v, cos, sin_signed):
    lane = lax.broadcasted_iota(jnp.int32, v.shape, 1)
    swapped = jnp.where(lane % 2 == 0, pltpu.roll(v, LANES - 1, 1), pltpu.roll(v, 1, 1))
    return v * cos + swapped * sin_signed


def _proj_kernel(x_ref, w_ref, qgain_ref, kgain_ref, cos_ref, sin_ref,
                 qna_ref, kna_ref, vna_ref, qg_ref, kg_ref, vg_ref, qm_ref):
    xb = x_ref[...].astype(BF16)
    scale = 1.0 / math.sqrt(HEAD_DIM)
    o = 0
    qna_ref[...] = (_dot(xb, w_ref[:, o:o + NA_W]) * scale).astype(BF16)
    o += NA_W
    kna_ref[...] = _dot(xb, w_ref[:, o:o + NA_W]).astype(BF16)
    o += NA_W
    vna_ref[...] = _dot(xb, w_ref[:, o:o + NA_W]).astype(BF16)
    o += NA_W
    cos, sin = cos_ref[...], sin_ref[...]
    lane = lax.broadcasted_iota(jnp.int32, (PROJ_TM, LANES), 1)
    low = lane < HEAD_DIM

    zq = _dot(xb, w_ref[:, o:o + GQA_Q_W])
    o += GQA_Q_W
    ms = _head_sums(zq * zq, GQA_Q_W) * (1.0 / HEAD_DIM)
    zq = zq * lax.rsqrt(ms + RMS_EPS) * qgain_ref[...]
    for p in range(GQA_HEADS // 2):
        r = _rope(zq[:, p * LANES:(p + 1) * LANES], cos, sin) * (scale * math.log2(math.e))
        rolled = pltpu.roll(r, HEAD_DIM, 1)
        if (2 * p) // GQA_GROUP == 0:
            even, odd = jnp.where(low, r, 0.0), jnp.where(low, rolled, 0.0)
        else:
            even, odd = jnp.where(low, 0.0, rolled), jnp.where(low, 0.0, r)
        qg_ref[0, 2 * p] = even.astype(BF16)
        qg_ref[0, 2 * p + 1] = odd.astype(BF16)

    zk = _dot(xb, w_ref[:, o:o + GQA_KV_W])
    o += GQA_KV_W
    ms = _head_sums(zk * zk, GQA_KV_W) * (1.0 / HEAD_DIM)
    zk = zk * lax.rsqrt(ms + RMS_EPS) * kgain_ref[...]
    kg_ref[...] = _rope(zk, cos, sin).astype(BF16)
    zv = _dot(xb, w_ref[:, o:o + GQA_KV_W])
    vg_ref[0] = jnp.where(low, zv, 1.0).astype(BF16)
    vg_ref[1] = jnp.where(low, 1.0, zv).astype(BF16)
    o += GQA_KV_W
    qm_ref[...] = _dot(xb, w_ref[:, o:o + MEM_W]).astype(BF16)


def _in_proj(x2, w_qkv, qgain, kgain, batch, seq):
    tokens = x2.shape[0]
    tm = PROJ_TM
    spb = seq // tm
    cos, sin = _rope_tables(seq)
    row = lambda w: pl.BlockSpec((tm, w), lambda i: (i, 0))
    const = lambda shape: pl.BlockSpec(shape, lambda i: (0,) * len(shape))
    tbl = pl.BlockSpec((tm, LANES), lambda i: (i % spb, 0))
    sds = lambda w: jax.ShapeDtypeStruct((tokens, w), BF16)
    return pl.pallas_call(
        _proj_kernel,
        grid=(tokens // tm,),
        in_specs=[row(D_MODEL), const((D_MODEL, QKV_W)), const((1, GQA_Q_W)), const((1, GQA_KV_W)), tbl, tbl],
        out_specs=[row(NA_W), row(NA_W), row(NA_W),
                   pl.BlockSpec((1, GQA_HEADS, tm, LANES), lambda i: (i // spb, 0, i % spb, 0)),
                   row(GQA_KV_W), pl.BlockSpec((GQA_KV_HEADS, tm, LANES), lambda i: (0, i, 0)), row(MEM_W)],
        out_shape=[sds(NA_W), sds(NA_W), sds(NA_W),
                   jax.ShapeDtypeStruct((batch, GQA_HEADS, seq, LANES), BF16),
                   sds(GQA_KV_W), jax.ShapeDtypeStruct((GQA_KV_HEADS, tokens, LANES), BF16), sds(MEM_W)],
        compiler_params=_cparams(("parallel",)),
        name="in_proj",
    )(x2, w_qkv, qgain, kgain, cos, sin)


def _memkv_kernel(m_ref, w_ref, k_ref, v_ref):
    z = _dot(m_ref[...].astype(BF16), w_ref[...])
    k_ref[...] = z[:, :MEM_W].astype(BF16)
    v_ref[...] = z[:, MEM_W:].astype(BF16)


def _mem_kv(mem2, w_kv):
    rows = mem2.shape[0]
    tm = 256
    return pl.pallas_call(
        _memkv_kernel,
        grid=(rows // tm,),
        in_specs=[pl.BlockSpec((tm, D_MODEL), lambda i: (i, 0)),
                  pl.BlockSpec((D_MODEL, 2 * MEM_W), lambda i: (0, 0))],
        out_specs=[pl.BlockSpec((tm, MEM_W), lambda i: (i, 0))] * 2,
        out_shape=[jax.ShapeDtypeStruct((rows, MEM_W), BF16)] * 2,
        compiler_params=_cparams(("parallel",)),
        name="mem_kv",
    )(mem2, w_kv)


NA_TQ = NA_QROWS * GRID_W
NA_TK = NA_KROWS * GRID_W


def _na_kernel(q_ref, k_ref, v_ref, bias_ref, o_ref, *, n_groups):
    rg = pl.program_id(2)
    key_row0 = jnp.clip(NA_QROWS * rg - NA_KH // 2, 0, GRID_W - NA_KROWS)
    start = pl.multiple_of(key_row0 * GRID_W, GRID_W)
    variant = jnp.where(rg == 0, 0, jnp.where(rg == n_groups - 1, 2, 1))
    k = k_ref[pl.ds(start, NA_TK), :]
    v = v_ref[pl.ds(start, NA_TK), :]
    q = q_ref[...]
    lane = lax.broadcasted_iota(jnp.int32, q.shape, 1)
    low = lane < HEAD_DIM
    outs = []
    for hh in range(2):
        qh = jnp.where(low if hh == 0 else jnp.logical_not(low), q, jnp.zeros_like(q))
        s = _dot_nt(qh, k) + bias_ref[variant, hh]
        m = jnp.max(s, axis=-1, keepdims=True)
        p = jnp.exp(s - m)
        l = jnp.sum(p, axis=-1, keepdims=True)
        outs.append(_dot(p.astype(BF16), v) / l)
    o_ref[...] = jnp.where(low, outs[0], outs[1]).astype(BF16)


def _na_attention(qna, kna, vna, bias, batch, seq):
    n_groups = seq // NA_TQ
    pairs = NA_HEADS // 2
    kv = pl.BlockSpec((seq, LANES), lambda p, b, g: (b, p))
    qo = pl.BlockSpec((NA_TQ, LANES), lambda p, b, g: (b * n_groups + g, p))
    return pl.pallas_call(
        functools.partial(_na_kernel, n_groups=n_groups),
        grid=(pairs, batch, n_groups),
        in_specs=[qo, kv, kv, pl.BlockSpec((3, 2, NA_TQ, NA_TK), lambda p, b, g: (0, p, 0, 0))],
        out_specs=qo,
        out_shape=jax.ShapeDtypeStruct(qna.shape, BF16),
        compiler_params=_cparams(("parallel", "parallel", "parallel")),
        name="na_attn",
    )(qna, kna, vna, bias)


GQA_TQ = 128
GQA_TK = 512


def _gqa_kernel(q_ref, k_ref, v_ref, o_ref, s_sc, mx_sc, acc_sc, *, seq):
    rows = GQA_GROUP * GQA_TQ
    n_chunks = seq // GQA_TK
    tiles = GQA_TK // LANES
    groups = range(GQA_KV_HEADS)
    lane = lax.broadcasted_iota(jnp.int32, (GQA_TQ, LANES), 1)
    low = lane < HEAD_DIM
    qs = [q_ref[0, g * GQA_GROUP:(g + 1) * GQA_GROUP].reshape(rows, LANES) for g in groups]
    mx_sc[...] = jnp.full(mx_sc.shape, -jnp.inf, F32)

    def scores(c, carry):
        k = k_ref[pl.ds(pl.multiple_of(c * GQA_TK, GQA_TK), GQA_TK), :]
        for g in groups:
            s = _dot_nt(qs[g], k)
            s_sc[g, c] = s
            part = s[:, :LANES]
            for j in range(1, tiles):
                part = jnp.maximum(part, s[:, j * LANES:(j + 1) * LANES])
            mx_sc[g] = jnp.maximum(mx_sc[g], part)
        return carry

    lax.fori_loop(0, n_chunks, scores, 0)
    ms = [jnp.broadcast_to(jnp.max(mx_sc[g], axis=-1, keepdims=True), (rows, LANES)) for g in groups]
    acc_sc[...] = jnp.zeros(acc_sc.shape, F32)

    def weigh(c, carry):
        off = pl.multiple_of(c * GQA_TK, GQA_TK)
        for g in groups:
            p = [jnp.exp2(s_sc[g, c, :, j * LANES:(j + 1) * LANES] - ms[g]).astype(BF16) for j in range(tiles)]
            acc_sc[g] += _dot(jnp.concatenate(p, axis=-1), v_ref[g, pl.ds(off, GQA_TK), :])
        return carry

    lax.fori_loop(0, n_chunks, weigh, 0)
    for g in groups:
        acc = acc_sc[g]
        out = acc / pltpu.roll(acc, HEAD_DIM, 1)
        for pp in range(GQA_GROUP // 2):
            a = out[(2 * pp) * GQA_TQ:(2 * pp + 1) * GQA_TQ]
            b = out[(2 * pp + 1) * GQA_TQ:(2 * pp + 2) * GQA_TQ]
            if g == 0:
                pair = jnp.where(low, a, pltpu.roll(b, HEAD_DIM, 1))
            else:
                pair = jnp.where(low, pltpu.roll(a, HEAD_DIM, 1), b)
            col = (g * (GQA_GROUP // 2) + pp) * LANES
            o_ref[:, col:col + LANES] = pair.astype(BF16)


def _gqa_attention(qg, kg, vg, batch, seq):
    nq = seq // GQA_TQ
    rows = GQA_GROUP * GQA_TQ
    stat = pltpu.VMEM((GQA_KV_HEADS, rows, LANES), F32)
    return pl.pallas_call(
        functools.partial(_gqa_kernel, seq=seq),
        grid=(batch, nq),
        in_specs=[pl.BlockSpec((1, GQA_HEADS, GQA_TQ, LANES), lambda b, i: (b, 0, i, 0)),
                  pl.BlockSpec((seq, LANES), lambda b, i: (b, 0)),
                  pl.BlockSpec((GQA_KV_HEADS, seq, LANES), lambda b, i: (0, b, 0))],
        out_specs=pl.BlockSpec((GQA_TQ, GQA_Q_W), lambda b, i: (b * nq + i, 0)),
        out_shape=jax.ShapeDtypeStruct((batch * seq, GQA_Q_W), BF16),
        scratch_shapes=[pltpu.VMEM((GQA_KV_HEADS, seq // GQA_TK, rows, GQA_TK), F32), stat, stat],
        compiler_params=_cparams(("parallel", "parallel")),
        name="gqa_attn",
    )(qg, kg, vg)


MERGE_TM = 256


def _merge_kernel(x_ref, yna_ref, ygqa_ref, qm_ref, mk_ref, mv_ref, wg_ref, bg_ref, wb_ref, wo_ref,
                  g1_ref, b1_ref, wr_ref, x1_ref, x1p_ref, aff_ref, *, alpha):
    x = x_ref[...]
    xb = x.astype(BF16)
    mem_scale = 1.0 / math.sqrt(MEM_HEAD_DIM)
    ymem = []
    for h in range(MEM_HEADS):
        sl = slice(h * MEM_HEAD_DIM, (h + 1) * MEM_HEAD_DIM)
        s = _dot_nt(qm_ref[:, sl], mk_ref[:, sl]) * mem_scale
        m = jnp.max(s, axis=-1, keepdims=True)
        p = jnp.exp(s - m)
        l = jnp.sum(p, axis=-1, keepdims=True)
        ymem.append((_dot(p.astype(BF16), mv_ref[:, sl]) / l).astype(BF16))
    branches = (yna_ref[...], ygqa_ref[...], jnp.concatenate(ymem, axis=-1))
    merged = jnp.zeros((MERGE_TM, D_MODEL), F32)
    for g in range(N_BRANCH):
        cols = slice(g * D_MODEL, (g + 1) * D_MODEL)
        gate = jax.nn.sigmoid(_dot(xb, wg_ref[:, cols]) + bg_ref[:, cols])
        merged = merged + gate * _dot(branches[g], wb_ref[g])
    mix = _dot(merged.astype(BF16), wo_ref[...])
    x1 = _layer_norm(alpha * x + mix, g1_ref[...], b1_ref[...])
    x1_ref[...] = x1
    half = D_MODEL // 2
    lo = pltpu.bitcast(x1[:, :half].astype(BF16).astype(F32), jnp.uint32)
    hi = pltpu.bitcast(x1[:, half:].astype(BF16).astype(F32), jnp.uint32)
    x1p_ref[...] = (hi & jnp.uint32(0xFFFF0000)) | (lo >> 16)
    logits = lax.dot_general(wr_ref[...], x1, (((1,), (1,)), ((), ())),
                             precision=lax.Precision.HIGHEST, preferred_element_type=F32)
    m = jnp.max(logits, axis=0, keepdims=True)
    e = jnp.exp(logits - m)
    aff_ref[0] = e / jnp.sum(e, axis=0, keepdims=True)


def _merge(x2, yna, ygqa, qm, mk, mv, w_gate, b_gate, w_branch, w_out, g1, b1, w_router, seq, mem_tokens, alpha):
    tokens = x2.shape[0]
    tm = MERGE_TM
    tpb = seq // tm
    row = lambda w: pl.BlockSpec((tm, w), lambda i: (i, 0))
    const = lambda shape: pl.BlockSpec(shape, lambda i: (0,) * len(shape))
    memb = pl.BlockSpec((mem_tokens, MEM_W), lambda i: (i // tpb, 0))
    return pl.pallas_call(
        functools.partial(_merge_kernel, alpha=alpha),
        grid=(tokens // tm,),
        in_specs=[row(D_MODEL), row(NA_W), row(GQA_Q_W), row(MEM_W), memb, memb,
                  const(w_gate.shape), const(b_gate.shape), const(w_branch.shape), const(w_out.shape),
                  const(g1.shape), const(b1.shape), const(w_router.shape)],
        out_specs=[row(D_MODEL), row(D_MODEL // 2),
                   pl.BlockSpec((1, N_EXPERTS, tm), lambda i: (i // tpb, 0, i % tpb))],
        out_shape=[jax.ShapeDtypeStruct((tokens, D_MODEL), F32),
                   jax.ShapeDtypeStruct((tokens, D_MODEL // 2), jnp.uint32),
                   jax.ShapeDtypeStruct((tokens // seq, N_EXPERTS, seq), F32)],
        compiler_params=_cparams(("parallel",)),
        name="merge",
    )(x2, yna, ygqa, qm, mk, mv, w_gate, b_gate, w_branch, w_out, g1, b1, w_router)


SEL_BLK = 256
SLOT_HI = 16
SLOT_LO = 32
SEL_PARTS = 5


def _prefix_counts(mask, out_ref, seq):
    r = lax.broadcasted_iota(jnp.int32, (SEL_BLK, SEL_BLK), 0)
    c = lax.broadcasted_iota(jnp.int32, (SEL_BLK, SEL_BLK), 1)
    tri = (r <= c).astype(BF16)
    run = jnp.zeros((mask.shape[0], 1), F32)
    for j in range(seq // SEL_BLK):
        cs = _dot(mask[:, j * SEL_BLK:(j + 1) * SEL_BLK].astype(BF16), tri) + run
        out_ref[:, j * SEL_BLK:(j + 1) * SEL_BLK] = cs
        run = cs[:, SEL_BLK - 1:SEL_BLK]


def _select_kernel(aff_ref, idx_ref, gate_ref, cnt_sc, lhs_sc, rhs_sc, *, seq, cap):
    a = aff_ref[0]
    bits = pltpu.bitcast(a, jnp.int32)
    thr = jnp.zeros((N_EXPERTS, 1), jnp.int32)
    for bit in range(30, -1, -1):
        cand = thr | (1 << bit)
        cnt = jnp.sum((bits >= cand).astype(F32), axis=-1, keepdims=True)
        thr = jnp.where(cnt >= cap, cand, thr)
    gt = bits > thr
    eq = bits == thr
    need = cap - jnp.sum(gt.astype(F32), axis=-1, keepdims=True)
    _prefix_counts(eq.astype(F32), cnt_sc, seq)
    sel = gt | (eq & (cnt_sc[...] <= need))
    _prefix_counts(sel.astype(F32), cnt_sc, seq)

    slot = cnt_sc[...] - 1.0
    s_hi = jnp.floor(slot * (1.0 / SLOT_LO))
    s_lo = slot - SLOT_LO * s_hi
    tok = lax.broadcasted_iota(jnp.int32, (1, seq), 1).astype(F32)
    t_hi = jnp.floor(tok * (1.0 / 64.0))
    t_lo = tok - 64.0 * t_hi
    ph = lax.broadcasted_iota(jnp.int32, (SLOT_HI, seq), 0).astype(F32)
    pl_ = lax.broadcasted_iota(jnp.int32, (SLOT_LO, seq), 0).astype(F32)
    for e in range(N_EXPERTS):
        row = slice(e, e + 1)
        onehot_hi = jnp.where((s_hi[row] == ph) & sel[row], 1.0, 0.0)
        g = a[row]
        g1 = g.astype(BF16).astype(F32)
        g2 = (g - g1).astype(BF16).astype(F32)
        g3 = g - g1 - g2
        for part, val in enumerate((t_hi, t_lo, g1, g2, g3)):
            r0 = (part * N_EXPERTS + e) * SLOT_HI
            lhs_sc[r0:r0 + SLOT_HI, :] = (onehot_hi * val).astype(BF16)
        rhs_sc[e * SLOT_LO:(e + 1) * SLOT_LO, :] = jnp.where(s_lo[row] == pl_, 1.0, 0.0).astype(BF16)
    res = _dot_nt(lhs_sc[...], rhs_sc[...])
    n = N_EXPERTS * SLOT_HI
    r = lax.broadcasted_iota(jnp.int32, (n, N_EXPERTS * SLOT_LO), 0) // SLOT_HI
    c = lax.broadcasted_iota(jnp.int32, (n, N_EXPERTS * SLOT_LO), 1) // SLOT_LO
    parts = []
    for part in range(SEL_PARTS):
        blk = jnp.where(r == c, res[part * n:(part + 1) * n], 0.0)
        parts.append(jnp.sum(blk.reshape(N_EXPERTS, SLOT_HI, N_EXPERTS * SLOT_LO), axis=0))
    idx_ref[0] = (64.0 * parts[0] + parts[1]).astype(jnp.int32)
    gate_ref[0] = (parts[2] + parts[3]) + parts[4]


def _ec_select(aff, batch, seq, cap):
    assert cap == SLOT_HI * SLOT_LO
    out = pl.BlockSpec((1, SLOT_HI, N_EXPERTS * SLOT_LO), lambda b: (b, 0, 0))
    idx_t, gate_t = pl.pallas_call(
        functools.partial(_select_kernel, seq=seq, cap=cap),
        grid=(batch,),
        in_specs=[pl.BlockSpec((1, N_EXPERTS, seq), lambda b: (b, 0, 0))],
        out_specs=[out, out],
        out_shape=[jax.ShapeDtypeStruct((batch, SLOT_HI, N_EXPERTS * SLOT_LO), jnp.int32),
                   jax.ShapeDtypeStruct((batch, SLOT_HI, N_EXPERTS * SLOT_LO), F32)],
        scratch_shapes=[pltpu.VMEM((N_EXPERTS, seq), F32),
                        pltpu.VMEM((SEL_PARTS * N_EXPERTS * SLOT_HI, seq), BF16),
                        pltpu.VMEM((N_EXPERTS * SLOT_LO, seq), BF16)],
        compiler_params=_cparams(("parallel",)),
        name="ec_select",
    )(aff)
    order = lambda v: v.reshape(batch, SLOT_HI, N_EXPERTS, SLOT_LO).transpose(0, 2, 1, 3).reshape(batch, N_EXPERTS, cap)
    return order(idx_t), order(gate_t)


FFN_UNROLL = 8


def _ffn_kernel(idx_ref, gate_ref, xp_ref, wgu_ref, wd_ref, out_hbm, u_sc, ye_sc, acc_sc, sem, *, cap, d_ff):
    b = pl.program_id(0)
    e = pl.program_id(1)

    @pl.when(e == 0)
    def _():
        acc_sc[...] = jnp.zeros(acc_sc.shape, F32)

    def gather(i, carry):
        for j in range(FFN_UNROLL):
            p = i * FFN_UNROLL + j
            u_sc[pl.ds(p, 1), :] = xp_ref[0, pl.ds(idx_ref[0, 0, p], 1), :]
        return carry

    lax.fori_loop(0, cap // FFN_UNROLL, gather, 0)
    u = u_sc[...]
    lo = pltpu.bitcast(u << 16, F32).astype(BF16)
    hi = pltpu.bitcast(u & jnp.uint32(0xFFFF0000), F32).astype(BF16)
    xs = jnp.concatenate([lo, hi], axis=-1)
    h = _dot(xs, wgu_ref[0])
    hg, hu = h[:, :d_ff], h[:, d_ff:]
    act = (hg * jax.nn.sigmoid(hg) * hu).astype(BF16)
    ye_sc[...] = _dot(act, wd_ref[0])

    def scatter(i, carry):
        ps = [i * FFN_UNROLL + j for j in range(FFN_UNROLL)]
        ts = [idx_ref[0, 0, p] for p in ps]
        rows = [acc_sc[pl.ds(t, 1), :] + ye_sc[pl.ds(p, 1), :] * gate_ref[0, 0, p] for p, t in zip(ps, ts)]
        for t, row in zip(ts, rows):
            acc_sc[pl.ds(t, 1), :] = row
        return carry

    lax.fori_loop(0, cap // FFN_UNROLL, scatter, 0)

    @pl.when(e == N_EXPERTS - 1)
    def _():
        copy = pltpu.make_async_copy(acc_sc, out_hbm.at[b], sem)
        copy.start()
        copy.wait()


def _ec_ffn(idx, gate, x1p, wgu, wd, batch, seq, cap):
    d_ff = wd.shape[1]
    slot = pl.BlockSpec((1, 1, cap), lambda b, e: (b * N_EXPERTS + e, 0, 0), memory_space=pltpu.SMEM)
    return pl.pallas_call(
        functools.partial(_ffn_kernel, cap=cap, d_ff=d_ff),
        grid=(batch, N_EXPERTS),
        in_specs=[slot, slot,
                  pl.BlockSpec((1, seq, D_MODEL // 2), lambda b, e: (b, 0, 0), pipeline_mode=pl.Buffered(1)),
                  pl.BlockSpec((1, D_MODEL, 2 * d_ff), lambda b, e: (e, 0, 0)),
                  pl.BlockSpec((1, d_ff, D_MODEL), lambda b, e: (e, 0, 0))],
        out_specs=pl.BlockSpec(memory_space=pl.ANY),
        out_shape=jax.ShapeDtypeStruct((batch, seq, D_MODEL), F32),
        scratch_shapes=[pltpu.VMEM((cap, D_MODEL // 2), jnp.uint32), pltpu.VMEM((cap, D_MODEL), F32),
                        pltpu.VMEM((seq, D_MODEL), F32), pltpu.SemaphoreType.DMA(())],
        compiler_params=_cparams(("arbitrary", "arbitrary")),
        name="ec_ffn",
    )(idx.reshape(batch * N_EXPERTS, 1, cap), gate.reshape(batch * N_EXPERTS, 1, cap), x1p, wgu, wd)


LN2_TM = 1024


def _ln2_kernel(x_ref, f_ref, g_ref, b_ref, o_ref, *, alpha):
    o_ref[...] = _layer_norm(alpha * x_ref[...] + f_ref[...], g_ref[...], b_ref[...])


def _residual_ln(x1, ffn, g2, b2, alpha):
    tokens = x1.shape[0]
    row = pl.BlockSpec((LN2_TM, D_MODEL), lambda i: (i, 0))
    const = pl.BlockSpec((1, D_MODEL), lambda i: (0, 0))
    return pl.pallas_call(
        functools.partial(_ln2_kernel, alpha=alpha),
        grid=(tokens // LN2_TM,),
        in_specs=[row, row, const, const],
        out_specs=row,
        out_shape=jax.ShapeDtypeStruct((tokens, D_MODEL), F32),
        compiler_params=_cparams(("parallel",)),
        name="residual_ln",
    )(x1, ffn, g2, b2)


def kernel(x, mem, w_in, b_gate, na_rpb, gqa_q_gain, gqa_k_gain, w_mem_kv, w_branch, w_out,
           ln1_g, ln1_b, w_router, w_gate_up, w_down, ln2_g, ln2_b):
    batch, seq, d = x.shape
    mem_tokens = mem.shape[1]
    depth = w_in.shape[0]
    alpha = (2 * depth) ** 0.25
    cap = EC_CAPACITY_FACTOR * seq // N_EXPERTS
    assert d == D_MODEL and seq == GRID_W * GRID_W and w_in.shape[2] == QKV_W + N_BRANCH * D_MODEL
    row = lambda v: v.reshape(1, -1)
    for l in range(depth):
        x2 = x.reshape(batch * seq, d)
        w_qkv = w_in[l, :, :QKV_W].astype(BF16)
        w_gate = w_in[l, :, QKV_W:].astype(BF16)
        qna, kna, vna, qg, kg, vg, qm = _in_proj(
            x2, w_qkv, row(jnp.tile(gqa_q_gain[l], GQA_HEADS)), row(jnp.tile(gqa_k_gain[l], GQA_KV_HEADS)),
            batch, seq)
        mk, mv = _mem_kv(mem.reshape(batch * mem_tokens, d), w_mem_kv[l].astype(BF16))
        yna = _na_attention(qna, kna, vna, _na_bias_tiles(na_rpb[l]), batch, seq)
        ygqa = _gqa_attention(qg, kg, vg, batch, seq)
        x1, x1p, aff = _merge(x2, yna, ygqa, qm, mk, mv, w_gate, row(b_gate[l]), w_branch[l].astype(BF16),
                              w_out[l].astype(BF16), row(ln1_g[l]), row(ln1_b[l]), w_router[l].T, seq, mem_tokens,
                              alpha)
        idx, gate = _ec_select(aff, batch, seq, cap)
        ffn = _ec_ffn(idx, gate, x1p.reshape(batch, seq, d // 2), w_gate_up[l].astype(BF16),
                      w_down[l].astype(BF16), batch, seq, cap)
        x = _residual_ln(x1, ffn.reshape(batch * seq, d), row(ln2_g[l]), row(ln2_b[l]), alpha).reshape(batch, seq, d)
    return x
```

```python
import functools
import math

import jax
import jax.numpy as jnp
from jax import lax
from jax.experimental import pallas as pl
from jax.experimental.pallas import tpu as pltpu

F32 = jnp.float32
BF16 = jnp.bfloat16

D_MODEL = 1024
GRID_W = 64
HEAD_DIM = 64
LANES = 128
NA_HEADS = 8
NA_KH = 8
NA_KW = 16
NA_W = NA_HEADS * HEAD_DIM
NA_QROWS = 4
NA_KROWS = 12
GQA_HEADS = 8
GQA_KV_HEADS = 2
GQA_GROUP = GQA_HEADS // GQA_KV_HEADS
GQA_Q_W = GQA_HEADS * HEAD_DIM
GQA_KV_W = GQA_KV_HEADS * HEAD_DIM
ROPE_THETA = 10000.0
MEM_HEADS = 4
MEM_HEAD_DIM = 128
MEM_W = MEM_HEADS * MEM_HEAD_DIM
N_BRANCH = 3
N_EXPERTS = 16
EC_CAPACITY_FACTOR = 2
LN_EPS = 1e-5
RMS_EPS = 1e-6
NEG_INF = -1e30
QKV_W = 3 * NA_W + GQA_Q_W + 2 * GQA_KV_W + MEM_W

VMEM_LIMIT = 56 * 1024 * 1024


def _cparams(sem, vmem=VMEM_LIMIT):
    return pltpu.CompilerParams(dimension_semantics=sem, vmem_limit_bytes=vmem)


def _dot(a, b):
    return jnp.dot(a, b, preferred_element_type=F32)


def _dot_nt(a, b):
    return lax.dot_general(a, b, (((1,), (1,)), ((), ())), preferred_element_type=F32)


def _layer_norm(v, g, b):
    mu = jnp.mean(v, axis=-1, keepdims=True)
    var = jnp.mean(jnp.square(v - mu), axis=-1, keepdims=True)
    return (v - mu) * lax.rsqrt(var + LN_EPS) * g + b


def _na_bias_kernel(rpb_ref, out_ref):
    n = GRID_W * GRID_W
    col = lax.broadcasted_iota(jnp.int32, (32, n), 1)
    d = lax.broadcasted_iota(jnp.int32, (32, n), 0)
    c, kc = col // GRID_W, col % GRID_W
    onehot = (kc - c + (NA_KW - 1) == d).astype(F32)
    t = jnp.dot(rpb_ref[...], onehot, precision=lax.Precision.HIGHEST, preferred_element_type=F32)
    col1 = lax.broadcasted_iota(jnp.int32, (1, n), 1)
    c1, kc1 = col1 // GRID_W, col1 % GRID_W
    start = jnp.clip(c1 - NA_KW // 2, 0, GRID_W - NA_KW)
    in_win = (kc1 >= start) & (kc1 < start + NA_KW)
    out_ref[...] = jnp.where(in_win, t * math.log2(math.e), NEG_INF)


def _na_bias_tiles(rpb):
    h = rpb.shape[0]
    nd = 2 * NA_KH - 1
    rpb2 = jnp.pad(rpb.reshape(h * nd, 2 * NA_KW - 1), ((0, 0), (0, 1)))
    t = pl.pallas_call(
        _na_bias_kernel,
        out_shape=jax.ShapeDtypeStruct((h * nd, GRID_W * GRID_W), F32),
        name="na_bias",
    )(rpb2).reshape(h, nd, GRID_W, GRID_W)
    masked = jnp.full((h, GRID_W, GRID_W), NEG_INF, F32)
    variants = []
    for variant in range(3):
        rows = []
        for i in range(NA_QROWS):
            blocks = []
            for j in range(NA_KROWS):
                if variant == 0:
                    dr, ok = j - i, j < NA_KH
                elif variant == 1:
                    dr = j - i - NA_KH // 2
                    ok = -(NA_KH // 2) <= dr < NA_KH // 2
                else:
                    dr, ok = j - i - NA_KH, j >= NA_KROWS - NA_KH
                blocks.append(t[:, dr + NA_KH - 1] if ok else masked)
            rows.append(jnp.concatenate(blocks, axis=-1))
        variants.append(jnp.concatenate(rows, axis=-2))
    return jnp.stack(variants, axis=0)


PROJ_TM = 1024


def _rope_tables(seq_len):
    t = jnp.arange(seq_len)
    row = (t // GRID_W).astype(F32)
    col = (t % GRID_W).astype(F32)
    half = HEAD_DIM // 2
    inv = ROPE_THETA ** (-jnp.arange(0, half, 2, dtype=F32) / half)
    ang = jnp.concatenate([row[:, None] * inv, col[:, None] * inv], axis=-1)
    cos, sin = jnp.cos(ang), jnp.sin(ang)
    cos_l = jnp.repeat(cos, 2, axis=-1)
    sin_l = jnp.stack([-sin, sin], axis=-1).reshape(seq_len, HEAD_DIM)
    return jnp.tile(cos_l, (1, 2)), jnp.tile(sin_l, (1, 2))


def _head_sums(sq, width):
    r = lax.broadcasted_iota(jnp.int32, (width, width), 0) // HEAD_DIM
    c = lax.broadcasted_iota(jnp.int32, (width, width), 1) // HEAD_DIM
    ones = (r == c).astype(BF16)
    hi = sq.astype(BF16)
    lo = (sq - hi.astype(F32)).astype(BF16)
    return _dot(hi, ones) + _dot(lo, ones)


def _rope(v, cos, sin_signed):
    lane = lax.broadcasted_iota(jnp.int32, v.shape, 1)
    swapped = jnp.where(lane % 2 == 0, pltpu.roll(v, LANES - 1, 1), pltpu.roll(v, 1, 1))
    return v * cos + swapped * sin_signed


def _proj_kernel(x_ref, w_ref, qgain_ref, kgain_ref, cos_ref, sin_ref,
                 qna_ref, kna_ref, vna_ref, qg_ref, kg_ref, vg_ref, qm_ref):
    xb = x_ref[...].astype(BF16)
    qscale = math.log2(math.e) / math.sqrt(HEAD_DIM)
    o = 0
    qna_ref[...] = (_dot(xb, w_ref[:, o:o + NA_W]) * qscale).astype(BF16)
    o += NA_W
    kna_ref[...] = _dot(xb, w_ref[:, o:o + NA_W]).astype(BF16)
    o += NA_W
    vna_ref[...] = _dot(xb, w_ref[:, o:o + NA_W]).astype(BF16)
    o += NA_W
    cos, sin = cos_ref[...], sin_ref[...]
    lane = lax.broadcasted_iota(jnp.int32, (PROJ_TM, LANES), 1)
    low = lane < HEAD_DIM

    zq = _dot(xb, w_ref[:, o:o + GQA_Q_W])
    o += GQA_Q_W
    ms = _head_sums(zq * zq, GQA_Q_W) * (1.0 / HEAD_DIM)
    zq = zq * lax.rsqrt(ms + RMS_EPS) * qgain_ref[...]
    for p in range(GQA_HEADS // 2):
        r = _rope(zq[:, p * LANES:(p + 1) * LANES], cos, sin) * qscale
        rolled = pltpu.roll(r, HEAD_DIM, 1)
        if (2 * p) // GQA_GROUP == 0:
            even, odd = jnp.where(low, r, 0.0), jnp.where(low, rolled, 0.0)
        else:
            even, odd = jnp.where(low, 0.0, rolled), jnp.where(low, 0.0, r)
        qg_ref[0, 2 * p] = even.astype(BF16)
        qg_ref[0, 2 * p + 1] = odd.astype(BF16)

    zk = _dot(xb, w_ref[:, o:o + GQA_KV_W])
    o += GQA_KV_W
    ms = _head_sums(zk * zk, GQA_KV_W) * (1.0 / HEAD_DIM)
    zk = zk * lax.rsqrt(ms + RMS_EPS) * kgain_ref[...]
    kg_ref[...] = _rope(zk, cos, sin).astype(BF16)
    zv = _dot(xb, w_ref[:, o:o + GQA_KV_W])
    vg_ref[0] = jnp.where(low, zv, 1.0).astype(BF16)
    vg_ref[1] = jnp.where(low, 1.0, zv).astype(BF16)
    o += GQA_KV_W
    qm_ref[...] = _dot(xb, w_ref[:, o:o + MEM_W]).astype(BF16)


def _in_proj(x2, w_qkv, qgain, kgain, batch, seq):
    tokens = x2.shape[0]
    tm = PROJ_TM
    spb = seq // tm
    cos, sin = _rope_tables(seq)
    row = lambda w: pl.BlockSpec((tm, w), lambda i: (i, 0))
    const = lambda shape: pl.BlockSpec(shape, lambda i: (0,) * len(shape), pipeline_mode=pl.Buffered(1))
    tbl = pl.BlockSpec((tm, LANES), lambda i: (i % spb, 0))
    sds = lambda w: jax.ShapeDtypeStruct((tokens, w), BF16)
    return pl.pallas_call(
        _proj_kernel,
        grid=(tokens // tm,),
        in_specs=[row(D_MODEL), const((D_MODEL, QKV_W)), const((1, GQA_Q_W)), const((1, GQA_KV_W)), tbl, tbl],
        out_specs=[row(NA_W), row(NA_W), row(NA_W),
                   pl.BlockSpec((1, GQA_HEADS, tm, LANES), lambda i: (i // spb, 0, i % spb, 0)),
                   row(GQA_KV_W), pl.BlockSpec((GQA_KV_HEADS, tm, LANES), lambda i: (0, i, 0)), row(MEM_W)],
        out_shape=[sds(NA_W), sds(NA_W), sds(NA_W),
                   jax.ShapeDtypeStruct((batch, GQA_HEADS, seq, LANES), BF16),
                   sds(GQA_KV_W), jax.ShapeDtypeStruct((GQA_KV_HEADS, tokens, LANES), BF16), sds(MEM_W)],
        compiler_params=_cparams(("parallel",)),
        name="in_proj",
    )(x2, w_qkv, qgain, kgain, cos, sin)


def _memkv_kernel(m_ref, w_ref, k_ref, v_ref):
    z = _dot(m_ref[...].astype(BF16), w_ref[...])
    k_ref[...] = z[:, :MEM_W].astype(BF16)
    v_ref[...] = z[:, MEM_W:].astype(BF16)


def _mem_kv(mem2, w_kv):
    rows = mem2.shape[0]
    tm = 256
    return pl.pallas_call(
        _memkv_kernel,
        grid=(rows // tm,),
        in_specs=[pl.BlockSpec((tm, D_MODEL), lambda i: (i, 0)),
                  pl.BlockSpec((D_MODEL, 2 * MEM_W), lambda i: (0, 0))],
        out_specs=[pl.BlockSpec((tm, MEM_W), lambda i: (i, 0))] * 2,
        out_shape=[jax.ShapeDtypeStruct((rows, MEM_W), BF16)] * 2,
        compiler_params=_cparams(("parallel",)),
        name="mem_kv",
    )(mem2, w_kv)


NA_TQ = NA_QROWS * GRID_W
NA_TK = NA_KROWS * GRID_W


NA_GROUPS_PER_STEP = 2


def _na_kernel(q_ref, k_ref, v_ref, bias_ref, o_ref, *, n_groups):
    low = lax.broadcasted_iota(jnp.int32, (NA_TQ, LANES), 1) < HEAD_DIM
    low_k = lax.broadcasted_iota(jnp.int32, (NA_TK, LANES), 1) < HEAD_DIM
    tiles = NA_TK // LANES
    for sub in range(NA_GROUPS_PER_STEP):
        rg = pl.program_id(2) * NA_GROUPS_PER_STEP + sub
        key_row0 = jnp.clip(NA_QROWS * rg - NA_KH // 2, 0, GRID_W - NA_KROWS)
        start = pl.multiple_of(key_row0 * GRID_W, GRID_W)
        variant = jnp.where(rg == 0, 0, jnp.where(rg == n_groups - 1, 2, 1))
        k = k_ref[pl.ds(start, NA_TK), :]
        v = v_ref[pl.ds(start, NA_TK), :]
        q = q_ref[sub * NA_TQ:(sub + 1) * NA_TQ, :]
        outs = []
        for hh in range(2):
            mine, mine_k = (low, low_k) if hh == 0 else (jnp.logical_not(low), jnp.logical_not(low_k))
            qh = jnp.where(mine, q, jnp.zeros_like(q))
            vh = jnp.where(mine_k, v, jnp.ones_like(v))
            s = _dot_nt(qh, k) + bias_ref[variant, hh]
            st = [s[:, j * LANES:(j + 1) * LANES] for j in range(tiles)]
            part = st[0]
            for t in st[1:]:
                part = jnp.maximum(part, t)
            m = jnp.broadcast_to(jnp.max(part, axis=-1, keepdims=True), part.shape)
            p = jnp.concatenate([jnp.exp2(t - m).astype(BF16) for t in st], axis=-1)
            acc = _dot(p, vh)
            outs.append(acc / pltpu.roll(acc, HEAD_DIM, 1))
        o_ref[sub * NA_TQ:(sub + 1) * NA_TQ, :] = jnp.where(low, outs[0], outs[1]).astype(BF16)


def _na_attention(qna, kna, vna, bias, batch, seq):
    n_groups = seq // NA_TQ
    steps = n_groups // NA_GROUPS_PER_STEP
    pairs = NA_HEADS // 2
    kv = pl.BlockSpec((seq, LANES), lambda p, b, g: (b, p))
    qo = pl.BlockSpec((NA_GROUPS_PER_STEP * NA_TQ, LANES), lambda p, b, g: (b * steps + g, p))
    return pl.pallas_call(
        functools.partial(_na_kernel, n_groups=n_groups),
        grid=(pairs, batch, steps),
        in_specs=[qo, kv, kv, pl.BlockSpec((3, 2, NA_TQ, NA_TK), lambda p, b, g: (0, p, 0, 0))],
        out_specs=qo,
        out_shape=jax.ShapeDtypeStruct(qna.shape, BF16),
        compiler_params=_cparams(("parallel", "parallel", "parallel")),
        name="na_attn",
    )(qna, kna, vna, bias)


GQA_TQ = 128
GQA_TK = 1024


def _gqa_kernel(q_ref, k_ref, v_ref, o_ref, s_sc, mx_sc, acc_sc, *, seq):
    rows = GQA_GROUP * GQA_TQ
    n_chunks = seq // GQA_TK
    tiles = GQA_TK // LANES
    groups = range(GQA_KV_HEADS)
    lane = lax.broadcasted_iota(jnp.int32, (GQA_TQ, LANES), 1)
    low = lane < HEAD_DIM
    qs = [q_ref[0, g * GQA_GROUP:(g + 1) * GQA_GROUP].reshape(rows, LANES) for g in groups]
    mx_sc[...] = jnp.full(mx_sc.shape, -jnp.inf, F32)

    for c in range(n_chunks):
        k = k_ref[c * GQA_TK:(c + 1) * GQA_TK, :]
        for g in groups:
            s = _dot_nt(qs[g], k)
            s_sc[g, c] = s
            part = s[:, :LANES]
            for j in range(1, tiles):
                part = jnp.maximum(part, s[:, j * LANES:(j + 1) * LANES])
            mx_sc[g] = jnp.maximum(mx_sc[g], part)
    ms = [jnp.broadcast_to(jnp.max(mx_sc[g], axis=-1, keepdims=True), (rows, LANES)) for g in groups]
    acc_sc[...] = jnp.zeros(acc_sc.shape, F32)

    for c in range(n_chunks):
        for g in groups:
            p = [jnp.exp2(s_sc[g, c, :, j * LANES:(j + 1) * LANES] - ms[g]).astype(BF16) for j in range(tiles)]
            acc_sc[g] += _dot(jnp.concatenate(p, axis=-1), v_ref[g, c * GQA_TK:(c + 1) * GQA_TK, :])
    for g in groups:
        acc = acc_sc[g]
        out = acc / pltpu.roll(acc, HEAD_DIM, 1)
        for pp in range(GQA_GROUP // 2):
            a = out[(2 * pp) * GQA_TQ:(2 * pp + 1) * GQA_TQ]
            b = out[(2 * pp + 1) * GQA_TQ:(2 * pp + 2) * GQA_TQ]
            if g == 0:
                pair = jnp.where(low, a, pltpu.roll(b, HEAD_DIM, 1))
            else:
                pair = jnp.where(low, pltpu.roll(a, HEAD_DIM, 1), b)
            col = (g * (GQA_GROUP // 2) + pp) * LANES
            o_ref[:, col:col + LANES] = pair.astype(BF16)


def _gqa_attention(qg, kg, vg, batch, seq):
    nq = seq // GQA_TQ
    rows = GQA_GROUP * GQA_TQ
    stat = pltpu.VMEM((GQA_KV_HEADS, rows, LANES), F32)
    return pl.pallas_call(
        functools.partial(_gqa_kernel, seq=seq),
        grid=(batch, nq),
        in_specs=[pl.BlockSpec((1, GQA_HEADS, GQA_TQ, LANES), lambda b, i: (b, 0, i, 0)),
                  pl.BlockSpec((seq, LANES), lambda b, i: (b, 0)),
                  pl.BlockSpec((GQA_KV_HEADS, seq, LANES), lambda b, i: (0, b, 0))],
        out_specs=pl.BlockSpec((GQA_TQ, GQA_Q_W), lambda b, i: (b * nq + i, 0)),
        out_shape=jax.ShapeDtypeStruct((batch * seq, GQA_Q_W), BF16),
        scratch_shapes=[pltpu.VMEM((GQA_KV_HEADS, seq // GQA_TK, rows, GQA_TK), F32), stat, stat],
        compiler_params=_cparams(("parallel", "parallel")),
        name="gqa_attn",
    )(qg, kg, vg)


MERGE_TM = 512


def _merge_kernel(x_ref, yna_ref, ygqa_ref, qm_ref, mk_ref, mv_ref, wg_ref, bg_ref, wb_ref, wo_ref,
                  g1_ref, b1_ref, wr_ref, x1_ref, x1p_ref, aff_ref, *, alpha):
    x = x_ref[...]
    xb = x.astype(BF16)
    mem_scale = 1.0 / math.sqrt(MEM_HEAD_DIM)
    ymem = []
    for h in range(MEM_HEADS):
        sl = slice(h * MEM_HEAD_DIM, (h + 1) * MEM_HEAD_DIM)
        s = _dot_nt(qm_ref[:, sl], mk_ref[:, sl]) * mem_scale
        m = jnp.max(s, axis=-1, keepdims=True)
        p = jnp.exp(s - m)
        l = jnp.sum(p, axis=-1, keepdims=True)
        ymem.append((_dot(p.astype(BF16), mv_ref[:, sl]) / l).astype(BF16))
    branches = (yna_ref[...], ygqa_ref[...], jnp.concatenate(ymem, axis=-1))
    merged = jnp.zeros((MERGE_TM, D_MODEL), F32)
    for g in range(N_BRANCH):
        cols = slice(g * D_MODEL, (g + 1) * D_MODEL)
        gate = jax.nn.sigmoid(_dot(xb, wg_ref[:, cols]) + bg_ref[:, cols])
        merged = merged + gate * _dot(branches[g], wb_ref[g])
    mix = _dot(merged.astype(BF16), wo_ref[...])
    x1 = _layer_norm(alpha * x + mix, g1_ref[...], b1_ref[...])
    x1_ref[...] = x1
    half = D_MODEL // 2
    lo = pltpu.bitcast(x1[:, :half].astype(BF16).astype(F32), jnp.uint32)
    hi = pltpu.bitcast(x1[:, half:].astype(BF16).astype(F32), jnp.uint32)
    x1p_ref[...] = (hi & jnp.uint32(0xFFFF0000)) | (lo >> 16)
    logits = lax.dot_general(wr_ref[...], x1, (((1,), (1,)), ((), ())),
                             precision=lax.Precision.HIGHEST, preferred_element_type=F32)
    m = jnp.max(logits, axis=0, keepdims=True)
    e = jnp.exp(logits - m)
    aff_ref[0] = e / jnp.sum(e, axis=0, keepdims=True)


def _merge(x2, yna, ygqa, qm, mk, mv, w_gate, b_gate, w_branch, w_out, g1, b1, w_router, seq, mem_tokens, alpha):
    tokens = x2.shape[0]
    tm = MERGE_TM
    tpb = seq // tm
    row = lambda w: pl.BlockSpec((tm, w), lambda i: (i, 0))
    const = lambda shape: pl.BlockSpec(shape, lambda i: (0,) * len(shape), pipeline_mode=pl.Buffered(1))
    memb = pl.BlockSpec((mem_tokens, MEM_W), lambda i: (i // tpb, 0))
    return pl.pallas_call(
        functools.partial(_merge_kernel, alpha=alpha),
        grid=(tokens // tm,),
        in_specs=[row(D_MODEL), row(NA_W), row(GQA_Q_W), row(MEM_W), memb, memb,
                  const(w_gate.shape), const(b_gate.shape), const(w_branch.shape), const(w_out.shape),
                  const(g1.shape), const(b1.shape), const(w_router.shape)],
        out_specs=[row(D_MODEL), row(D_MODEL // 2),
                   pl.BlockSpec((1, N_EXPERTS, tm), lambda i: (i // tpb, 0, i % tpb))],
        out_shape=[jax.ShapeDtypeStruct((tokens, D_MODEL), F32),
                   jax.ShapeDtypeStruct((tokens, D_MODEL // 2), jnp.uint32),
                   jax.ShapeDtypeStruct((tokens // seq, N_EXPERTS, seq), F32)],
        compiler_params=_cparams(("parallel",)),
        name="merge",
    )(x2, yna, ygqa, qm, mk, mv, w_gate, b_gate, w_branch, w_out, g1, b1, w_router)


SEL_BLK = 256
SLOT_HI = 16
SLOT_LO = 32
SEL_PARTS = 5


def _prefix_counts(mask, out_ref, seq):
    r = lax.broadcasted_iota(jnp.int32, (SEL_BLK, SEL_BLK), 0)
    c = lax.broadcasted_iota(jnp.int32, (SEL_BLK, SEL_BLK), 1)
    tri = (r <= c).astype(BF16)
    run = jnp.zeros((mask.shape[0], 1), F32)
    for j in range(seq // SEL_BLK):
        cs = _dot(mask[:, j * SEL_BLK:(j + 1) * SEL_BLK].astype(BF16), tri) + run
        out_ref[:, j * SEL_BLK:(j + 1) * SEL_BLK] = cs
        run = cs[:, SEL_BLK - 1:SEL_BLK]


def _select_kernel(aff_ref, idx_ref, gate_ref, cnt_sc, lhs_sc, rhs_sc, *, seq, cap):
    a = aff_ref[0]
    bits = pltpu.bitcast(a, jnp.int32)
    thr = jnp.zeros((N_EXPERTS, 1), jnp.int32)
    for bit in range(30, -1, -1):
        cand = thr | (1 << bit)
        cnt = jnp.sum((bits >= cand).astype(F32), axis=-1, keepdims=True)
        thr = jnp.where(cnt >= cap, cand, thr)
    gt = bits > thr
    eq = bits == thr
    need = cap - jnp.sum(gt.astype(F32), axis=-1, keepdims=True)
    _prefix_counts(eq.astype(F32), cnt_sc, seq)
    sel = gt | (eq & (cnt_sc[...] <= need))
    _prefix_counts(sel.astype(F32), cnt_sc, seq)

    slot = cnt_sc[...] - 1.0
    s_hi = jnp.floor(slot * (1.0 / SLOT_LO))
    s_lo = slot - SLOT_LO * s_hi
    tok = lax.broadcasted_iota(jnp.int32, (1, seq), 1).astype(F32)
    t_hi = jnp.floor(tok * (1.0 / 64.0))
    t_lo = tok - 64.0 * t_hi
    ph = lax.broadcasted_iota(jnp.int32, (SLOT_HI, seq), 0).astype(F32)
    pl_ = lax.broadcasted_iota(jnp.int32, (SLOT_LO, seq), 0).astype(F32)
    for e in range(N_EXPERTS):
        row = slice(e, e + 1)
        onehot_hi = jnp.where((s_hi[row] == ph) & sel[row], 1.0, 0.0)
        g = a[row]
        g1 = g.astype(BF16).astype(F32)
        g2 = (g - g1).astype(BF16).astype(F32)
        g3 = g - g1 - g2
        for part, val in enumerate((t_hi, t_lo, g1, g2, g3)):
            r0 = (part * N_EXPERTS + e) * SLOT_HI
            lhs_sc[r0:r0 + SLOT_HI, :] = (onehot_hi * val).astype(BF16)
        rhs_sc[e * SLOT_LO:(e + 1) * SLOT_LO, :] = jnp.where(s_lo[row] == pl_, 1.0, 0.0).astype(BF16)
    res = _dot_nt(lhs_sc[...], rhs_sc[...])
    n = N_EXPERTS * SLOT_HI
    r = lax.broadcasted_iota(jnp.int32, (n, N_EXPERTS * SLOT_LO), 0) // SLOT_HI
    c = lax.broadcasted_iota(jnp.int32, (n, N_EXPERTS * SLOT_LO), 1) // SLOT_LO
    parts = []
    for part in range(SEL_PARTS):
        blk = jnp.where(r == c, res[part * n:(part + 1) * n], 0.0)
        parts.append(jnp.sum(blk.reshape(N_EXPERTS, SLOT_HI, N_EXPERTS * SLOT_LO), axis=0))
    idx_ref[0] = (64.0 * parts[0] + parts[1]).astype(jnp.int32)
    gate_ref[0] = (parts[2] + parts[3]) + parts[4]


def _ec_select(aff, batch, seq, cap):
    assert cap == SLOT_HI * SLOT_LO
    out = pl.BlockSpec((1, SLOT_HI, N_EXPERTS * SLOT_LO), lambda b: (b, 0, 0))
    idx_t, gate_t = pl.pallas_call(
        functools.partial(_select_kernel, seq=seq, cap=cap),
        grid=(batch,),
        in_specs=[pl.BlockSpec((1, N_EXPERTS, seq), lambda b: (b, 0, 0))],
        out_specs=[out, out],
        out_shape=[jax.ShapeDtypeStruct((batch, SLOT_HI, N_EXPERTS * SLOT_LO), jnp.int32),
                   jax.ShapeDtypeStruct((batch, SLOT_HI, N_EXPERTS * SLOT_LO), F32)],
        scratch_shapes=[pltpu.VMEM((N_EXPERTS, seq), F32),
                        pltpu.VMEM((SEL_PARTS * N_EXPERTS * SLOT_HI, seq), BF16),
                        pltpu.VMEM((N_EXPERTS * SLOT_LO, seq), BF16)],
        compiler_params=_cparams(("parallel",)),
        name="ec_select",
    )(aff)
    order = lambda v: v.reshape(batch, SLOT_HI, N_EXPERTS, SLOT_LO).transpose(0, 2, 1, 3).reshape(batch, N_EXPERTS, cap)
    return order(idx_t), order(gate_t)


FFN_PHASES = 4


def _gather_rows(idx_ref, xp_ref, dst, lo, hi):
    for p in range(lo, hi):
        dst[pl.ds(p, 1), :] = xp_ref[0, pl.ds(idx_ref[0, 0, p], 1), :]


def _scatter_rows(idx_ref, gate_ref, src, acc_sc, lo, hi, live):
    for p0 in range(lo, hi, 8):
        ps = range(p0, p0 + 8)
        ts = [idx_ref[0, 0, p] for p in ps]
        rows = [acc_sc[pl.ds(t, 1), :] + src[pl.ds(p, 1), :] * (gate_ref[0, 0, p] * live) for p, t in zip(ps, ts)]
        for t, row in zip(ts, rows):
            acc_sc[pl.ds(t, 1), :] = row


def _ffn_kernel(idx_ref, idx_next_ref, idx_prev_ref, gate_ref, gate_prev_ref, xp_ref, wgu_ref, wd_ref, out_hbm,
                u_sc, ye_sc, acc_sc, sem, *, cap, d_ff):
    b = pl.program_id(0)
    e = pl.program_id(1)
    cur = e % 2
    nxt = 1 - cur

    @pl.when(e == 0)
    def _():
        acc_sc[...] = jnp.zeros(acc_sc.shape, F32)
        _gather_rows(idx_ref, xp_ref, u_sc.at[0], 0, cap)

    @pl.when((e == 0) & (b == 0))
    def _():
        ye_sc[1] = jnp.zeros(ye_sc.shape[1:], F32)

    u = u_sc[cur]
    lo = pltpu.bitcast(u << 16, F32).astype(BF16)
    hi = pltpu.bitcast(u & jnp.uint32(0xFFFF0000), F32).astype(BF16)
    xs = jnp.concatenate([lo, hi], axis=-1)
    live = jnp.where(e > 0, 1.0, 0.0)
    share = cap // FFN_PHASES
    width = d_ff // FFN_PHASES
    acts = []
    for ph in range(FFN_PHASES):
        hg = _dot(xs, wgu_ref[0, :, ph * width:(ph + 1) * width])
        hu = _dot(xs, wgu_ref[0, :, d_ff + ph * width:d_ff + (ph + 1) * width])
        acts.append((hg * jax.nn.sigmoid(hg) * hu).astype(BF16))
        _gather_rows(idx_next_ref, xp_ref, u_sc.at[nxt], ph * share, (ph + 1) * share)
        _scatter_rows(idx_prev_ref, gate_prev_ref, ye_sc.at[nxt], acc_sc, ph * share, (ph + 1) * share, live)
    ye_sc[cur] = _dot(jnp.concatenate(acts, axis=-1), wd_ref[0])

    @pl.when(e == N_EXPERTS - 1)
    def _():
        _scatter_rows(idx_ref, gate_ref, ye_sc.at[cur], acc_sc, 0, cap, 1.0)
        copy = pltpu.make_async_copy(acc_sc, out_hbm.at[b], sem)
        copy.start()
        copy.wait()


def _ec_ffn(idx, gate, x1p, wgu, wd, batch, seq, cap):
    d_ff = wd.shape[1]
    last = N_EXPERTS - 1
    slot = lambda step: pl.BlockSpec((1, 1, cap), lambda b, e: (b * N_EXPERTS + step(e), 0, 0),
                                     memory_space=pltpu.SMEM)
    this, following, previous = (slot(lambda e: e), slot(lambda e: jnp.minimum(e + 1, last)),
                                 slot(lambda e: jnp.maximum(e - 1, 0)))
    idx3 = idx.reshape(batch * N_EXPERTS, 1, cap)
    gate3 = gate.reshape(batch * N_EXPERTS, 1, cap)
    return pl.pallas_call(
        functools.partial(_ffn_kernel, cap=cap, d_ff=d_ff),
        grid=(batch, N_EXPERTS),
        in_specs=[this, following, previous, this, previous,
                  pl.BlockSpec((1, seq, D_MODEL // 2), lambda b, e: (b, 0, 0), pipeline_mode=pl.Buffered(1)),
                  pl.BlockSpec((1, D_MODEL, 2 * d_ff), lambda b, e: (e, 0, 0)),
                  pl.BlockSpec((1, d_ff, D_MODEL), lambda b, e: (e, 0, 0))],
        out_specs=pl.BlockSpec(memory_space=pl.ANY),
        out_shape=jax.ShapeDtypeStruct((batch, seq, D_MODEL), F32),
        scratch_shapes=[pltpu.VMEM((2, cap, D_MODEL // 2), jnp.uint32), pltpu.VMEM((2, cap, D_MODEL), F32),
                        pltpu.VMEM((seq, D_MODEL), F32), pltpu.SemaphoreType.DMA(())],
        compiler_params=_cparams(("arbitrary", "arbitrary")),
        name="ec_ffn",
    )(idx3, idx3, idx3, gate3, gate3, x1p, wgu, wd)


LN2_TM = 1024


def _ln2_kernel(x_ref, f_ref, g_ref, b_ref, o_ref, *, alpha):
    o_ref[...] = _layer_norm(alpha * x_ref[...] + f_ref[...], g_ref[...], b_ref[...])


def _residual_ln(x1, ffn, g2, b2, alpha):
    tokens = x1.shape[0]
    row = pl.BlockSpec((LN2_TM, D_MODEL), lambda i: (i, 0))
    const = pl.BlockSpec((1, D_MODEL), lambda i: (0, 0))
    return pl.pallas_call(
        functools.partial(_ln2_kernel, alpha=alpha),
        grid=(tokens // LN2_TM,),
        in_specs=[row, row, const, const],
        out_specs=row,
        out_shape=jax.ShapeDtypeStruct((tokens, D_MODEL), F32),
        compiler_params=_cparams(("parallel",)),
        name="residual_ln",
    )(x1, ffn, g2, b2)


def kernel(x, mem, w_in, b_gate, na_rpb, gqa_q_gain, gqa_k_gain, w_mem_kv, w_branch, w_out,
           ln1_g, ln1_b, w_router, w_gate_up, w_down, ln2_g, ln2_b):
    batch, seq, d = x.shape
    mem_tokens = mem.shape[1]
    depth = w_in.shape[0]
    alpha = (2 * depth) ** 0.25
    cap = EC_CAPACITY_FACTOR * seq // N_EXPERTS
    assert d == D_MODEL and seq == GRID_W * GRID_W and w_in.shape[2] == QKV_W + N_BRANCH * D_MODEL
    row = lambda v: v.reshape(1, -1)
    for l in range(depth):
        x2 = x.reshape(batch * seq, d)
        w_qkv = w_in[l, :, :QKV_W].astype(BF16)
        w_gate = w_in[l, :, QKV_W:].astype(BF16)
        qna, kna, vna, qg, kg, vg, qm = _in_proj(
            x2, w_qkv, row(jnp.tile(gqa_q_gain[l], GQA_HEADS)), row(jnp.tile(gqa_k_gain[l], GQA_KV_HEADS)),
            batch, seq)
        mk, mv = _mem_kv(mem.reshape(batch * mem_tokens, d), w_mem_kv[l].astype(BF16))
        yna = _na_attention(qna, kna, vna, _na_bias_tiles(na_rpb[l]), batch, seq)
        ygqa = _gqa_attention(qg, kg, vg, batch, seq)
        x1, x1p, aff = _merge(x2, yna, ygqa, qm, mk, mv, w_gate, row(b_gate[l]), w_branch[l].astype(BF16),
                              w_out[l].astype(BF16), row(ln1_g[l]), row(ln1_b[l]), w_router[l].T, seq, mem_tokens,
                              alpha)
        idx, gate = _ec_select(aff, batch, seq, cap)
        ffn = _ec_ffn(idx, gate, x1p.reshape(batch, seq, d // 2), w_gate_up[l].astype(BF16),
                      w_down[l].astype(BF16), batch, seq, cap)
        x = _residual_ln(x1, ffn.reshape(batch * seq, d), row(ln2_g[l]), row(ln2_b[l]), alpha).reshape(batch, seq, d)
    return x
```

```python
import functools
import math

import jax
import jax.numpy as jnp
import numpy as np
from jax import lax
from jax.experimental import pallas as pl
from jax.experimental.pallas import tpu as pltpu

F32 = jnp.float32
BF16 = jnp.bfloat16

D_MODEL = 1024
GRID_W = 64
HEAD_DIM = 64
LANES = 128
NA_HEADS = 8
NA_KH = 8
NA_KW = 16
NA_W = NA_HEADS * HEAD_DIM
NA_QROWS = 4
NA_KROWS = 12
GQA_HEADS = 8
GQA_KV_HEADS = 2
GQA_GROUP = GQA_HEADS // GQA_KV_HEADS
GQA_Q_W = GQA_HEADS * HEAD_DIM
GQA_KV_W = GQA_KV_HEADS * HEAD_DIM
ROPE_THETA = 10000.0
MEM_HEADS = 4
MEM_HEAD_DIM = 128
MEM_W = MEM_HEADS * MEM_HEAD_DIM
N_BRANCH = 3
N_EXPERTS = 16
EC_CAPACITY_FACTOR = 2
LN_EPS = 1e-5
RMS_EPS = 1e-6
NEG_INF = -1e30
QKV_W = 3 * NA_W + GQA_Q_W + 2 * GQA_KV_W + MEM_W

VMEM_LIMIT = 56 * 1024 * 1024


def _cparams(sem, vmem=VMEM_LIMIT):
    return pltpu.CompilerParams(dimension_semantics=sem, vmem_limit_bytes=vmem)


def _dot(a, b):
    return jnp.dot(a, b, preferred_element_type=F32)


def _dot_nt(a, b):
    return lax.dot_general(a, b, (((1,), (1,)), ((), ())), preferred_element_type=F32)


def _layer_norm(v, g, b):
    mu = jnp.mean(v, axis=-1, keepdims=True)
    var = jnp.mean(jnp.square(v - mu), axis=-1, keepdims=True)
    return (v - mu) * lax.rsqrt(var + LN_EPS) * g + b


def _na_bias_kernel(rpb_ref, out_ref):
    n = GRID_W * GRID_W
    col = lax.broadcasted_iota(jnp.int32, (32, n), 1)
    d = lax.broadcasted_iota(jnp.int32, (32, n), 0)
    c, kc = col // GRID_W, col % GRID_W
    onehot = (kc - c + (NA_KW - 1) == d).astype(F32)
    t = jnp.dot(rpb_ref[...], onehot, precision=lax.Precision.HIGHEST, preferred_element_type=F32)
    col1 = lax.broadcasted_iota(jnp.int32, (1, n), 1)
    c1, kc1 = col1 // GRID_W, col1 % GRID_W
    start = jnp.clip(c1 - NA_KW // 2, 0, GRID_W - NA_KW)
    in_win = (kc1 >= start) & (kc1 < start + NA_KW)
    out_ref[...] = jnp.where(in_win, t * math.log2(math.e), NEG_INF)


def _na_bias_tiles(rpb):
    h = rpb.shape[0]
    nd = 2 * NA_KH - 1
    rpb2 = jnp.pad(rpb.reshape(h * nd, 2 * NA_KW - 1), ((0, 0), (0, 1)))
    t = pl.pallas_call(
        _na_bias_kernel,
        out_shape=jax.ShapeDtypeStruct((h * nd, GRID_W * GRID_W), F32),
        name="na_bias",
    )(rpb2).reshape(h, nd, GRID_W, GRID_W)
    t = jnp.concatenate([t, jnp.full((h, 1, GRID_W, GRID_W), NEG_INF, F32)], axis=1)
    which = np.full((3, NA_QROWS, NA_KROWS), nd, np.int32)
    for variant in range(3):
        for i in range(NA_QROWS):
            for j in range(NA_KROWS):
                if variant == 0:
                    dr, ok = j - i, j < NA_KH
                elif variant == 1:
                    dr = j - i - NA_KH // 2
                    ok = -(NA_KH // 2) <= dr < NA_KH // 2
                else:
                    dr, ok = j - i - NA_KH, j >= NA_KROWS - NA_KH
                if ok:
                    which[variant, i, j] = dr + NA_KH - 1
    tiles = jnp.take(t, jnp.asarray(which.reshape(-1)), axis=1)
    tiles = tiles.reshape(h, 3, NA_QROWS, NA_KROWS, GRID_W, GRID_W).transpose(1, 0, 2, 4, 3, 5)
    return tiles.reshape(3, h, NA_QROWS * GRID_W, NA_KROWS * GRID_W)


PROJ_TM = 1024


def _rope_tables(seq_len):
    t = jnp.arange(seq_len)
    row = (t // GRID_W).astype(F32)
    col = (t % GRID_W).astype(F32)
    half = HEAD_DIM // 2
    inv = ROPE_THETA ** (-jnp.arange(0, half, 2, dtype=F32) / half)
    ang = jnp.concatenate([row[:, None] * inv, col[:, None] * inv], axis=-1)
    cos, sin = jnp.cos(ang), jnp.sin(ang)
    cos_l = jnp.repeat(cos, 2, axis=-1)
    sin_l = jnp.stack([-sin, sin], axis=-1).reshape(seq_len, HEAD_DIM)
    return jnp.tile(cos_l, (1, 2)), jnp.tile(sin_l, (1, 2))


def _head_sums(sq, width):
    r = lax.broadcasted_iota(jnp.int32, (width, width), 0) // HEAD_DIM
    c = lax.broadcasted_iota(jnp.int32, (width, width), 1) // HEAD_DIM
    ones = (r == c).astype(BF16)
    hi = sq.astype(BF16)
    lo = (sq - hi.astype(F32)).astype(BF16)
    return _dot(hi, ones) + _dot(lo, ones)


def _rope(v, cos, sin_signed):
    lane = lax.broadcasted_iota(jnp.int32, v.shape, 1)
    swapped = jnp.where(lane % 2 == 0, pltpu.roll(v, LANES - 1, 1), pltpu.roll(v, 1, 1))
    return v * cos + swapped * sin_signed


def _proj_kernel(x_ref, w_ref, qgain_ref, kgain_ref, cos_ref, sin_ref,
                 qna_ref, kna_ref, vna_ref, qg_ref, kg_ref, vg_ref, qm_ref):
    xb = x_ref[...].astype(BF16)
    qscale = math.log2(math.e) / math.sqrt(HEAD_DIM)
    o = 0
    qna_ref[...] = (_dot(xb, w_ref[:, o:o + NA_W]) * qscale).astype(BF16)
    o += NA_W
    kna_ref[...] = _dot(xb, w_ref[:, o:o + NA_W]).astype(BF16)
    o += NA_W
    vna_ref[...] = _dot(xb, w_ref[:, o:o + NA_W]).astype(BF16)
    o += NA_W
    cos, sin = cos_ref[...], sin_ref[...]
    lane = lax.broadcasted_iota(jnp.int32, (PROJ_TM, LANES), 1)
    low = lane < HEAD_DIM

    zq = _dot(xb, w_ref[:, o:o + GQA_Q_W])
    o += GQA_Q_W
    ms = _head_sums(zq * zq, GQA_Q_W) * (1.0 / HEAD_DIM)
    zq = zq * lax.rsqrt(ms + RMS_EPS) * qgain_ref[...]
    for p in range(GQA_HEADS // 2):
        r = _rope(zq[:, p * LANES:(p + 1) * LANES], cos, sin) * qscale
        rolled = pltpu.roll(r, HEAD_DIM, 1)
        if (2 * p) // GQA_GROUP == 0:
            even, odd = jnp.where(low, r, 0.0), jnp.where(low, rolled, 0.0)
        else:
            even, odd = jnp.where(low, 0.0, rolled), jnp.where(low, 0.0, r)
        qg_ref[0, 2 * p] = even.astype(BF16)
        qg_ref[0, 2 * p + 1] = odd.astype(BF16)

    zk = _dot(xb, w_ref[:, o:o + GQA_KV_W])
    o += GQA_KV_W
    ms = _head_sums(zk * zk, GQA_KV_W) * (1.0 / HEAD_DIM)
    zk = zk * lax.rsqrt(ms + RMS_EPS) * kgain_ref[...]
    kg_ref[...] = _rope(zk, cos, sin).astype(BF16)
    zv = _dot(xb, w_ref[:, o:o + GQA_KV_W])
    vg_ref[0] = jnp.where(low, zv, 1.0).astype(BF16)
    vg_ref[1] = jnp.where(low, 1.0, zv).astype(BF16)
    o += GQA_KV_W
    qm_ref[...] = _dot(xb, w_ref[:, o:o + MEM_W]).astype(BF16)


def _in_proj(x2, w_qkv, qgain, kgain, batch, seq):
    tokens = x2.shape[0]
    tm = PROJ_TM
    spb = seq // tm
    cos, sin = _rope_tables(seq)
    row = lambda w: pl.BlockSpec((tm, w), lambda i: (i, 0))
    const = lambda shape: pl.BlockSpec(shape, lambda i: (0,) * len(shape), pipeline_mode=pl.Buffered(1))
    tbl = pl.BlockSpec((tm, LANES), lambda i: (i % spb, 0))
    sds = lambda w: jax.ShapeDtypeStruct((tokens, w), BF16)
    return pl.pallas_call(
        _proj_kernel,
        grid=(tokens // tm,),
        in_specs=[row(D_MODEL), const((D_MODEL, QKV_W)), const((1, GQA_Q_W)), const((1, GQA_KV_W)), tbl, tbl],
        out_specs=[row(NA_W), row(NA_W), row(NA_W),
                   pl.BlockSpec((1, GQA_HEADS, tm, LANES), lambda i: (i // spb, 0, i % spb, 0)),
                   row(GQA_KV_W), pl.BlockSpec((GQA_KV_HEADS, tm, LANES), lambda i: (0, i, 0)), row(MEM_W)],
        out_shape=[sds(NA_W), sds(NA_W), sds(NA_W),
                   jax.ShapeDtypeStruct((batch, GQA_HEADS, seq, LANES), BF16),
                   sds(GQA_KV_W), jax.ShapeDtypeStruct((GQA_KV_HEADS, tokens, LANES), BF16), sds(MEM_W)],
        compiler_params=_cparams(("parallel",)),
        name="in_proj",
    )(x2, w_qkv, qgain, kgain, cos, sin)


def _memkv_kernel(m_ref, w_ref, k_ref, v_ref):
    z = _dot(m_ref[...].astype(BF16), w_ref[...])
    k_ref[...] = z[:, :MEM_W].astype(BF16)
    v_ref[...] = z[:, MEM_W:].astype(BF16)


def _mem_kv(mem2, w_kv):
    rows = mem2.shape[0]
    tm = 256
    return pl.pallas_call(
        _memkv_kernel,
        grid=(rows // tm,),
        in_specs=[pl.BlockSpec((tm, D_MODEL), lambda i: (i, 0)),
                  pl.BlockSpec((D_MODEL, 2 * MEM_W), lambda i: (0, 0))],
        out_specs=[pl.BlockSpec((tm, MEM_W), lambda i: (i, 0))] * 2,
        out_shape=[jax.ShapeDtypeStruct((rows, MEM_W), BF16)] * 2,
        compiler_params=_cparams(("parallel",)),
        name="mem_kv",
    )(mem2, w_kv)


NA_TQ = NA_QROWS * GRID_W
NA_TK = NA_KROWS * GRID_W


NA_GROUPS_PER_STEP = 4


def _na_kernel(q_ref, k_ref, v_ref, bias_ref, o_ref, *, n_groups):
    low = lax.broadcasted_iota(jnp.int32, (NA_TQ, LANES), 1) < HEAD_DIM
    low_k = lax.broadcasted_iota(jnp.int32, (NA_TK, LANES), 1) < HEAD_DIM
    tiles = NA_TK // LANES
    for sub in range(NA_GROUPS_PER_STEP):
        rg = pl.program_id(2) * NA_GROUPS_PER_STEP + sub
        key_row0 = jnp.clip(NA_QROWS * rg - NA_KH // 2, 0, GRID_W - NA_KROWS)
        start = pl.multiple_of(key_row0 * GRID_W, GRID_W)
        variant = jnp.where(rg == 0, 0, jnp.where(rg == n_groups - 1, 2, 1))
        k = k_ref[pl.ds(start, NA_TK), :]
        v = v_ref[pl.ds(start, NA_TK), :]
        q = q_ref[sub * NA_TQ:(sub + 1) * NA_TQ, :]
        outs = []
        for hh in range(2):
            mine, mine_k = (low, low_k) if hh == 0 else (jnp.logical_not(low), jnp.logical_not(low_k))
            qh = jnp.where(mine, q, jnp.zeros_like(q))
            vh = jnp.where(mine_k, v, jnp.ones_like(v))
            s = _dot_nt(qh, k) + bias_ref[variant, hh]
            st = [s[:, j * LANES:(j + 1) * LANES] for j in range(tiles)]
            part = st[0]
            for t in st[1:]:
                part = jnp.maximum(part, t)
            m = jnp.broadcast_to(jnp.max(part, axis=-1, keepdims=True), part.shape)
            p = jnp.concatenate([jnp.exp2(t - m).astype(BF16) for t in st], axis=-1)
            acc = _dot(p, vh)
            outs.append(acc / pltpu.roll(acc, HEAD_DIM, 1))
        o_ref[sub * NA_TQ:(sub + 1) * NA_TQ, :] = jnp.where(low, outs[0], outs[1]).astype(BF16)


def _na_attention(qna, kna, vna, bias, batch, seq):
    n_groups = seq // NA_TQ
    steps = n_groups // NA_GROUPS_PER_STEP
    pairs = NA_HEADS // 2
    kv = pl.BlockSpec((seq, LANES), lambda p, b, g: (b, p))
    qo = pl.BlockSpec((NA_GROUPS_PER_STEP * NA_TQ, LANES), lambda p, b, g: (b * steps + g, p))
    return pl.pallas_call(
        functools.partial(_na_kernel, n_groups=n_groups),
        grid=(pairs, batch, steps),
        in_specs=[qo, kv, kv, pl.BlockSpec((3, 2, NA_TQ, NA_TK), lambda p, b, g: (0, p, 0, 0))],
        out_specs=qo,
        out_shape=jax.ShapeDtypeStruct(qna.shape, BF16),
        compiler_params=_cparams(("parallel", "parallel", "parallel")),
        name="na_attn",
    )(qna, kna, vna, bias)


GQA_TQ = 128
GQA_TK = 1024


def _gqa_kernel(q_ref, k_ref, v_ref, o_ref, s_sc, mx_sc, acc_sc, *, seq):
    rows = GQA_GROUP * GQA_TQ
    n_chunks = seq // GQA_TK
    tiles = GQA_TK // LANES
    groups = range(GQA_KV_HEADS)
    lane = lax.broadcasted_iota(jnp.int32, (GQA_TQ, LANES), 1)
    low = lane < HEAD_DIM
    qs = [q_ref[0, g * GQA_GROUP:(g + 1) * GQA_GROUP].reshape(rows, LANES) for g in groups]
    mx_sc[...] = jnp.full(mx_sc.shape, -jnp.inf, F32)

    for c in range(n_chunks):
        k = k_ref[c * GQA_TK:(c + 1) * GQA_TK, :]
        for g in groups:
            s = _dot_nt(qs[g], k)
            s_sc[g, c] = s
            part = s[:, :LANES]
            for j in range(1, tiles):
                part = jnp.maximum(part, s[:, j * LANES:(j + 1) * LANES])
            mx_sc[g] = jnp.maximum(mx_sc[g], part)
    ms = [jnp.broadcast_to(jnp.max(mx_sc[g], axis=-1, keepdims=True), (rows, LANES)) for g in groups]
    acc_sc[...] = jnp.zeros(acc_sc.shape, F32)

    for c in range(n_chunks):
        for g in groups:
            p = [jnp.exp2(s_sc[g, c, :, j * LANES:(j + 1) * LANES] - ms[g]).astype(BF16) for j in range(tiles)]
            acc_sc[g] += _dot(jnp.concatenate(p, axis=-1), v_ref[g, c * GQA_TK:(c + 1) * GQA_TK, :])
    for g in groups:
        acc = acc_sc[g]
        out = acc / pltpu.roll(acc, HEAD_DIM, 1)
        for pp in range(GQA_GROUP // 2):
            a = out[(2 * pp) * GQA_TQ:(2 * pp + 1) * GQA_TQ]
            b = out[(2 * pp + 1) * GQA_TQ:(2 * pp + 2) * GQA_TQ]
            if g == 0:
                pair = jnp.where(low, a, pltpu.roll(b, HEAD_DIM, 1))
            else:
                pair = jnp.where(low, pltpu.roll(a, HEAD_DIM, 1), b)
            col = (g * (GQA_GROUP // 2) + pp) * LANES
            o_ref[:, col:col + LANES] = pair.astype(BF16)


def _gqa_attention(qg, kg, vg, batch, seq):
    nq = seq // GQA_TQ
    rows = GQA_GROUP * GQA_TQ
    stat = pltpu.VMEM((GQA_KV_HEADS, rows, LANES), F32)
    return pl.pallas_call(
        functools.partial(_gqa_kernel, seq=seq),
        grid=(batch, nq),
        in_specs=[pl.BlockSpec((1, GQA_HEADS, GQA_TQ, LANES), lambda b, i: (b, 0, i, 0)),
                  pl.BlockSpec((seq, LANES), lambda b, i: (b, 0)),
                  pl.BlockSpec((GQA_KV_HEADS, seq, LANES), lambda b, i: (0, b, 0))],
        out_specs=pl.BlockSpec((GQA_TQ, GQA_Q_W), lambda b, i: (b * nq + i, 0)),
        out_shape=jax.ShapeDtypeStruct((batch * seq, GQA_Q_W), BF16),
        scratch_shapes=[pltpu.VMEM((GQA_KV_HEADS, seq // GQA_TK, rows, GQA_TK), F32), stat, stat],
        compiler_params=_cparams(("parallel", "parallel")),
        name="gqa_attn",
    )(qg, kg, vg)


MERGE_TM = 512


def _merge_kernel(x_ref, yna_ref, ygqa_ref, qm_ref, mk_ref, mv_ref, wg_ref, bg_ref, wb_ref, wo_ref,
                  g1_ref, b1_ref, wr_ref, x1_ref, x1p_ref, aff_ref, *, alpha):
    x = x_ref[...]
    xb = x.astype(BF16)
    mem_scale = 1.0 / math.sqrt(MEM_HEAD_DIM)
    ymem = []
    for h in range(MEM_HEADS):
        sl = slice(h * MEM_HEAD_DIM, (h + 1) * MEM_HEAD_DIM)
        s = _dot_nt(qm_ref[:, sl], mk_ref[:, sl]) * mem_scale
        m = jnp.max(s, axis=-1, keepdims=True)
        p = jnp.exp(s - m)
        l = jnp.sum(p, axis=-1, keepdims=True)
        ymem.append((_dot(p.astype(BF16), mv_ref[:, sl]) / l).astype(BF16))
    branches = (yna_ref[...], ygqa_ref[...], jnp.concatenate(ymem, axis=-1))
    merged = jnp.zeros((MERGE_TM, D_MODEL), F32)
    for g in range(N_BRANCH):
        cols = slice(g * D_MODEL, (g + 1) * D_MODEL)
        gate = jax.nn.sigmoid(_dot(xb, wg_ref[:, cols]) + bg_ref[:, cols])
        merged = merged + gate * _dot(branches[g], wb_ref[g])
    mix = _dot(merged.astype(BF16), wo_ref[...])
    x1 = _layer_norm(alpha * x + mix, g1_ref[...], b1_ref[...])
    x1_ref[...] = x1
    half = D_MODEL // 2
    lo = pltpu.bitcast(x1[:, :half].astype(BF16).astype(F32), jnp.uint32)
    hi = pltpu.bitcast(x1[:, half:].astype(BF16).astype(F32), jnp.uint32)
    x1p_ref[...] = (hi & jnp.uint32(0xFFFF0000)) | (lo >> 16)
    logits = lax.dot_general(wr_ref[...], x1, (((1,), (1,)), ((), ())),
                             precision=lax.Precision.HIGHEST, preferred_element_type=F32)
    m = jnp.max(logits, axis=0, keepdims=True)
    e = jnp.exp(logits - m)
    aff_ref[0] = e / jnp.sum(e, axis=0, keepdims=True)


def _merge(x2, yna, ygqa, qm, mk, mv, w_gate, b_gate, w_branch, w_out, g1, b1, w_router, seq, mem_tokens, alpha):
    tokens = x2.shape[0]
    tm = MERGE_TM
    tpb = seq // tm
    row = lambda w: pl.BlockSpec((tm, w), lambda i: (i, 0))
    const = lambda shape: pl.BlockSpec(shape, lambda i: (0,) * len(shape), pipeline_mode=pl.Buffered(1))
    memb = pl.BlockSpec((mem_tokens, MEM_W), lambda i: (i // tpb, 0))
    return pl.pallas_call(
        functools.partial(_merge_kernel, alpha=alpha),
        grid=(tokens // tm,),
        in_specs=[row(D_MODEL), row(NA_W), row(GQA_Q_W), row(MEM_W), memb, memb,
                  const(w_gate.shape), const(b_gate.shape), const(w_branch.shape), const(w_out.shape),
                  const(g1.shape), const(b1.shape), const(w_router.shape)],
        out_specs=[row(D_MODEL), row(D_MODEL // 2),
                   pl.BlockSpec((1, N_EXPERTS, tm), lambda i: (i // tpb, 0, i % tpb))],
        out_shape=[jax.ShapeDtypeStruct((tokens, D_MODEL), F32),
                   jax.ShapeDtypeStruct((tokens, D_MODEL // 2), jnp.uint32),
                   jax.ShapeDtypeStruct((tokens // seq, N_EXPERTS, seq), F32)],
        compiler_params=_cparams(("parallel",)),
        name="merge",
    )(x2, yna, ygqa, qm, mk, mv, w_gate, b_gate, w_branch, w_out, g1, b1, w_router)


SEL_BLK = 256
SLOT_HI = 16
SLOT_LO = 32
SEL_PARTS = 5


def _prefix_counts(mask, out_ref, seq):
    r = lax.broadcasted_iota(jnp.int32, (SEL_BLK, SEL_BLK), 0)
    c = lax.broadcasted_iota(jnp.int32, (SEL_BLK, SEL_BLK), 1)
    tri = (r <= c).astype(BF16)
    run = jnp.zeros((mask.shape[0], 1), F32)
    for j in range(seq // SEL_BLK):
        cs = _dot(mask[:, j * SEL_BLK:(j + 1) * SEL_BLK].astype(BF16), tri) + run
        out_ref[:, j * SEL_BLK:(j + 1) * SEL_BLK] = cs
        run = cs[:, SEL_BLK - 1:SEL_BLK]


def _select_kernel(aff_ref, idx_ref, gate_ref, cnt_sc, lhs_sc, rhs_sc, *, seq, cap):
    a = aff_ref[0]
    bits = pltpu.bitcast(a, jnp.int32)
    thr = jnp.zeros((N_EXPERTS, 1), jnp.int32)
    for bit in range(30, -1, -1):
        cand = thr | (1 << bit)
        cnt = jnp.sum((bits >= cand).astype(F32), axis=-1, keepdims=True)
        thr = jnp.where(cnt >= cap, cand, thr)
    gt = bits > thr
    eq = bits == thr
    need = cap - jnp.sum(gt.astype(F32), axis=-1, keepdims=True)
    _prefix_counts(eq.astype(F32), cnt_sc, seq)
    sel = gt | (eq & (cnt_sc[...] <= need))
    _prefix_counts(sel.astype(F32), cnt_sc, seq)

    slot = cnt_sc[...] - 1.0
    s_hi = jnp.floor(slot * (1.0 / SLOT_LO))
    s_lo = slot - SLOT_LO * s_hi
    tok = lax.broadcasted_iota(jnp.int32, (1, seq), 1).astype(F32)
    t_hi = jnp.floor(tok * (1.0 / 64.0))
    t_lo = tok - 64.0 * t_hi
    ph = lax.broadcasted_iota(jnp.int32, (SLOT_HI, seq), 0).astype(F32)
    pl_ = lax.broadcasted_iota(jnp.int32, (SLOT_LO, seq), 0).astype(F32)
    for e in range(N_EXPERTS):
        row = slice(e, e + 1)
        onehot_hi = jnp.where((s_hi[row] == ph) & sel[row], 1.0, 0.0)
        g = a[row]
        g1 = g.astype(BF16).astype(F32)
        g2 = (g - g1).astype(BF16).astype(F32)
        g3 = g - g1 - g2
        for part, val in enumerate((t_hi, t_lo, g1, g2, g3)):
            r0 = (part * N_EXPERTS + e) * SLOT_HI
            lhs_sc[r0:r0 + SLOT_HI, :] = (onehot_hi * val).astype(BF16)
        rhs_sc[e * SLOT_LO:(e + 1) * SLOT_LO, :] = jnp.where(s_lo[row] == pl_, 1.0, 0.0).astype(BF16)
    res = _dot_nt(lhs_sc[...], rhs_sc[...])
    n = N_EXPERTS * SLOT_HI
    r = lax.broadcasted_iota(jnp.int32, (n, N_EXPERTS * SLOT_LO), 0) // SLOT_HI
    c = lax.broadcasted_iota(jnp.int32, (n, N_EXPERTS * SLOT_LO), 1) // SLOT_LO
    parts = []
    for part in range(SEL_PARTS):
        blk = jnp.where(r == c, res[part * n:(part + 1) * n], 0.0)
        parts.append(jnp.sum(blk.reshape(N_EXPERTS, SLOT_HI, N_EXPERTS * SLOT_LO), axis=0))
    idx_ref[0] = (64.0 * parts[0] + parts[1]).astype(jnp.int32)
    gate_ref[0] = (parts[2] + parts[3]) + parts[4]


def _ec_select(aff, batch, seq, cap):
    assert cap == SLOT_HI * SLOT_LO
    out = pl.BlockSpec((1, SLOT_HI, N_EXPERTS * SLOT_LO), lambda b: (b, 0, 0))
    idx_t, gate_t = pl.pallas_call(
        functools.partial(_select_kernel, seq=seq, cap=cap),
        grid=(batch,),
        in_specs=[pl.BlockSpec((1, N_EXPERTS, seq), lambda b: (b, 0, 0))],
        out_specs=[out, out],
        out_shape=[jax.ShapeDtypeStruct((batch, SLOT_HI, N_EXPERTS * SLOT_LO), jnp.int32),
                   jax.ShapeDtypeStruct((batch, SLOT_HI, N_EXPERTS * SLOT_LO), F32)],
        scratch_shapes=[pltpu.VMEM((N_EXPERTS, seq), F32),
                        pltpu.VMEM((SEL_PARTS * N_EXPERTS * SLOT_HI, seq), BF16),
                        pltpu.VMEM((N_EXPERTS * SLOT_LO, seq), BF16)],
        compiler_params=_cparams(("parallel",)),
        name="ec_select",
    )(aff)
    order = lambda v: v.reshape(batch, SLOT_HI, N_EXPERTS, SLOT_LO).transpose(0, 2, 1, 3).reshape(batch, N_EXPERTS, cap)
    return order(idx_t), order(gate_t)


FFN_COL_PHASES = 2
FFN_UP_ROWS = 160
FFN_DOWN_ROWS = 96


def _gather_rows(idx_ref, xp_ref, dst, lo, hi):
    for p in range(lo, hi):
        dst[pl.ds(p, 1), :] = xp_ref[0, pl.ds(idx_ref[0, 0, p], 1), :]


def _scatter_rows(idx_ref, gate_ref, src, acc_sc, lo, hi):
    for p0 in range(lo, hi, 8):
        ps = range(p0, p0 + 8)
        ts = [idx_ref[0, 0, p] for p in ps]
        rows = [acc_sc[pl.ds(t, 1), :] + src[pl.ds(p, 1), :] * gate_ref[0, 0, p] for p, t in zip(ps, ts)]
        for t, row in zip(ts, rows):
            acc_sc[pl.ds(t, 1), :] = row


def _ffn_kernel(idx_ref, idx_next_ref, idx_prev_ref, gate_ref, gate_prev_ref, xp_ref, wgu_ref, wd_ref, out_hbm,
                u_sc, xs_sc, act_sc, ye_sc, acc_sc, sem, *, cap, d_ff):
    b = pl.program_id(0)
    e = pl.program_id(1)
    cur = e % 2
    nxt = 1 - cur

    @pl.when(e == 0)
    def _():
        acc_sc[...] = jnp.zeros(acc_sc.shape, F32)
        _gather_rows(idx_ref, xp_ref, u_sc.at[0], 0, cap)

    @pl.when((e == 0) & (b == 0))
    def _():
        ye_sc[1] = jnp.zeros(ye_sc.shape[1:], F32)

    u = u_sc[cur]
    lo = pltpu.bitcast(u << 16, F32).astype(BF16)
    hi = pltpu.bitcast(u & jnp.uint32(0xFFFF0000), F32).astype(BF16)
    xs_sc[...] = jnp.concatenate([lo, hi], axis=-1)

    width = d_ff // FFN_COL_PHASES
    out_w = D_MODEL // FFN_COL_PHASES
    row0 = 0
    for ph in range(2 * FFN_COL_PHASES):
        up = ph < FFN_COL_PHASES
        row1 = row0 + (FFN_UP_ROWS if up else FFN_DOWN_ROWS)

        @pl.when(e + ph >= 0)
        def _(ph=ph, up=up, row0=row0, row1=row1):
            _gather_rows(idx_next_ref, xp_ref, u_sc.at[nxt], row0, row1)
            _scatter_rows(idx_prev_ref, gate_prev_ref, ye_sc.at[nxt], acc_sc, row0, row1)
            if up:
                xs = xs_sc[...]
                hg = _dot(xs, wgu_ref[0, :, ph * width:(ph + 1) * width])
                hu = _dot(xs, wgu_ref[0, :, d_ff + ph * width:d_ff + (ph + 1) * width])
                act_sc[:, ph * width:(ph + 1) * width] = (hg * jax.nn.sigmoid(hg) * hu).astype(BF16)
            else:
                c = ph - FFN_COL_PHASES
                ye_sc[cur, :, c * out_w:(c + 1) * out_w] = _dot(act_sc[...], wd_ref[0, :, c * out_w:(c + 1) * out_w])

        row0 = row1
    assert row0 == cap

    @pl.when(e == N_EXPERTS - 1)
    def _():
        _scatter_rows(idx_ref, gate_ref, ye_sc.at[cur], acc_sc, 0, cap)
        copy = pltpu.make_async_copy(acc_sc, out_hbm.at[b], sem)
        copy.start()
        copy.wait()


def _ec_ffn(idx, gate, x1p, wgu, wd, batch, seq, cap):
    d_ff = wd.shape[1]
    slot = pl.BlockSpec((1, 1, cap), lambda b, e: (b * N_EXPERTS + e, 0, 0), memory_space=pltpu.SMEM)
    flat = lambda v: v.reshape(batch * N_EXPERTS, 1, cap)
    idx_next = jnp.concatenate([idx[:, 1:], idx[:, -1:]], axis=1)
    idx_prev = jnp.concatenate([idx[:, :1], idx[:, :-1]], axis=1)
    gate_prev = jnp.concatenate([jnp.zeros_like(gate[:, :1]), gate[:, :-1]], axis=1)
    return pl.pallas_call(
        functools.partial(_ffn_kernel, cap=cap, d_ff=d_ff),
        grid=(batch, N_EXPERTS),
        in_specs=[slot, slot, slot, slot, slot,
                  pl.BlockSpec((1, seq, D_MODEL // 2), lambda b, e: (b, 0, 0), pipeline_mode=pl.Buffered(1)),
                  pl.BlockSpec((1, D_MODEL, 2 * d_ff), lambda b, e: (e, 0, 0)),
                  pl.BlockSpec((1, d_ff, D_MODEL), lambda b, e: (e, 0, 0))],
        out_specs=pl.BlockSpec(memory_space=pl.ANY),
        out_shape=jax.ShapeDtypeStruct((batch, seq, D_MODEL), F32),
        scratch_shapes=[pltpu.VMEM((2, cap, D_MODEL // 2), jnp.uint32), pltpu.VMEM((cap, D_MODEL), BF16),
                        pltpu.VMEM((cap, d_ff), BF16), pltpu.VMEM((2, cap, D_MODEL), F32),
                        pltpu.VMEM((seq, D_MODEL), F32), pltpu.SemaphoreType.DMA(())],
        compiler_params=_cparams(("arbitrary", "arbitrary")),
        name="ec_ffn",
    )(flat(idx), flat(idx_next), flat(idx_prev), flat(gate), flat(gate_prev), x1p, wgu, wd)


LN2_TM = 1024


def _ln2_kernel(x_ref, f_ref, g_ref, b_ref, o_ref, *, alpha):
    o_ref[...] = _layer_norm(alpha * x_ref[...] + f_ref[...], g_ref[...], b_ref[...])


def _residual_ln(x1, ffn, g2, b2, alpha):
    tokens = x1.shape[0]
    row = pl.BlockSpec((LN2_TM, D_MODEL), lambda i: (i, 0))
    const = pl.BlockSpec((1, D_MODEL), lambda i: (0, 0))
    return pl.pallas_call(
        functools.partial(_ln2_kernel, alpha=alpha),
        grid=(tokens // LN2_TM,),
        in_specs=[row, row, const, const],
        out_specs=row,
        out_shape=jax.ShapeDtypeStruct((tokens, D_MODEL), F32),
        compiler_params=_cparams(("parallel",)),
        name="residual_ln",
    )(x1, ffn, g2, b2)


def kernel(x, mem, w_in, b_gate, na_rpb, gqa_q_gain, gqa_k_gain, w_mem_kv, w_branch, w_out,
           ln1_g, ln1_b, w_router, w_gate_up, w_down, ln2_g, ln2_b):
    batch, seq, d = x.shape
    mem_tokens = mem.shape[1]
    depth = w_in.shape[0]
    alpha = (2 * depth) ** 0.25
    cap = EC_CAPACITY_FACTOR * seq // N_EXPERTS
    assert d == D_MODEL and seq == GRID_W * GRID_W and w_in.shape[2] == QKV_W + N_BRANCH * D_MODEL
    row = lambda v: v.reshape(1, -1)
    for l in range(depth):
        x2 = x.reshape(batch * seq, d)
        w_qkv = w_in[l, :, :QKV_W].astype(BF16)
        w_gate = w_in[l, :, QKV_W:].astype(BF16)
        qna, kna, vna, qg, kg, vg, qm = _in_proj(
            x2, w_qkv, row(jnp.tile(gqa_q_gain[l], GQA_HEADS)), row(jnp.tile(gqa_k_gain[l], GQA_KV_HEADS)),
            batch, seq)
        mk, mv = _mem_kv(mem.reshape(batch * mem_tokens, d), w_mem_kv[l].astype(BF16))
        yna = _na_attention(qna, kna, vna, _na_bias_tiles(na_rpb[l]), batch, seq)
        ygqa = _gqa_attention(qg, kg, vg, batch, seq)
        x1, x1p, aff = _merge(x2, yna, ygqa, qm, mk, mv, w_gate, row(b_gate[l]), w_branch[l].astype(BF16),
                              w_out[l].astype(BF16), row(ln1_g[l]), row(ln1_b[l]), w_router[l].T, seq, mem_tokens,
                              alpha)
        idx, gate = _ec_select(aff, batch, seq, cap)
        ffn = _ec_ffn(idx, gate, x1p.reshape(batch, seq, d // 2), w_gate_up[l].astype(BF16),
                      w_down[l].astype(BF16), batch, seq, cap)
        x = _residual_ln(x1, ffn.reshape(batch * seq, d), row(ln2_g[l]), row(ln2_b[l]), alpha).reshape(batch, seq, d)
    return x
```

```python
import functools
import math

import jax
import jax.numpy as jnp
import numpy as np
from jax import lax
from jax.experimental import pallas as pl
from jax.experimental.pallas import tpu as pltpu

F32 = jnp.float32
BF16 = jnp.bfloat16

D_MODEL = 1024
GRID_W = 64
HEAD_DIM = 64
LANES = 128
NA_HEADS = 8
NA_KH = 8
NA_KW = 16
NA_W = NA_HEADS * HEAD_DIM
NA_QROWS = 4
NA_KROWS = 12
GQA_HEADS = 8
GQA_KV_HEADS = 2
GQA_GROUP = GQA_HEADS // GQA_KV_HEADS
GQA_Q_W = GQA_HEADS * HEAD_DIM
GQA_KV_W = GQA_KV_HEADS * HEAD_DIM
ROPE_THETA = 10000.0
MEM_HEADS = 4
MEM_HEAD_DIM = 128
MEM_W = MEM_HEADS * MEM_HEAD_DIM
N_BRANCH = 3
N_EXPERTS = 16
EC_CAPACITY_FACTOR = 2
LN_EPS = 1e-5
RMS_EPS = 1e-6
NEG_INF = -1e30
QKV_W = 3 * NA_W + GQA_Q_W + 2 * GQA_KV_W + MEM_W

VMEM_LIMIT = 56 * 1024 * 1024
FFN_VMEM_LIMIT = 60 * 1024 * 1024


def _cparams(sem, vmem=VMEM_LIMIT):
    return pltpu.CompilerParams(dimension_semantics=sem, vmem_limit_bytes=vmem)


def _dot(a, b):
    return jnp.dot(a, b, preferred_element_type=F32)


def _dot_nt(a, b):
    return lax.dot_general(a, b, (((1,), (1,)), ((), ())), preferred_element_type=F32)


def _layer_norm(v, g, b):
    mu = jnp.mean(v, axis=-1, keepdims=True)
    var = jnp.mean(jnp.square(v - mu), axis=-1, keepdims=True)
    return (v - mu) * lax.rsqrt(var + LN_EPS) * g + b


def _na_bias_kernel(rpb_ref, out_ref):
    n = GRID_W * GRID_W
    col = lax.broadcasted_iota(jnp.int32, (32, n), 1)
    d = lax.broadcasted_iota(jnp.int32, (32, n), 0)
    c, kc = col // GRID_W, col % GRID_W
    onehot = (kc - c + (NA_KW - 1) == d).astype(F32)
    t = jnp.dot(rpb_ref[...], onehot, precision=lax.Precision.HIGHEST, preferred_element_type=F32)
    col1 = lax.broadcasted_iota(jnp.int32, (1, n), 1)
    c1, kc1 = col1 // GRID_W, col1 % GRID_W
    start = jnp.clip(c1 - NA_KW // 2, 0, GRID_W - NA_KW)
    in_win = (kc1 >= start) & (kc1 < start + NA_KW)
    out_ref[...] = jnp.where(in_win, t * math.log2(math.e), NEG_INF)


def _na_bias_tiles(rpb):
    h = rpb.shape[0]
    nd = 2 * NA_KH - 1
    rpb2 = jnp.pad(rpb.reshape(h * nd, 2 * NA_KW - 1), ((0, 0), (0, 1)))
    t = pl.pallas_call(
        _na_bias_kernel,
        out_shape=jax.ShapeDtypeStruct((h * nd, GRID_W * GRID_W), F32),
        name="na_bias",
    )(rpb2).reshape(h, nd, GRID_W, GRID_W)
    t = jnp.concatenate([t, jnp.full((h, 1, GRID_W, GRID_W), NEG_INF, F32)], axis=1)
    which = np.full((3, NA_QROWS, NA_KROWS), nd, np.int32)
    for variant in range(3):
        for i in range(NA_QROWS):
            for j in range(NA_KROWS):
                if variant == 0:
                    dr, ok = j - i, j < NA_KH
                elif variant == 1:
                    dr = j - i - NA_KH // 2
                    ok = -(NA_KH // 2) <= dr < NA_KH // 2
                else:
                    dr, ok = j - i - NA_KH, j >= NA_KROWS - NA_KH
                if ok:
                    which[variant, i, j] = dr + NA_KH - 1
    return pl.pallas_call(
        functools.partial(_na_tiles_kernel, which=which),
        grid=(3, h),
        in_specs=[pl.BlockSpec((1, nd + 1, GRID_W, GRID_W), lambda v, hh: (hh, 0, 0, 0))],
        out_specs=pl.BlockSpec((1, 1, NA_QROWS * GRID_W, NA_KROWS * GRID_W), lambda v, hh: (v, hh, 0, 0)),
        out_shape=jax.ShapeDtypeStruct((3, h, NA_QROWS * GRID_W, NA_KROWS * GRID_W), F32),
        compiler_params=_cparams(("parallel", "parallel")),
        name="na_tiles",
    )(t)


def _na_tiles_kernel(t_ref, o_ref, *, which):
    for variant in range(which.shape[0]):
        @pl.when(pl.program_id(0) == variant)
        def _(variant=variant):
            for i in range(NA_QROWS):
                for j in range(0, NA_KROWS, 2):
                    pair = jnp.concatenate([t_ref[0, int(which[variant, i, j])],
                                            t_ref[0, int(which[variant, i, j + 1])]], axis=-1)
                    o_ref[0, 0, i * GRID_W:(i + 1) * GRID_W, j * GRID_W:(j + 2) * GRID_W] = pair


PROJ_TM = 1024


def _rope_tables(seq_len):
    t = jnp.arange(seq_len)
    row = (t // GRID_W).astype(F32)
    col = (t % GRID_W).astype(F32)
    half = HEAD_DIM // 2
    inv = ROPE_THETA ** (-jnp.arange(0, half, 2, dtype=F32) / half)
    ang = jnp.concatenate([row[:, None] * inv, col[:, None] * inv], axis=-1)
    cos, sin = jnp.cos(ang), jnp.sin(ang)
    cos_l = jnp.repeat(cos, 2, axis=-1)
    sin_l = jnp.stack([-sin, sin], axis=-1).reshape(seq_len, HEAD_DIM)
    return jnp.tile(cos_l, (1, 2)), jnp.tile(sin_l, (1, 2))


def _head_sums(sq, width):
    r = lax.broadcasted_iota(jnp.int32, (width, width), 0) // HEAD_DIM
    c = lax.broadcasted_iota(jnp.int32, (width, width), 1) // HEAD_DIM
    ones = (r == c).astype(BF16)
    hi = sq.astype(BF16)
    lo = (sq - hi.astype(F32)).astype(BF16)
    return _dot(hi, ones) + _dot(lo, ones)


def _rope(v, cos, sin_signed):
    lane = lax.broadcasted_iota(jnp.int32, v.shape, 1)
    swapped = jnp.where(lane % 2 == 0, pltpu.roll(v, LANES - 1, 1), pltpu.roll(v, 1, 1))
    return v * cos + swapped * sin_signed


def _proj_kernel(x_ref, w_ref, qgain_ref, kgain_ref, cos_ref, sin_ref,
                 qna_ref, kna_ref, vna_ref, qg_ref, kg_ref, vg_ref, qm_ref):
    xb = x_ref[...].astype(BF16)
    qscale = math.log2(math.e) / math.sqrt(HEAD_DIM)
    o = 0
    qna_ref[...] = (_dot(xb, w_ref[:, o:o + NA_W]) * qscale).astype(BF16)
    o += NA_W
    kna_ref[...] = _dot(xb, w_ref[:, o:o + NA_W]).astype(BF16)
    o += NA_W
    vna_ref[...] = _dot(xb, w_ref[:, o:o + NA_W]).astype(BF16)
    o += NA_W
    cos, sin = cos_ref[...], sin_ref[...]
    lane = lax.broadcasted_iota(jnp.int32, (PROJ_TM, LANES), 1)
    low = lane < HEAD_DIM

    zq = _dot(xb, w_ref[:, o:o + GQA_Q_W])
    o += GQA_Q_W
    ms = _head_sums(zq * zq, GQA_Q_W) * (1.0 / HEAD_DIM)
    zq = zq * lax.rsqrt(ms + RMS_EPS) * qgain_ref[...]
    for p in range(GQA_HEADS // 2):
        r = _rope(zq[:, p * LANES:(p + 1) * LANES], cos, sin) * qscale
        rolled = pltpu.roll(r, HEAD_DIM, 1)
        if (2 * p) // GQA_GROUP == 0:
            even, odd = jnp.where(low, r, 0.0), jnp.where(low, rolled, 0.0)
        else:
            even, odd = jnp.where(low, 0.0, rolled), jnp.where(low, 0.0, r)
        qg_ref[0, 2 * p] = even.astype(BF16)
        qg_ref[0, 2 * p + 1] = odd.astype(BF16)

    zk = _dot(xb, w_ref[:, o:o + GQA_KV_W])
    o += GQA_KV_W
    ms = _head_sums(zk * zk, GQA_KV_W) * (1.0 / HEAD_DIM)
    zk = zk * lax.rsqrt(ms + RMS_EPS) * kgain_ref[...]
    kg_ref[...] = _rope(zk, cos, sin).astype(BF16)
    zv = _dot(xb, w_ref[:, o:o + GQA_KV_W])
    vg_ref[0] = jnp.where(low, zv, 1.0).astype(BF16)
    vg_ref[1] = jnp.where(low, 1.0, zv).astype(BF16)
    o += GQA_KV_W
    qm_ref[...] = _dot(xb, w_ref[:, o:o + MEM_W]).astype(BF16)


def _in_proj(x2, w_qkv, qgain, kgain, batch, seq):
    tokens = x2.shape[0]
    tm = PROJ_TM
    spb = seq // tm
    cos, sin = _rope_tables(seq)
    row = lambda w: pl.BlockSpec((tm, w), lambda i: (i, 0))
    const = lambda shape: pl.BlockSpec(shape, lambda i: (0,) * len(shape), pipeline_mode=pl.Buffered(1))
    tbl = pl.BlockSpec((tm, LANES), lambda i: (i % spb, 0))
    sds = lambda w: jax.ShapeDtypeStruct((tokens, w), BF16)
    return pl.pallas_call(
        _proj_kernel,
        grid=(tokens // tm,),
        in_specs=[row(D_MODEL), const((D_MODEL, QKV_W)), const((1, GQA_Q_W)), const((1, GQA_KV_W)), tbl, tbl],
        out_specs=[row(NA_W), row(NA_W), row(NA_W),
                   pl.BlockSpec((1, GQA_HEADS, tm, LANES), lambda i: (i // spb, 0, i % spb, 0)),
                   row(GQA_KV_W), pl.BlockSpec((GQA_KV_HEADS, tm, LANES), lambda i: (0, i, 0)), row(MEM_W)],
        out_shape=[sds(NA_W), sds(NA_W), sds(NA_W),
                   jax.ShapeDtypeStruct((batch, GQA_HEADS, seq, LANES), BF16),
                   sds(GQA_KV_W), jax.ShapeDtypeStruct((GQA_KV_HEADS, tokens, LANES), BF16), sds(MEM_W)],
        compiler_params=_cparams(("parallel",)),
        name="in_proj",
    )(x2, w_qkv, qgain, kgain, cos, sin)


def _memkv_kernel(m_ref, w_ref, k_ref, v_ref):
    z = _dot(m_ref[...].astype(BF16), w_ref[...])
    k_ref[...] = z[:, :MEM_W].astype(BF16)
    v_ref[...] = z[:, MEM_W:].astype(BF16)


def _mem_kv(mem2, w_kv):
    rows = mem2.shape[0]
    tm = 256
    return pl.pallas_call(
        _memkv_kernel,
        grid=(rows // tm,),
        in_specs=[pl.BlockSpec((tm, D_MODEL), lambda i: (i, 0)),
                  pl.BlockSpec((D_MODEL, 2 * MEM_W), lambda i: (0, 0))],
        out_specs=[pl.BlockSpec((tm, MEM_W), lambda i: (i, 0))] * 2,
        out_shape=[jax.ShapeDtypeStruct((rows, MEM_W), BF16)] * 2,
        compiler_params=_cparams(("parallel",)),
        name="mem_kv",
    )(mem2, w_kv)


NA_TQ = NA_QROWS * GRID_W
NA_TK = NA_KROWS * GRID_W


NA_GROUPS_PER_STEP = 8


def _na_kernel(q_ref, k_ref, v_ref, bias_ref, o_ref, *, n_groups):
    low = lax.broadcasted_iota(jnp.int32, (NA_TQ, LANES), 1) < HEAD_DIM
    low_k = lax.broadcasted_iota(jnp.int32, (NA_TK, LANES), 1) < HEAD_DIM
    tiles = NA_TK // LANES
    for sub in range(NA_GROUPS_PER_STEP):
        rg = pl.program_id(2) * NA_GROUPS_PER_STEP + sub
        key_row0 = jnp.clip(NA_QROWS * rg - NA_KH // 2, 0, GRID_W - NA_KROWS)
        start = pl.multiple_of(key_row0 * GRID_W, GRID_W)
        variant = jnp.where(rg == 0, 0, jnp.where(rg == n_groups - 1, 2, 1))
        k = k_ref[pl.ds(start, NA_TK), :]
        v = v_ref[pl.ds(start, NA_TK), :]
        q = q_ref[sub * NA_TQ:(sub + 1) * NA_TQ, :]
        outs = []
        for hh in range(2):
            mine, mine_k = (low, low_k) if hh == 0 else (jnp.logical_not(low), jnp.logical_not(low_k))
            qh = jnp.where(mine, q, jnp.zeros_like(q))
            vh = jnp.where(mine_k, v, jnp.ones_like(v))
            s = _dot_nt(qh, k) + bias_ref[variant, hh]
            st = [s[:, j * LANES:(j + 1) * LANES] for j in range(tiles)]
            part = st[0]
            for t in st[1:]:
                part = jnp.maximum(part, t)
            m = jnp.broadcast_to(jnp.max(part, axis=-1, keepdims=True), part.shape)
            p = jnp.concatenate([jnp.exp2(t - m).astype(BF16) for t in st], axis=-1)
            acc = _dot(p, vh)
            outs.append(acc / pltpu.roll(acc, HEAD_DIM, 1))
        o_ref[sub * NA_TQ:(sub + 1) * NA_TQ, :] = jnp.where(low, outs[0], outs[1]).astype(BF16)


def _na_attention(qna, kna, vna, bias, batch, seq):
    n_groups = seq // NA_TQ
    steps = n_groups // NA_GROUPS_PER_STEP
    pairs = NA_HEADS // 2
    kv = pl.BlockSpec((seq, LANES), lambda p, b, g: (b, p))
    qo = pl.BlockSpec((NA_GROUPS_PER_STEP * NA_TQ, LANES), lambda p, b, g: (b * steps + g, p))
    return pl.pallas_call(
        functools.partial(_na_kernel, n_groups=n_groups),
        grid=(pairs, batch, steps),
        in_specs=[qo, kv, kv, pl.BlockSpec((3, 2, NA_TQ, NA_TK), lambda p, b, g: (0, p, 0, 0))],
        out_specs=qo,
        out_shape=jax.ShapeDtypeStruct(qna.shape, BF16),
        compiler_params=_cparams(("parallel", "parallel", "parallel")),
        name="na_attn",
    )(qna, kna, vna, bias)


GQA_TQ = 128
GQA_TK = 1024


def _gqa_kernel(q_ref, k_ref, v_ref, o_ref, s_sc, mx_sc, acc_sc, *, seq):
    rows = GQA_GROUP * GQA_TQ
    n_chunks = seq // GQA_TK
    tiles = GQA_TK // LANES
    groups = range(GQA_KV_HEADS)
    lane = lax.broadcasted_iota(jnp.int32, (GQA_TQ, LANES), 1)
    low = lane < HEAD_DIM
    qs = [q_ref[0, g * GQA_GROUP:(g + 1) * GQA_GROUP].reshape(rows, LANES) for g in groups]
    mx_sc[...] = jnp.full(mx_sc.shape, -jnp.inf, F32)

    for c in range(n_chunks):
        k = k_ref[c * GQA_TK:(c + 1) * GQA_TK, :]
        for g in groups:
            s = _dot_nt(qs[g], k)
            s_sc[g, c] = s
            part = s[:, :LANES]
            for j in range(1, tiles):
                part = jnp.maximum(part, s[:, j * LANES:(j + 1) * LANES])
            mx_sc[g] = jnp.maximum(mx_sc[g], part)
    ms = [jnp.broadcast_to(jnp.max(mx_sc[g], axis=-1, keepdims=True), (rows, LANES)) for g in groups]
    acc_sc[...] = jnp.zeros(acc_sc.shape, F32)

    for c in range(n_chunks):
        for g in groups:
            p = [jnp.exp2(s_sc[g, c, :, j * LANES:(j + 1) * LANES] - ms[g]).astype(BF16) for j in range(tiles)]
            acc_sc[g] += _dot(jnp.concatenate(p, axis=-1), v_ref[g, c * GQA_TK:(c + 1) * GQA_TK, :])
    for g in groups:
        acc = acc_sc[g]
        out = acc / pltpu.roll(acc, HEAD_DIM, 1)
        for pp in range(GQA_GROUP // 2):
            a = out[(2 * pp) * GQA_TQ:(2 * pp + 1) * GQA_TQ]
            b = out[(2 * pp + 1) * GQA_TQ:(2 * pp + 2) * GQA_TQ]
            if g == 0:
                pair = jnp.where(low, a, pltpu.roll(b, HEAD_DIM, 1))
            else:
                pair = jnp.where(low, pltpu.roll(a, HEAD_DIM, 1), b)
            col = (g * (GQA_GROUP // 2) + pp) * LANES
            o_ref[:, col:col + LANES] = pair.astype(BF16)


def _gqa_attention(qg, kg, vg, batch, seq):
    nq = seq // GQA_TQ
    rows = GQA_GROUP * GQA_TQ
    stat = pltpu.VMEM((GQA_KV_HEADS, rows, LANES), F32)
    return pl.pallas_call(
        functools.partial(_gqa_kernel, seq=seq),
        grid=(batch, nq),
        in_specs=[pl.BlockSpec((1, GQA_HEADS, GQA_TQ, LANES), lambda b, i: (b, 0, i, 0)),
                  pl.BlockSpec((seq, LANES), lambda b, i: (b, 0)),
                  pl.BlockSpec((GQA_KV_HEADS, seq, LANES), lambda b, i: (0, b, 0))],
        out_specs=pl.BlockSpec((GQA_TQ, GQA_Q_W), lambda b, i: (b * nq + i, 0)),
        out_shape=jax.ShapeDtypeStruct((batch * seq, GQA_Q_W), BF16),
        scratch_shapes=[pltpu.VMEM((GQA_KV_HEADS, seq // GQA_TK, rows, GQA_TK), F32), stat, stat],
        compiler_params=_cparams(("parallel", "parallel")),
        name="gqa_attn",
    )(qg, kg, vg)


MERGE_TM = 512


def _merge_kernel(x_ref, yna_ref, ygqa_ref, qm_ref, mk_ref, mv_ref, wg_ref, bg_ref, wb_ref, wo_ref,
                  g1_ref, b1_ref, wr_ref, x1_ref, aff_ref, *, alpha):
    x = x_ref[...]
    xb = x.astype(BF16)
    mem_scale = 1.0 / math.sqrt(MEM_HEAD_DIM)
    ymem = []
    for h in range(MEM_HEADS):
        sl = slice(h * MEM_HEAD_DIM, (h + 1) * MEM_HEAD_DIM)
        s = _dot_nt(qm_ref[:, sl], mk_ref[:, sl]) * mem_scale
        m = jnp.max(s, axis=-1, keepdims=True)
        p = jnp.exp(s - m)
        l = jnp.sum(p, axis=-1, keepdims=True)
        ymem.append((_dot(p.astype(BF16), mv_ref[:, sl]) / l).astype(BF16))
    branches = (yna_ref[...], ygqa_ref[...], jnp.concatenate(ymem, axis=-1))
    merged = jnp.zeros((MERGE_TM, D_MODEL), F32)
    for g in range(N_BRANCH):
        cols = slice(g * D_MODEL, (g + 1) * D_MODEL)
        gate = jax.nn.sigmoid(_dot(xb, wg_ref[:, cols]) + bg_ref[:, cols])
        merged = merged + gate * _dot(branches[g], wb_ref[g])
    mix = _dot(merged.astype(BF16), wo_ref[...])
    x1 = _layer_norm(alpha * x + mix, g1_ref[...], b1_ref[...])
    x1_ref[...] = x1
    x1_hi = x1.astype(BF16)
    x1_lo = (x1 - x1_hi.astype(F32)).astype(BF16)
    w = wr_ref[...]
    w_hi = w.astype(BF16)
    w_lo = (w - w_hi.astype(F32)).astype(BF16)
    by_hi = _dot_nt(jnp.concatenate([w_hi, w_lo], axis=0), x1_hi)
    logits = by_hi[:N_EXPERTS] + by_hi[N_EXPERTS:] + _dot_nt(w_hi, x1_lo)
    m = jnp.max(logits, axis=0, keepdims=True)
    e = jnp.exp(logits - m)
    aff_ref[0] = e / jnp.sum(e, axis=0, keepdims=True)


def _merge(x2, yna, ygqa, qm, mk, mv, w_gate, b_gate, w_branch, w_out, g1, b1, w_router, seq, mem_tokens, alpha):
    tokens = x2.shape[0]
    tm = MERGE_TM
    tpb = seq // tm
    row = lambda w: pl.BlockSpec((tm, w), lambda i: (i, 0))
    const = lambda shape: pl.BlockSpec(shape, lambda i: (0,) * len(shape), pipeline_mode=pl.Buffered(1))
    memb = pl.BlockSpec((mem_tokens, MEM_W), lambda i: (i // tpb, 0))
    return pl.pallas_call(
        functools.partial(_merge_kernel, alpha=alpha),
        grid=(tokens // tm,),
        in_specs=[row(D_MODEL), row(NA_W), row(GQA_Q_W), row(MEM_W), memb, memb,
                  const(w_gate.shape), const(b_gate.shape), const(w_branch.shape), const(w_out.shape),
                  const(g1.shape), const(b1.shape), const(w_router.shape)],
        out_specs=[row(D_MODEL), pl.BlockSpec((1, N_EXPERTS, tm), lambda i: (i // tpb, 0, i % tpb))],
        out_shape=[jax.ShapeDtypeStruct((tokens, D_MODEL), F32),
                   jax.ShapeDtypeStruct((tokens // seq, N_EXPERTS, seq), F32)],
        compiler_params=_cparams(("parallel",)),
        name="merge",
    )(x2, yna, ygqa, qm, mk, mv, w_gate, b_gate, w_branch, w_out, g1, b1, w_router)


SEL_BLK = 256
SLOT_HI = 16
SLOT_LO = 32
SEL_PARTS = 5


def _prefix_counts(mask, out_ref, seq):
    r = lax.broadcasted_iota(jnp.int32, (SEL_BLK, SEL_BLK), 0)
    c = lax.broadcasted_iota(jnp.int32, (SEL_BLK, SEL_BLK), 1)
    tri = (r <= c).astype(BF16)
    run = jnp.zeros((mask.shape[0], 1), F32)
    for j in range(seq // SEL_BLK):
        cs = _dot(mask[:, j * SEL_BLK:(j + 1) * SEL_BLK].astype(BF16), tri) + run
        out_ref[:, j * SEL_BLK:(j + 1) * SEL_BLK] = cs
        run = cs[:, SEL_BLK - 1:SEL_BLK]


def _select_kernel(aff_ref, idx_ref, gate_ref, cnt_sc, lhs_sc, rhs_sc, *, seq, cap):
    a = aff_ref[0]
    thr_bits = jnp.zeros((N_EXPERTS, 1), jnp.int32)
    for bit in range(30, -1, -1):
        cand = thr_bits | (1 << bit)
        cnt = jnp.sum((a >= pltpu.bitcast(cand, F32)).astype(F32), axis=-1, keepdims=True)
        thr_bits = jnp.where(cnt >= cap, cand, thr_bits)
    thr = pltpu.bitcast(thr_bits, F32)
    gt = a > thr
    eq = a == thr
    need = cap - jnp.sum(gt.astype(F32), axis=-1, keepdims=True)
    _prefix_counts(eq.astype(F32), cnt_sc, seq)
    sel = gt | (eq & (cnt_sc[...] <= need))
    _prefix_counts(sel.astype(F32), cnt_sc, seq)

    slot = cnt_sc[...] - 1.0
    s_hi = jnp.floor(slot * (1.0 / SLOT_LO))
    s_lo = slot - SLOT_LO * s_hi
    tok = lax.broadcasted_iota(jnp.int32, (1, seq), 1).astype(F32)
    t_hi = jnp.floor(tok * (1.0 / 64.0))
    t_lo = tok - 64.0 * t_hi
    ph = lax.broadcasted_iota(jnp.int32, (SLOT_HI, seq), 0).astype(F32)
    pl_ = lax.broadcasted_iota(jnp.int32, (SLOT_LO, seq), 0).astype(F32)
    for e in range(N_EXPERTS):
        row = slice(e, e + 1)
        onehot_hi = jnp.where((s_hi[row] == ph) & sel[row], 1.0, 0.0)
        g = a[row]
        g1 = g.astype(BF16).astype(F32)
        g2 = (g - g1).astype(BF16).astype(F32)
        g3 = g - g1 - g2
        for part, val in enumerate((t_hi, t_lo, g1, g2, g3)):
            r0 = (part * N_EXPERTS + e) * SLOT_HI
            lhs_sc[r0:r0 + SLOT_HI, :] = (onehot_hi * val).astype(BF16)
        rhs_sc[e * SLOT_LO:(e + 1) * SLOT_LO, :] = jnp.where(s_lo[row] == pl_, 1.0, 0.0).astype(BF16)
    res = _dot_nt(lhs_sc[...], rhs_sc[...])
    n = N_EXPERTS * SLOT_HI
    r = lax.broadcasted_iota(jnp.int32, (n, N_EXPERTS * SLOT_LO), 0) // SLOT_HI
    c = lax.broadcasted_iota(jnp.int32, (n, N_EXPERTS * SLOT_LO), 1) // SLOT_LO
    parts = []
    for part in range(SEL_PARTS):
        blk = jnp.where(r == c, res[part * n:(part + 1) * n], 0.0)
        parts.append(jnp.sum(blk.reshape(N_EXPERTS, SLOT_HI, N_EXPERTS * SLOT_LO), axis=0))
    idx_ref[0] = (64.0 * parts[0] + parts[1]).astype(jnp.int32)
    gate_ref[0] = (parts[2] + parts[3]) + parts[4]


def _ec_select(aff, batch, seq, cap):
    assert cap == SLOT_HI * SLOT_LO
    out = pl.BlockSpec((1, SLOT_HI, N_EXPERTS * SLOT_LO), lambda b: (b, 0, 0))
    idx_t, gate_t = pl.pallas_call(
        functools.partial(_select_kernel, seq=seq, cap=cap),
        grid=(batch,),
        in_specs=[pl.BlockSpec((1, N_EXPERTS, seq), lambda b: (b, 0, 0))],
        out_specs=[out, out],
        out_shape=[jax.ShapeDtypeStruct((batch, SLOT_HI, N_EXPERTS * SLOT_LO), jnp.int32),
                   jax.ShapeDtypeStruct((batch, SLOT_HI, N_EXPERTS * SLOT_LO), F32)],
        scratch_shapes=[pltpu.VMEM((N_EXPERTS, seq), F32),
                        pltpu.VMEM((SEL_PARTS * N_EXPERTS * SLOT_HI, seq), BF16),
                        pltpu.VMEM((N_EXPERTS * SLOT_LO, seq), BF16)],
        compiler_params=_cparams(("parallel",)),
        name="ec_select",
    )(aff)
    order = lambda v: v.reshape(batch, SLOT_HI, N_EXPERTS, SLOT_LO).transpose(0, 2, 1, 3).reshape(batch, N_EXPERTS, cap)
    return order(idx_t), order(gate_t)


FFN_COL_PHASES = 2
LN2_TM = 256


def _gather_rows(idx_ref, x_ref, dst, lo, hi):
    for p in range(lo, hi):
        dst[pl.ds(p, 1), :] = x_ref[0, pl.ds(idx_ref[0, 0, p], 1), :]


def _scatter_rows(idx_ref, gate_ref, src, acc_sc, lo, hi):
    for p0 in range(lo, hi, 8):
        ps = range(p0, p0 + 8)
        ts = [idx_ref[0, 0, p] for p in ps]
        rows = [acc_sc[pl.ds(t, 1), :] + src[pl.ds(p, 1), :] * gate_ref[0, 0, p] for p, t in zip(ps, ts)]
        for t, row in zip(ts, rows):
            acc_sc[pl.ds(t, 1), :] = row


def _ffn_kernel(idx_ref, idx_next_ref, idx_prev_ref, gate_ref, gate_prev_ref, x_ref, wgu_ref, wd_ref,
                g2_ref, b2_ref, out_hbm, u_sc, xs_sc, act_sc, ye_sc, acc_sc, out_sc, sem_out,
                *, cap, d_ff, seq, alpha):
    b = pl.program_id(0)
    e = pl.program_id(1)

    @pl.when(e == 0)
    def _():
        acc_sc[...] = jnp.zeros(acc_sc.shape, F32)
        _gather_rows(idx_ref, x_ref, u_sc, 0, cap)

    @pl.when((e == 0) & (b == 0))
    def _():
        ye_sc[...] = jnp.zeros(ye_sc.shape, F32)

    xs_sc[...] = u_sc[...].astype(BF16)

    width = d_ff // FFN_COL_PHASES
    out_w = D_MODEL // FFN_COL_PHASES
    share = cap // FFN_COL_PHASES
    for ph in range(2 * FFN_COL_PHASES):
        @pl.when(e + ph >= 0)
        def _(ph=ph):
            if ph < FFN_COL_PHASES:
                _gather_rows(idx_next_ref, x_ref, u_sc, ph * share, (ph + 1) * share)
                _scatter_rows(idx_prev_ref, gate_prev_ref, ye_sc, acc_sc, ph * share, (ph + 1) * share)
                xs = xs_sc[...]
                hg = _dot(xs, wgu_ref[0, :, ph * width:(ph + 1) * width])
                hu = _dot(xs, wgu_ref[0, :, d_ff + ph * width:d_ff + (ph + 1) * width])
                act_sc[:, ph * width:(ph + 1) * width] = (hg * jax.nn.sigmoid(hg) * hu).astype(BF16)
            else:
                c = ph - FFN_COL_PHASES
                ye_sc[:, c * out_w:(c + 1) * out_w] = _dot(act_sc[...], wd_ref[0, :, c * out_w:(c + 1) * out_w])

    @pl.when(e == N_EXPERTS - 1)
    def _():
        def o_copy(i, slot):
            return pltpu.make_async_copy(out_sc.at[slot], out_hbm.at[b, pl.ds(i * LN2_TM, LN2_TM)], sem_out.at[slot])

        _scatter_rows(idx_ref, gate_ref, ye_sc, acc_sc, 0, cap)
        n_tiles = seq // LN2_TM
        for i in range(n_tiles):
            slot = i % 2
            rows = slice(i * LN2_TM, (i + 1) * LN2_TM)
            if i >= 2:
                o_copy(i - 2, slot).wait()
            out_sc[slot] = _layer_norm(alpha * x_ref[0, rows, :] + acc_sc[rows, :], g2_ref[...], b2_ref[...])
            o_copy(i, slot).start()
        o_copy(n_tiles - 2, n_tiles % 2).wait()
        o_copy(n_tiles - 1, (n_tiles - 1) % 2).wait()


def _ec_ffn(idx, gate, x1, wgu, wd, g2, b2, batch, seq, cap, alpha):
    d_ff = wd.shape[1]
    slot = pl.BlockSpec((1, 1, cap), lambda b, e: (b * N_EXPERTS + e, 0, 0), memory_space=pltpu.SMEM)
    vec = pl.BlockSpec((1, D_MODEL), lambda b, e: (0, 0))
    flat = lambda v: v.reshape(batch * N_EXPERTS, 1, cap)
    idx_next = jnp.concatenate([idx[:, 1:], idx[:, -1:]], axis=1)
    idx_prev = jnp.concatenate([idx[:, :1], idx[:, :-1]], axis=1)
    gate_prev = jnp.concatenate([jnp.zeros_like(gate[:, :1]), gate[:, :-1]], axis=1)
    return pl.pallas_call(
        functools.partial(_ffn_kernel, cap=cap, d_ff=d_ff, seq=seq, alpha=alpha),
        grid=(batch, N_EXPERTS),
        in_specs=[slot, slot, slot, slot, slot,
                  pl.BlockSpec((1, seq, D_MODEL), lambda b, e: (b, 0, 0), pipeline_mode=pl.Buffered(1)),
                  pl.BlockSpec((1, D_MODEL, 2 * d_ff), lambda b, e: (e, 0, 0)),
                  pl.BlockSpec((1, d_ff, D_MODEL), lambda b, e: (e, 0, 0)),
                  vec, vec],
        out_specs=pl.BlockSpec(memory_space=pl.ANY),
        out_shape=jax.ShapeDtypeStruct((batch, seq, D_MODEL), F32),
        scratch_shapes=[pltpu.VMEM((cap, D_MODEL), F32), pltpu.VMEM((cap, D_MODEL), BF16),
                        pltpu.VMEM((cap, d_ff), BF16), pltpu.VMEM((cap, D_MODEL), F32),
                        pltpu.VMEM((seq, D_MODEL), F32), pltpu.VMEM((2, LN2_TM, D_MODEL), F32),
                        pltpu.SemaphoreType.DMA((2,))],
        compiler_params=_cparams(("arbitrary", "arbitrary"), FFN_VMEM_LIMIT),
        name="ec_ffn",
    )(flat(idx), flat(idx_next), flat(idx_prev), flat(gate), flat(gate_prev), x1, wgu, wd, g2, b2)


def kernel(x, mem, w_in, b_gate, na_rpb, gqa_q_gain, gqa_k_gain, w_mem_kv, w_branch, w_out,
           ln1_g, ln1_b, w_router, w_gate_up, w_down, ln2_g, ln2_b):
    batch, seq, d = x.shape
    mem_tokens = mem.shape[1]
    depth = w_in.shape[0]
    alpha = (2 * depth) ** 0.25
    cap = EC_CAPACITY_FACTOR * seq // N_EXPERTS
    assert d == D_MODEL and seq == GRID_W * GRID_W and w_in.shape[2] == QKV_W + N_BRANCH * D_MODEL
    row = lambda v: v.reshape(1, -1)
    for l in range(depth):
        x2 = x.reshape(batch * seq, d)
        w_qkv = w_in[l, :, :QKV_W].astype(BF16)
        w_gate = w_in[l, :, QKV_W:].astype(BF16)
        qna, kna, vna, qg, kg, vg, qm = _in_proj(
            x2, w_qkv, row(jnp.tile(gqa_q_gain[l], GQA_HEADS)), row(jnp.tile(gqa_k_gain[l], GQA_KV_HEADS)),
            batch, seq)
        mk, mv = _mem_kv(mem.reshape(batch * mem_tokens, d), w_mem_kv[l].astype(BF16))
        yna = _na_attention(qna, kna, vna, _na_bias_tiles(na_rpb[l]), batch, seq)
        ygqa = _gqa_attention(qg, kg, vg, batch, seq)
        x1, aff = _merge(x2, yna, ygqa, qm, mk, mv, w_gate, row(b_gate[l]), w_branch[l].astype(BF16),
                              w_out[l].astype(BF16), row(ln1_g[l]), row(ln1_b[l]), w_router[l].T, seq,
                              mem_tokens, alpha)
        idx, gate = _ec_select(aff, batch, seq, cap)
        x = _ec_ffn(idx, gate, x1.reshape(batch, seq, d), w_gate_up[l].astype(BF16), w_down[l].astype(BF16),
                    row(ln2_g[l]), row(ln2_b[l]), batch, seq, cap, alpha)
    return x
```

```python
import functools
import math

import jax
import jax.numpy as jnp
import numpy as np
from jax import lax
from jax.experimental import pallas as pl
from jax.experimental.pallas import tpu as pltpu

F32 = jnp.float32
BF16 = jnp.bfloat16

D_MODEL = 1024
GRID_W = 64
HEAD_DIM = 64
LANES = 128
NA_HEADS = 8
NA_KH = 8
NA_KW = 16
NA_W = NA_HEADS * HEAD_DIM
NA_QROWS = 4
NA_KROWS = 12
GQA_HEADS = 8
GQA_KV_HEADS = 2
GQA_GROUP = GQA_HEADS // GQA_KV_HEADS
GQA_Q_W = GQA_HEADS * HEAD_DIM
GQA_KV_W = GQA_KV_HEADS * HEAD_DIM
ROPE_THETA = 10000.0
MEM_HEADS = 4
MEM_HEAD_DIM = 128
MEM_W = MEM_HEADS * MEM_HEAD_DIM
N_BRANCH = 3
N_EXPERTS = 16
EC_CAPACITY_FACTOR = 2
LN_EPS = 1e-5
RMS_EPS = 1e-6
NEG_INF = -1e30
QKV_W = 3 * NA_W + GQA_Q_W + 2 * GQA_KV_W + MEM_W

VMEM_LIMIT = 56 * 1024 * 1024
FFN_VMEM_LIMIT = 60 * 1024 * 1024


def _cparams(sem, vmem=VMEM_LIMIT):
    return pltpu.CompilerParams(dimension_semantics=sem, vmem_limit_bytes=vmem)


def _dot(a, b):
    return jnp.dot(a, b, preferred_element_type=F32)


def _dot_nt(a, b):
    return lax.dot_general(a, b, (((1,), (1,)), ((), ())), preferred_element_type=F32)


def _layer_norm(v, g, b):
    mu = jnp.mean(v, axis=-1, keepdims=True)
    var = jnp.mean(jnp.square(v - mu), axis=-1, keepdims=True)
    return (v - mu) * lax.rsqrt(var + LN_EPS) * g + b


def _na_bias_kernel(rpb_ref, out_ref):
    n = GRID_W * GRID_W
    col = lax.broadcasted_iota(jnp.int32, (32, n), 1)
    d = lax.broadcasted_iota(jnp.int32, (32, n), 0)
    c, kc = col // GRID_W, col % GRID_W
    onehot = (kc - c + (NA_KW - 1) == d).astype(F32)
    t = jnp.dot(rpb_ref[...], onehot, precision=lax.Precision.HIGHEST, preferred_element_type=F32)
    col1 = lax.broadcasted_iota(jnp.int32, (1, n), 1)
    c1, kc1 = col1 // GRID_W, col1 % GRID_W
    start = jnp.clip(c1 - NA_KW // 2, 0, GRID_W - NA_KW)
    in_win = (kc1 >= start) & (kc1 < start + NA_KW)
    out_ref[...] = jnp.where(in_win, t * math.log2(math.e), NEG_INF)


def _na_bias_tiles(rpb):
    h = rpb.shape[0]
    nd = 2 * NA_KH - 1
    rpb2 = jnp.pad(rpb.reshape(h * nd, 2 * NA_KW - 1), ((0, 0), (0, 1)))
    t = pl.pallas_call(
        _na_bias_kernel,
        out_shape=jax.ShapeDtypeStruct((h * nd, GRID_W * GRID_W), F32),
        name="na_bias",
    )(rpb2).reshape(h, nd, GRID_W, GRID_W)
    t = jnp.concatenate([t, jnp.full((h, 1, GRID_W, GRID_W), NEG_INF, F32)], axis=1)
    which = np.full((3, NA_QROWS, NA_KROWS), nd, np.int32)
    for variant in range(3):
        for i in range(NA_QROWS):
            for j in range(NA_KROWS):
                if variant == 0:
                    dr, ok = j - i, j < NA_KH
                elif variant == 1:
                    dr = j - i - NA_KH // 2
                    ok = -(NA_KH // 2) <= dr < NA_KH // 2
                else:
                    dr, ok = j - i - NA_KH, j >= NA_KROWS - NA_KH
                if ok:
                    which[variant, i, j] = dr + NA_KH - 1
    return pl.pallas_call(
        functools.partial(_na_tiles_kernel, which=which),
        grid=(3, h),
        in_specs=[pl.BlockSpec((1, nd + 1, GRID_W, GRID_W), lambda v, hh: (hh, 0, 0, 0))],
        out_specs=pl.BlockSpec((1, 1, NA_QROWS * GRID_W, NA_KROWS * GRID_W), lambda v, hh: (v, hh, 0, 0)),
        out_shape=jax.ShapeDtypeStruct((3, h, NA_QROWS * GRID_W, NA_KROWS * GRID_W), F32),
        compiler_params=_cparams(("parallel", "parallel")),
        name="na_tiles",
    )(t)


def _na_tiles_kernel(t_ref, o_ref, *, which):
    for variant in range(which.shape[0]):
        @pl.when(pl.program_id(0) == variant)
        def _(variant=variant):
            for i in range(NA_QROWS):
                for j in range(0, NA_KROWS, 2):
                    pair = jnp.concatenate([t_ref[0, int(which[variant, i, j])],
                                            t_ref[0, int(which[variant, i, j + 1])]], axis=-1)
                    o_ref[0, 0, i * GRID_W:(i + 1) * GRID_W, j * GRID_W:(j + 2) * GRID_W] = pair


PROJ_TM = 1024


def _rope_tables(seq_len):
    t = jnp.arange(seq_len)
    row = (t // GRID_W).astype(F32)
    col = (t % GRID_W).astype(F32)
    half = HEAD_DIM // 2
    inv = ROPE_THETA ** (-jnp.arange(0, half, 2, dtype=F32) / half)
    ang = jnp.concatenate([row[:, None] * inv, col[:, None] * inv], axis=-1)
    cos, sin = jnp.cos(ang), jnp.sin(ang)
    cos_l = jnp.repeat(cos, 2, axis=-1)
    sin_l = jnp.stack([-sin, sin], axis=-1).reshape(seq_len, HEAD_DIM)
    return jnp.tile(cos_l, (1, 2)), jnp.tile(sin_l, (1, 2))


def _head_sums(sq, width):
    r = lax.broadcasted_iota(jnp.int32, (width, width), 0) // HEAD_DIM
    c = lax.broadcasted_iota(jnp.int32, (width, width), 1) // HEAD_DIM
    ones = (r == c).astype(BF16)
    hi = sq.astype(BF16)
    lo = (sq - hi.astype(F32)).astype(BF16)
    return _dot(hi, ones) + _dot(lo, ones)


def _rope(v, cos, sin_signed):
    lane = lax.broadcasted_iota(jnp.int32, v.shape, 1)
    swapped = jnp.where(lane % 2 == 0, pltpu.roll(v, LANES - 1, 1), pltpu.roll(v, 1, 1))
    return v * cos + swapped * sin_signed


def _proj_kernel(x_ref, w_ref, qgain_ref, kgain_ref, cos_ref, sin_ref,
                 qna_ref, kna_ref, vna_ref, qg_ref, kg_ref, vg_ref, qm_ref):
    xb = x_ref[...].astype(BF16)
    qscale = math.log2(math.e) / math.sqrt(HEAD_DIM)
    o = 0
    qna_ref[...] = (_dot(xb, w_ref[:, o:o + NA_W]) * qscale).astype(BF16)
    o += NA_W
    kna_ref[...] = _dot(xb, w_ref[:, o:o + NA_W]).astype(BF16)
    o += NA_W
    vna_ref[...] = _dot(xb, w_ref[:, o:o + NA_W]).astype(BF16)
    o += NA_W
    cos, sin = cos_ref[...], sin_ref[...]
    lane = lax.broadcasted_iota(jnp.int32, (PROJ_TM, LANES), 1)
    low = lane < HEAD_DIM

    zq = _dot(xb, w_ref[:, o:o + GQA_Q_W])
    o += GQA_Q_W
    ms = _head_sums(zq * zq, GQA_Q_W) * (1.0 / HEAD_DIM)
    zq = zq * lax.rsqrt(ms + RMS_EPS) * qgain_ref[...]
    for p in range(GQA_HEADS // 2):
        r = _rope(zq[:, p * LANES:(p + 1) * LANES], cos, sin) * qscale
        rolled = pltpu.roll(r, HEAD_DIM, 1)
        if (2 * p) // GQA_GROUP == 0:
            even, odd = jnp.where(low, r, 0.0), jnp.where(low, rolled, 0.0)
        else:
            even, odd = jnp.where(low, 0.0, rolled), jnp.where(low, 0.0, r)
        qg_ref[0, 2 * p] = even.astype(BF16)
        qg_ref[0, 2 * p + 1] = odd.astype(BF16)

    zk = _dot(xb, w_ref[:, o:o + GQA_KV_W])
    o += GQA_KV_W
    ms = _head_sums(zk * zk, GQA_KV_W) * (1.0 / HEAD_DIM)
    zk = zk * lax.rsqrt(ms + RMS_EPS) * kgain_ref[...]
    kg_ref[...] = _rope(zk, cos, sin).astype(BF16)
    zv = _dot(xb, w_ref[:, o:o + GQA_KV_W])
    vg_ref[0] = jnp.where(low, zv, 1.0).astype(BF16)
    vg_ref[1] = jnp.where(low, 1.0, zv).astype(BF16)
    o += GQA_KV_W
    qm_ref[...] = _dot(xb, w_ref[:, o:o + MEM_W]).astype(BF16)


def _in_proj(x2, w_qkv, qgain, kgain, batch, seq):
    tokens = x2.shape[0]
    tm = PROJ_TM
    spb = seq // tm
    cos, sin = _rope_tables(seq)
    row = lambda w: pl.BlockSpec((tm, w), lambda i: (i, 0))
    const = lambda shape: pl.BlockSpec(shape, lambda i: (0,) * len(shape), pipeline_mode=pl.Buffered(1))
    tbl = pl.BlockSpec((tm, LANES), lambda i: (i % spb, 0))
    sds = lambda w: jax.ShapeDtypeStruct((tokens, w), BF16)
    return pl.pallas_call(
        _proj_kernel,
        grid=(tokens // tm,),
        in_specs=[row(D_MODEL), const((D_MODEL, QKV_W)), const((1, GQA_Q_W)), const((1, GQA_KV_W)), tbl, tbl],
        out_specs=[row(NA_W), row(NA_W), row(NA_W),
                   pl.BlockSpec((1, GQA_HEADS, tm, LANES), lambda i: (i // spb, 0, i % spb, 0)),
                   row(GQA_KV_W), pl.BlockSpec((GQA_KV_HEADS, tm, LANES), lambda i: (0, i, 0)), row(MEM_W)],
        out_shape=[sds(NA_W), sds(NA_W), sds(NA_W),
                   jax.ShapeDtypeStruct((batch, GQA_HEADS, seq, LANES), BF16),
                   sds(GQA_KV_W), jax.ShapeDtypeStruct((GQA_KV_HEADS, tokens, LANES), BF16), sds(MEM_W)],
        compiler_params=_cparams(("parallel",)),
        name="in_proj",
    )(x2, w_qkv, qgain, kgain, cos, sin)


def _memkv_kernel(m_ref, w_ref, k_ref, v_ref):
    z = _dot(m_ref[...].astype(BF16), w_ref[...])
    k_ref[...] = z[:, :MEM_W].astype(BF16)
    v_ref[...] = z[:, MEM_W:].astype(BF16)


def _mem_kv(mem2, w_kv):
    rows = mem2.shape[0]
    tm = 256
    return pl.pallas_call(
        _memkv_kernel,
        grid=(rows // tm,),
        in_specs=[pl.BlockSpec((tm, D_MODEL), lambda i: (i, 0)),
                  pl.BlockSpec((D_MODEL, 2 * MEM_W), lambda i: (0, 0))],
        out_specs=[pl.BlockSpec((tm, MEM_W), lambda i: (i, 0))] * 2,
        out_shape=[jax.ShapeDtypeStruct((rows, MEM_W), BF16)] * 2,
        compiler_params=_cparams(("parallel",)),
        name="mem_kv",
    )(mem2, w_kv)


NA_TQ = NA_QROWS * GRID_W
NA_TK = NA_KROWS * GRID_W


NA_GROUPS_PER_STEP = 8


def _na_kernel(q_ref, k_ref, v_ref, bias_ref, o_ref, *, n_groups):
    low = lax.broadcasted_iota(jnp.int32, (NA_TQ, LANES), 1) < HEAD_DIM
    low_k = lax.broadcasted_iota(jnp.int32, (NA_TK, LANES), 1) < HEAD_DIM
    tiles = NA_TK // LANES
    for sub in range(NA_GROUPS_PER_STEP):
        rg = pl.program_id(2) * NA_GROUPS_PER_STEP + sub
        key_row0 = jnp.clip(NA_QROWS * rg - NA_KH // 2, 0, GRID_W - NA_KROWS)
        start = pl.multiple_of(key_row0 * GRID_W, GRID_W)
        variant = jnp.where(rg == 0, 0, jnp.where(rg == n_groups - 1, 2, 1))
        k = k_ref[pl.ds(start, NA_TK), :]
        v = v_ref[pl.ds(start, NA_TK), :]
        q = q_ref[sub * NA_TQ:(sub + 1) * NA_TQ, :]
        outs = []
        for hh in range(2):
            mine, mine_k = (low, low_k) if hh == 0 else (jnp.logical_not(low), jnp.logical_not(low_k))
            qh = jnp.where(mine, q, jnp.zeros_like(q))
            vh = jnp.where(mine_k, v, jnp.ones_like(v))
            s = _dot_nt(qh, k) + bias_ref[variant, hh]
            st = [s[:, j * LANES:(j + 1) * LANES] for j in range(tiles)]
            part = st[0]
            for t in st[1:]:
                part = jnp.maximum(part, t)
            m = jnp.broadcast_to(jnp.max(part, axis=-1, keepdims=True), part.shape)
            p = jnp.concatenate([jnp.exp2(t - m).astype(BF16) for t in st], axis=-1)
            acc = _dot(p, vh)
            outs.append(acc / pltpu.roll(acc, HEAD_DIM, 1))
        o_ref[sub * NA_TQ:(sub + 1) * NA_TQ, :] = jnp.where(low, outs[0], outs[1]).astype(BF16)


def _na_attention(qna, kna, vna, bias, batch, seq):
    n_groups = seq // NA_TQ
    steps = n_groups // NA_GROUPS_PER_STEP
    pairs = NA_HEADS // 2
    kv = pl.BlockSpec((seq, LANES), lambda p, b, g: (b, p))
    qo = pl.BlockSpec((NA_GROUPS_PER_STEP * NA_TQ, LANES), lambda p, b, g: (b * steps + g, p))
    return pl.pallas_call(
        functools.partial(_na_kernel, n_groups=n_groups),
        grid=(pairs, batch, steps),
        in_specs=[qo, kv, kv, pl.BlockSpec((3, 2, NA_TQ, NA_TK), lambda p, b, g: (0, p, 0, 0))],
        out_specs=qo,
        out_shape=jax.ShapeDtypeStruct(qna.shape, BF16),
        compiler_params=_cparams(("parallel", "parallel", "parallel")),
        name="na_attn",
    )(qna, kna, vna, bias)


GQA_TQ = 128
GQA_TK = 1024


def _gqa_kernel(q_ref, k_ref, v_ref, wgu_ref, wd_ref, o_ref, wgu_bf_ref, wd_bf_ref, s_sc, mx_sc, acc_sc, *, seq):
    wgu_bf_ref[...] = wgu_ref[...].astype(BF16)
    wd_bf_ref[...] = wd_ref[...].astype(BF16)
    rows = GQA_GROUP * GQA_TQ
    n_chunks = seq // GQA_TK
    tiles = GQA_TK // LANES
    groups = range(GQA_KV_HEADS)
    lane = lax.broadcasted_iota(jnp.int32, (GQA_TQ, LANES), 1)
    low = lane < HEAD_DIM
    qs = [q_ref[0, g * GQA_GROUP:(g + 1) * GQA_GROUP].reshape(rows, LANES) for g in groups]
    mx_sc[...] = jnp.full(mx_sc.shape, -jnp.inf, F32)

    for c in range(n_chunks):
        k = k_ref[c * GQA_TK:(c + 1) * GQA_TK, :]
        for g in groups:
            s = _dot_nt(qs[g], k)
            s_sc[g, c] = s
            part = s[:, :LANES]
            for j in range(1, tiles):
                part = jnp.maximum(part, s[:, j * LANES:(j + 1) * LANES])
            mx_sc[g] = jnp.maximum(mx_sc[g], part)
    ms = [jnp.broadcast_to(jnp.max(mx_sc[g], axis=-1, keepdims=True), (rows, LANES)) for g in groups]
    acc_sc[...] = jnp.zeros(acc_sc.shape, F32)

    for c in range(n_chunks):
        for g in groups:
            p = [jnp.exp2(s_sc[g, c, :, j * LANES:(j + 1) * LANES] - ms[g]).astype(BF16) for j in range(tiles)]
            acc_sc[g] += _dot(jnp.concatenate(p, axis=-1), v_ref[g, c * GQA_TK:(c + 1) * GQA_TK, :])
    for g in groups:
        acc = acc_sc[g]
        out = acc / pltpu.roll(acc, HEAD_DIM, 1)
        for pp in range(GQA_GROUP // 2):
            a = out[(2 * pp) * GQA_TQ:(2 * pp + 1) * GQA_TQ]
            b = out[(2 * pp + 1) * GQA_TQ:(2 * pp + 2) * GQA_TQ]
            if g == 0:
                pair = jnp.where(low, a, pltpu.roll(b, HEAD_DIM, 1))
            else:
                pair = jnp.where(low, pltpu.roll(a, HEAD_DIM, 1), b)
            col = (g * (GQA_GROUP // 2) + pp) * LANES
            o_ref[:, col:col + LANES] = pair.astype(BF16)


def _gqa_attention(qg, kg, vg, w_gate_up, w_down, batch, seq):
    nq = seq // GQA_TQ
    rows = GQA_GROUP * GQA_TQ
    stat = pltpu.VMEM((GQA_KV_HEADS, rows, LANES), F32)
    n_exp, w_rows = w_gate_up.shape[:2]
    slabs = batch * nq // n_exp
    slab = w_rows // slabs
    assert slabs * n_exp == batch * nq and slab * slabs == w_rows and w_down.shape[:2] == (n_exp, w_rows)
    wspec = lambda w: pl.BlockSpec((1, slab, w.shape[2]),
                                   lambda b, i: ((b * nq + i) // slabs, (b * nq + i) % slabs, 0))
    return pl.pallas_call(
        functools.partial(_gqa_kernel, seq=seq),
        grid=(batch, nq),
        in_specs=[pl.BlockSpec((1, GQA_HEADS, GQA_TQ, LANES), lambda b, i: (b, 0, i, 0)),
                  pl.BlockSpec((seq, LANES), lambda b, i: (b, 0)),
                  pl.BlockSpec((GQA_KV_HEADS, seq, LANES), lambda b, i: (0, b, 0)),
                  wspec(w_gate_up), wspec(w_down)],
        out_specs=[pl.BlockSpec((GQA_TQ, GQA_Q_W), lambda b, i: (b * nq + i, 0)), wspec(w_gate_up), wspec(w_down)],
        out_shape=[jax.ShapeDtypeStruct((batch * seq, GQA_Q_W), BF16),
                   jax.ShapeDtypeStruct(w_gate_up.shape, BF16), jax.ShapeDtypeStruct(w_down.shape, BF16)],
        scratch_shapes=[pltpu.VMEM((GQA_KV_HEADS, seq // GQA_TK, rows, GQA_TK), F32), stat, stat],
        compiler_params=_cparams(("parallel", "parallel")),
        name="gqa_attn",
    )(qg, kg, vg, w_gate_up, w_down)


MERGE_TM = 512


def _merge_kernel(x_ref, yna_ref, ygqa_ref, qm_ref, mk_ref, mv_ref, wg_ref, bg_ref, wb_ref, wo_ref,
                  g1_ref, b1_ref, wr_ref, x1_ref, aff_ref, *, alpha):
    x = x_ref[...]
    xb = x.astype(BF16)
    mem_scale = 1.0 / math.sqrt(MEM_HEAD_DIM)
    ymem = []
    for h in range(MEM_HEADS):
        sl = slice(h * MEM_HEAD_DIM, (h + 1) * MEM_HEAD_DIM)
        s = _dot_nt(qm_ref[:, sl], mk_ref[:, sl]) * mem_scale
        m = jnp.max(s, axis=-1, keepdims=True)
        p = jnp.exp(s - m)
        l = jnp.sum(p, axis=-1, keepdims=True)
        ymem.append((_dot(p.astype(BF16), mv_ref[:, sl]) / l).astype(BF16))
    branches = (yna_ref[...], ygqa_ref[...], jnp.concatenate(ymem, axis=-1))
    merged = jnp.zeros((MERGE_TM, D_MODEL), F32)
    for g in range(N_BRANCH):
        cols = slice(g * D_MODEL, (g + 1) * D_MODEL)
        gate = jax.nn.sigmoid(_dot(xb, wg_ref[:, cols]) + bg_ref[:, cols])
        merged = merged + gate * _dot(branches[g], wb_ref[g])
    mix = _dot(merged.astype(BF16), wo_ref[...])
    x1 = _layer_norm(alpha * x + mix, g1_ref[...], b1_ref[...])
    x1_ref[...] = x1
    x1_hi = x1.astype(BF16)
    x1_lo = (x1 - x1_hi.astype(F32)).astype(BF16)
    w = wr_ref[...]
    w_hi = w.astype(BF16)
    w_lo = (w - w_hi.astype(F32)).astype(BF16)
    by_hi = _dot_nt(jnp.concatenate([w_hi, w_lo], axis=0), x1_hi)
    logits = by_hi[:N_EXPERTS] + by_hi[N_EXPERTS:] + _dot_nt(w_hi, x1_lo)
    m = jnp.max(logits, axis=0, keepdims=True)
    e = jnp.exp(logits - m)
    aff_ref[0] = e / jnp.sum(e, axis=0, keepdims=True)


def _merge(x2, yna, ygqa, qm, mk, mv, w_gate, b_gate, w_branch, w_out, g1, b1, w_router, seq, mem_tokens, alpha):
    tokens = x2.shape[0]
    tm = MERGE_TM
    tpb = seq // tm
    row = lambda w: pl.BlockSpec((tm, w), lambda i: (i, 0))
    const = lambda shape: pl.BlockSpec(shape, lambda i: (0,) * len(shape), pipeline_mode=pl.Buffered(1))
    memb = pl.BlockSpec((mem_tokens, MEM_W), lambda i: (i // tpb, 0))
    return pl.pallas_call(
        functools.partial(_merge_kernel, alpha=alpha),
        grid=(tokens // tm,),
        in_specs=[row(D_MODEL), row(NA_W), row(GQA_Q_W), row(MEM_W), memb, memb,
                  const(w_gate.shape), const(b_gate.shape), const(w_branch.shape), const(w_out.shape),
                  const(g1.shape), const(b1.shape), const(w_router.shape)],
        out_specs=[row(D_MODEL), pl.BlockSpec((1, N_EXPERTS, tm), lambda i: (i // tpb, 0, i % tpb))],
        out_shape=[jax.ShapeDtypeStruct((tokens, D_MODEL), F32),
                   jax.ShapeDtypeStruct((tokens // seq, N_EXPERTS, seq), F32)],
        compiler_params=_cparams(("parallel",)),
        name="merge",
    )(x2, yna, ygqa, qm, mk, mv, w_gate, b_gate, w_branch, w_out, g1, b1, w_router)


SEL_BLK = 256
SLOT_HI = 16
SLOT_LO = 32
SEL_PARTS = 5


def _prefix_counts(mask, out_ref, seq):
    r = lax.broadcasted_iota(jnp.int32, (SEL_BLK, SEL_BLK), 0)
    c = lax.broadcasted_iota(jnp.int32, (SEL_BLK, SEL_BLK), 1)
    tri = (r <= c).astype(BF16)
    run = jnp.zeros((mask.shape[0], 1), F32)
    for j in range(seq // SEL_BLK):
        cs = _dot(mask[:, j * SEL_BLK:(j + 1) * SEL_BLK].astype(BF16), tri) + run
        out_ref[:, j * SEL_BLK:(j + 1) * SEL_BLK] = cs
        run = cs[:, SEL_BLK - 1:SEL_BLK]


def _select_kernel(aff_ref, idx_ref, gate_ref, cnt_sc, lhs_sc, rhs_sc, *, seq, cap):
    a = aff_ref[0]
    thr_bits = jnp.zeros((N_EXPERTS, 1), jnp.int32)
    for bit in range(30, -1, -1):
        cand = thr_bits | (1 << bit)
        cnt = jnp.sum((a >= pltpu.bitcast(cand, F32)).astype(F32), axis=-1, keepdims=True)
        thr_bits = jnp.where(cnt >= cap, cand, thr_bits)
    thr = pltpu.bitcast(thr_bits, F32)
    gt = a > thr
    eq = a == thr
    need = cap - jnp.sum(gt.astype(F32), axis=-1, keepdims=True)
    _prefix_counts(eq.astype(F32), cnt_sc, seq)
    sel = gt | (eq & (cnt_sc[...] <= need))
    _prefix_counts(sel.astype(F32), cnt_sc, seq)

    slot = cnt_sc[...] - 1.0
    s_hi = jnp.floor(slot * (1.0 / SLOT_LO))
    s_lo = slot - SLOT_LO * s_hi
    tok = lax.broadcasted_iota(jnp.int32, (1, seq), 1).astype(F32)
    t_hi = jnp.floor(tok * (1.0 / 64.0))
    t_lo = tok - 64.0 * t_hi
    ph = lax.broadcasted_iota(jnp.int32, (SLOT_HI, seq), 0).astype(F32)
    pl_ = lax.broadcasted_iota(jnp.int32, (SLOT_LO, seq), 0).astype(F32)
    for e in range(N_EXPERTS):
        row = slice(e, e + 1)
        onehot_hi = jnp.where((s_hi[row] == ph) & sel[row], 1.0, 0.0)
        g = a[row]
        g1 = g.astype(BF16).astype(F32)
        g2 = (g - g1).astype(BF16).astype(F32)
        g3 = g - g1 - g2
        for part, val in enumerate((t_hi, t_lo, g1, g2, g3)):
            r0 = (part * N_EXPERTS + e) * SLOT_HI
            lhs_sc[r0:r0 + SLOT_HI, :] = (onehot_hi * val).astype(BF16)
        rhs_sc[e * SLOT_LO:(e + 1) * SLOT_LO, :] = jnp.where(s_lo[row] == pl_, 1.0, 0.0).astype(BF16)
    res = _dot_nt(lhs_sc[...], rhs_sc[...])
    n = N_EXPERTS * SLOT_HI
    r = lax.broadcasted_iota(jnp.int32, (n, N_EXPERTS * SLOT_LO), 0) // SLOT_HI
    c = lax.broadcasted_iota(jnp.int32, (n, N_EXPERTS * SLOT_LO), 1) // SLOT_LO
    parts = []
    for part in range(SEL_PARTS):
        blk = jnp.where(r == c, res[part * n:(part + 1) * n], 0.0)
        parts.append(jnp.sum(blk.reshape(N_EXPERTS, SLOT_HI, N_EXPERTS * SLOT_LO), axis=0))
    idx_ref[0] = (64.0 * parts[0] + parts[1]).astype(jnp.int32)
    gate_ref[0] = (parts[2] + parts[3]) + parts[4]


def _ec_select(aff, batch, seq, cap):
    assert cap == SLOT_HI * SLOT_LO
    out = pl.BlockSpec((1, SLOT_HI, N_EXPERTS * SLOT_LO), lambda b: (b, 0, 0))
    idx_t, gate_t = pl.pallas_call(
        functools.partial(_select_kernel, seq=seq, cap=cap),
        grid=(batch,),
        in_specs=[pl.BlockSpec((1, N_EXPERTS, seq), lambda b: (b, 0, 0))],
        out_specs=[out, out],
        out_shape=[jax.ShapeDtypeStruct((batch, SLOT_HI, N_EXPERTS * SLOT_LO), jnp.int32),
                   jax.ShapeDtypeStruct((batch, SLOT_HI, N_EXPERTS * SLOT_LO), F32)],
        scratch_shapes=[pltpu.VMEM((N_EXPERTS, seq), F32),
                        pltpu.VMEM((SEL_PARTS * N_EXPERTS * SLOT_HI, seq), BF16),
                        pltpu.VMEM((N_EXPERTS * SLOT_LO, seq), BF16)],
        compiler_params=_cparams(("parallel",)),
        name="ec_select",
    )(aff)
    order = lambda v: v.reshape(batch, SLOT_HI, N_EXPERTS, SLOT_LO).transpose(0, 2, 1, 3).reshape(batch, N_EXPERTS, cap)
    return order(idx_t), order(gate_t)


FFN_COL_PHASES = 1
LN2_TM = 256


def _gather_rows(idx_ref, x_ref, dst, lo, hi):
    for p in range(lo, hi):
        dst[pl.ds(p, 1), :] = x_ref[0, pl.ds(idx_ref[0, 0, p], 1), :]


def _scatter_rows(idx_ref, gate_ref, src, acc_sc, lo, hi):
    for p0 in range(lo, hi, 8):
        ps = range(p0, p0 + 8)
        ts = [idx_ref[0, 0, p] for p in ps]
        rows = [acc_sc[pl.ds(t, 1), :] + src[pl.ds(p, 1), :] * gate_ref[0, 0, p] for p, t in zip(ps, ts)]
        for t, row in zip(ts, rows):
            acc_sc[pl.ds(t, 1), :] = row


def _ffn_kernel(idx_ref, idx_next_ref, idx_prev_ref, gate_ref, gate_prev_ref, x_ref, wgu_ref, wd_ref,
                g2_ref, b2_ref, out_hbm, u_sc, xs_sc, act_sc, ye_sc, acc_sc, out_sc, sem_out,
                *, cap, d_ff, seq, alpha):
    b = pl.program_id(0)
    e = pl.program_id(1)

    @pl.when(e == 0)
    def _():
        acc_sc[...] = jnp.zeros(acc_sc.shape, F32)
        _gather_rows(idx_ref, x_ref, u_sc, 0, cap)

    @pl.when((e == 0) & (b == 0))
    def _():
        ye_sc[...] = jnp.zeros(ye_sc.shape, F32)

    xs_sc[...] = u_sc[...].astype(BF16)

    width = d_ff // FFN_COL_PHASES
    out_w = D_MODEL // FFN_COL_PHASES
    share = cap // FFN_COL_PHASES
    for ph in range(2 * FFN_COL_PHASES):
        @pl.when(e + ph >= 0)
        def _(ph=ph):
            if ph < FFN_COL_PHASES:
                _scatter_rows(idx_prev_ref, gate_prev_ref, ye_sc, acc_sc, ph * share, (ph + 1) * share)
                xs = xs_sc[...]
                hg = _dot(xs, wgu_ref[0, :, ph * width:(ph + 1) * width])
                hu = _dot(xs, wgu_ref[0, :, d_ff + ph * width:d_ff + (ph + 1) * width])
                act_sc[:, ph * width:(ph + 1) * width] = (hg * jax.nn.sigmoid(hg) * hu).astype(BF16)
            else:
                c = ph - FFN_COL_PHASES
                _gather_rows(idx_next_ref, x_ref, u_sc, c * share, (c + 1) * share)
                ye_sc[:, c * out_w:(c + 1) * out_w] = _dot(act_sc[...], wd_ref[0, :, c * out_w:(c + 1) * out_w])

    @pl.when(e == N_EXPERTS - 1)
    def _():
        def o_copy(i, slot):
            return pltpu.make_async_copy(out_sc.at[slot], out_hbm.at[b, pl.ds(i * LN2_TM, LN2_TM)], sem_out.at[slot])

        _scatter_rows(idx_ref, gate_ref, ye_sc, acc_sc, 0, cap)
        n_tiles = seq // LN2_TM
        for i in range(n_tiles):
            slot = i % 2
            rows = slice(i * LN2_TM, (i + 1) * LN2_TM)
            if i >= 2:
                o_copy(i - 2, slot).wait()
            out_sc[slot] = _layer_norm(alpha * x_ref[0, rows, :] + acc_sc[rows, :], g2_ref[...], b2_ref[...])
            o_copy(i, slot).start()
        o_copy(n_tiles - 2, n_tiles % 2).wait()
        o_copy(n_tiles - 1, (n_tiles - 1) % 2).wait()


def _ec_ffn(idx, gate, x1, wgu, wd, g2, b2, batch, seq, cap, alpha):
    d_ff = wd.shape[1]
    slot = pl.BlockSpec((1, 1, cap), lambda b, e: (b * N_EXPERTS + e, 0, 0), memory_space=pltpu.SMEM)
    vec = pl.BlockSpec((1, D_MODEL), lambda b, e: (0, 0))
    flat = lambda v: v.reshape(batch * N_EXPERTS, 1, cap)
    idx_next = jnp.concatenate([idx[:, 1:], idx[:, -1:]], axis=1)
    idx_prev = jnp.concatenate([idx[:, :1], idx[:, :-1]], axis=1)
    gate_prev = jnp.concatenate([jnp.zeros_like(gate[:, :1]), gate[:, :-1]], axis=1)
    return pl.pallas_call(
        functools.partial(_ffn_kernel, cap=cap, d_ff=d_ff, seq=seq, alpha=alpha),
        grid=(batch, N_EXPERTS),
        in_specs=[slot, slot, slot, slot, slot,
                  pl.BlockSpec((1, seq, D_MODEL), lambda b, e: (b, 0, 0), pipeline_mode=pl.Buffered(1)),
                  pl.BlockSpec((1, D_MODEL, 2 * d_ff), lambda b, e: (e, 0, 0)),
                  pl.BlockSpec((1, d_ff, D_MODEL), lambda b, e: (e, 0, 0)),
                  vec, vec],
        out_specs=pl.BlockSpec(memory_space=pl.ANY),
        out_shape=jax.ShapeDtypeStruct((batch, seq, D_MODEL), F32),
        scratch_shapes=[pltpu.VMEM((cap, D_MODEL), F32), pltpu.VMEM((cap, D_MODEL), BF16),
                        pltpu.VMEM((cap, d_ff), BF16), pltpu.VMEM((cap, D_MODEL), F32),
                        pltpu.VMEM((seq, D_MODEL), F32), pltpu.VMEM((2, LN2_TM, D_MODEL), F32),
                        pltpu.SemaphoreType.DMA((2,))],
        compiler_params=_cparams(("arbitrary", "arbitrary"), FFN_VMEM_LIMIT),
        name="ec_ffn",
    )(flat(idx), flat(idx_next), flat(idx_prev), flat(gate), flat(gate_prev), x1, wgu, wd, g2, b2)


def kernel(x, mem, w_in, b_gate, na_rpb, gqa_q_gain, gqa_k_gain, w_mem_kv, w_branch, w_out,
           ln1_g, ln1_b, w_router, w_gate_up, w_down, ln2_g, ln2_b):
    batch, seq, d = x.shape
    mem_tokens = mem.shape[1]
    depth = w_in.shape[0]
    alpha = (2 * depth) ** 0.25
    cap = EC_CAPACITY_FACTOR * seq // N_EXPERTS
    assert d == D_MODEL and seq == GRID_W * GRID_W and w_in.shape[2] == QKV_W + N_BRANCH * D_MODEL
    row = lambda v: v.reshape(1, -1)
    for l in range(depth):
        x2 = x.reshape(batch * seq, d)
        w_qkv = w_in[l, :, :QKV_W].astype(BF16)
        w_gate = w_in[l, :, QKV_W:].astype(BF16)
        qna, kna, vna, qg, kg, vg, qm = _in_proj(
            x2, w_qkv, row(jnp.tile(gqa_q_gain[l], GQA_HEADS)), row(jnp.tile(gqa_k_gain[l], GQA_KV_HEADS)),
            batch, seq)
        mk, mv = _mem_kv(mem.reshape(batch * mem_tokens, d), w_mem_kv[l].astype(BF16))
        yna = _na_attention(qna, kna, vna, _na_bias_tiles(na_rpb[l]), batch, seq)
        ygqa, wgu_bf, wd_bf = _gqa_attention(qg, kg, vg, w_gate_up[l], w_down[l], batch, seq)
        x1, aff = _merge(x2, yna, ygqa, qm, mk, mv, w_gate, row(b_gate[l]), w_branch[l].astype(BF16),
                              w_out[l].astype(BF16), row(ln1_g[l]), row(ln1_b[l]), w_router[l].T, seq,
                              mem_tokens, alpha)
        idx, gate = _ec_select(aff, batch, seq, cap)
        x = _ec_ffn(idx, gate, x1.reshape(batch, seq, d), wgu_bf, wd_bf, row(ln2_g[l]), row(ln2_b[l]),
                    batch, seq, cap, alpha)
    return x
```

```python
import functools
import math

import jax
import jax.numpy as jnp
import numpy as np
from jax import lax
from jax.experimental import pallas as pl
from jax.experimental.pallas import tpu as pltpu

F32 = jnp.float32
BF16 = jnp.bfloat16

D_MODEL = 1024
GRID_W = 64
HEAD_DIM = 64
LANES = 128
NA_HEADS = 8
NA_KH = 8
NA_KW = 16
NA_W = NA_HEADS * HEAD_DIM
NA_QROWS = 4
NA_KROWS = 12
GQA_HEADS = 8
GQA_KV_HEADS = 2
GQA_GROUP = GQA_HEADS // GQA_KV_HEADS
GQA_Q_W = GQA_HEADS * HEAD_DIM
GQA_KV_W = GQA_KV_HEADS * HEAD_DIM
ROPE_THETA = 10000.0
MEM_HEADS = 4
MEM_HEAD_DIM = 128
MEM_W = MEM_HEADS * MEM_HEAD_DIM
N_BRANCH = 3
N_EXPERTS = 16
EC_CAPACITY_FACTOR = 2
LN_EPS = 1e-5
RMS_EPS = 1e-6
NEG_INF = -1e30
QKV_W = 3 * NA_W + GQA_Q_W + 2 * GQA_KV_W + MEM_W

VMEM_LIMIT = 56 * 1024 * 1024
FFN_VMEM_LIMIT = 60 * 1024 * 1024


def _cparams(sem, vmem=VMEM_LIMIT):
    return pltpu.CompilerParams(dimension_semantics=sem, vmem_limit_bytes=vmem)


def _dot(a, b):
    return jnp.dot(a, b, preferred_element_type=F32)


def _dot_nt(a, b):
    return lax.dot_general(a, b, (((1,), (1,)), ((), ())), preferred_element_type=F32)


def _layer_norm(v, g, b):
    mu = jnp.mean(v, axis=-1, keepdims=True)
    var = jnp.mean(jnp.square(v - mu), axis=-1, keepdims=True)
    return (v - mu) * lax.rsqrt(var + LN_EPS) * g + b


def _na_bias_kernel(rpb_ref, out_ref):
    n = GRID_W * GRID_W
    col = lax.broadcasted_iota(jnp.int32, (32, n), 1)
    d = lax.broadcasted_iota(jnp.int32, (32, n), 0)
    c, kc = col // GRID_W, col % GRID_W
    onehot = (kc - c + (NA_KW - 1) == d).astype(F32)
    t = jnp.dot(rpb_ref[...], onehot, precision=lax.Precision.HIGHEST, preferred_element_type=F32)
    col1 = lax.broadcasted_iota(jnp.int32, (1, n), 1)
    c1, kc1 = col1 // GRID_W, col1 % GRID_W
    start = jnp.clip(c1 - NA_KW // 2, 0, GRID_W - NA_KW)
    in_win = (kc1 >= start) & (kc1 < start + NA_KW)
    out_ref[...] = jnp.where(in_win, t * math.log2(math.e), NEG_INF)


def _na_bias_tiles(rpb):
    h = rpb.shape[0]
    nd = 2 * NA_KH - 1
    rpb2 = jnp.pad(rpb.reshape(h * nd, 2 * NA_KW - 1), ((0, 0), (0, 1)))
    t = pl.pallas_call(
        _na_bias_kernel,
        out_shape=jax.ShapeDtypeStruct((h * nd, GRID_W * GRID_W), F32),
        name="na_bias",
    )(rpb2).reshape(h, nd, GRID_W, GRID_W)
    t = jnp.concatenate([t, jnp.full((h, 1, GRID_W, GRID_W), NEG_INF, F32)], axis=1)
    which = np.full((3, NA_QROWS, NA_KROWS), nd, np.int32)
    for variant in range(3):
        for i in range(NA_QROWS):
            for j in range(NA_KROWS):
                if variant == 0:
                    dr, ok = j - i, j < NA_KH
                elif variant == 1:
                    dr = j - i - NA_KH // 2
                    ok = -(NA_KH // 2) <= dr < NA_KH // 2
                else:
                    dr, ok = j - i - NA_KH, j >= NA_KROWS - NA_KH
                if ok:
                    which[variant, i, j] = dr + NA_KH - 1
    return t, which


def _fill_na_tiles(t_ref, bias_sc, which):
    for variant in range(which.shape[0]):
        for hh in range(2):
            for i in range(NA_QROWS):
                for j in range(0, NA_KROWS, 2):
                    pair = jnp.concatenate([t_ref[hh, int(which[variant, i, j])],
                                            t_ref[hh, int(which[variant, i, j + 1])]], axis=-1)
                    bias_sc[variant, hh, i * GRID_W:(i + 1) * GRID_W, j * GRID_W:(j + 2) * GRID_W] = pair


PROJ_TM = 1024


def _rope_tables(seq_len):
    t = jnp.arange(seq_len)
    row = (t // GRID_W).astype(F32)
    col = (t % GRID_W).astype(F32)
    half = HEAD_DIM // 2
    inv = ROPE_THETA ** (-jnp.arange(0, half, 2, dtype=F32) / half)
    ang = jnp.concatenate([row[:, None] * inv, col[:, None] * inv], axis=-1)
    cos, sin = jnp.cos(ang), jnp.sin(ang)
    cos_l = jnp.repeat(cos, 2, axis=-1)
    sin_l = jnp.stack([-sin, sin], axis=-1).reshape(seq_len, HEAD_DIM)
    return jnp.tile(cos_l, (1, 2)), jnp.tile(sin_l, (1, 2))


def _head_sums(sq, width):
    r = lax.broadcasted_iota(jnp.int32, (width, width), 0) // HEAD_DIM
    c = lax.broadcasted_iota(jnp.int32, (width, width), 1) // HEAD_DIM
    ones = (r == c).astype(BF16)
    hi = sq.astype(BF16)
    lo = (sq - hi.astype(F32)).astype(BF16)
    return _dot(hi, ones) + _dot(lo, ones)


def _rope(v, cos, sin_signed):
    lane = lax.broadcasted_iota(jnp.int32, v.shape, 1)
    swapped = jnp.where(lane % 2 == 0, pltpu.roll(v, LANES - 1, 1), pltpu.roll(v, 1, 1))
    return v * cos + swapped * sin_signed


def _proj_kernel(x_ref, w_ref, qgain_ref, kgain_ref, cos_ref, sin_ref,
                 qna_ref, kna_ref, vna_ref, qg_ref, kg_ref, vg_ref, qm_ref):
    xb = x_ref[...].astype(BF16)
    qscale = math.log2(math.e) / math.sqrt(HEAD_DIM)
    o = 0
    qna_ref[...] = (_dot(xb, w_ref[:, o:o + NA_W]) * qscale).astype(BF16)
    o += NA_W
    kna_ref[...] = _dot(xb, w_ref[:, o:o + NA_W]).astype(BF16)
    o += NA_W
    vna_ref[...] = _dot(xb, w_ref[:, o:o + NA_W]).astype(BF16)
    o += NA_W
    cos, sin = cos_ref[...], sin_ref[...]
    lane = lax.broadcasted_iota(jnp.int32, (PROJ_TM, LANES), 1)
    low = lane < HEAD_DIM

    zq = _dot(xb, w_ref[:, o:o + GQA_Q_W])
    o += GQA_Q_W
    ms = _head_sums(zq * zq, GQA_Q_W) * (1.0 / HEAD_DIM)
    zq = zq * lax.rsqrt(ms + RMS_EPS) * qgain_ref[...]
    for p in range(GQA_HEADS // 2):
        r = _rope(zq[:, p * LANES:(p + 1) * LANES], cos, sin) * qscale
        rolled = pltpu.roll(r, HEAD_DIM, 1)
        if (2 * p) // GQA_GROUP == 0:
            even, odd = jnp.where(low, r, 0.0), jnp.where(low, rolled, 0.0)
        else:
            even, odd = jnp.where(low, 0.0, rolled), jnp.where(low, 0.0, r)
        qg_ref[0, 2 * p] = even.astype(BF16)
        qg_ref[0, 2 * p + 1] = odd.astype(BF16)

    zk = _dot(xb, w_ref[:, o:o + GQA_KV_W])
    o += GQA_KV_W
    ms = _head_sums(zk * zk, GQA_KV_W) * (1.0 / HEAD_DIM)
    zk = zk * lax.rsqrt(ms + RMS_EPS) * kgain_ref[...]
    kg_ref[...] = _rope(zk, cos, sin).astype(BF16)
    zv = _dot(xb, w_ref[:, o:o + GQA_KV_W])
    vg_ref[0] = jnp.where(low, zv, 1.0).astype(BF16)
    vg_ref[1] = jnp.where(low, 1.0, zv).astype(BF16)
    o += GQA_KV_W
    qm_ref[...] = _dot(xb, w_ref[:, o:o + MEM_W]).astype(BF16)


def _in_proj(x2, w_qkv, qgain, kgain, batch, seq):
    tokens = x2.shape[0]
    tm = PROJ_TM
    spb = seq // tm
    cos, sin = _rope_tables(seq)
    row = lambda w: pl.BlockSpec((tm, w), lambda i: (i, 0))
    const = lambda shape: pl.BlockSpec(shape, lambda i: (0,) * len(shape), pipeline_mode=pl.Buffered(1))
    tbl = pl.BlockSpec((tm, LANES), lambda i: (i % spb, 0))
    sds = lambda w: jax.ShapeDtypeStruct((tokens, w), BF16)
    return pl.pallas_call(
        _proj_kernel,
        grid=(tokens // tm,),
        in_specs=[row(D_MODEL), const((D_MODEL, QKV_W)), const((1, GQA_Q_W)), const((1, GQA_KV_W)), tbl, tbl],
        out_specs=[row(NA_W), row(NA_W), row(NA_W),
                   pl.BlockSpec((1, GQA_HEADS, tm, LANES), lambda i: (i // spb, 0, i % spb, 0)),
                   row(GQA_KV_W), pl.BlockSpec((GQA_KV_HEADS, tm, LANES), lambda i: (0, i, 0)), row(MEM_W)],
        out_shape=[sds(NA_W), sds(NA_W), sds(NA_W),
                   jax.ShapeDtypeStruct((batch, GQA_HEADS, seq, LANES), BF16),
                   sds(GQA_KV_W), jax.ShapeDtypeStruct((GQA_KV_HEADS, tokens, LANES), BF16), sds(MEM_W)],
        compiler_params=_cparams(("parallel",)),
        name="in_proj",
    )(x2, w_qkv, qgain, kgain, cos, sin)


def _memkv_kernel(m_ref, w_ref, k_ref, v_ref):
    z = _dot(m_ref[...].astype(BF16), w_ref[...])
    k_ref[...] = z[:, :MEM_W].astype(BF16)
    v_ref[...] = z[:, MEM_W:].astype(BF16)


def _mem_kv(mem2, w_kv):
    rows = mem2.shape[0]
    tm = 256
    return pl.pallas_call(
        _memkv_kernel,
        grid=(rows // tm,),
        in_specs=[pl.BlockSpec((tm, D_MODEL), lambda i: (i, 0)),
                  pl.BlockSpec((D_MODEL, 2 * MEM_W), lambda i: (0, 0))],
        out_specs=[pl.BlockSpec((tm, MEM_W), lambda i: (i, 0))] * 2,
        out_shape=[jax.ShapeDtypeStruct((rows, MEM_W), BF16)] * 2,
        compiler_params=_cparams(("parallel",)),
        name="mem_kv",
    )(mem2, w_kv)


NA_TQ = NA_QROWS * GRID_W
NA_TK = NA_KROWS * GRID_W


NA_GROUPS_PER_STEP = 8


def _na_kernel(q_ref, k_ref, v_ref, t_ref, o_ref, bias_sc, *, n_groups, which):
    @pl.when((pl.program_id(1) == 0) & (pl.program_id(2) == 0))
    def _():
        _fill_na_tiles(t_ref, bias_sc, which)

    low = lax.broadcasted_iota(jnp.int32, (NA_TQ, LANES), 1) < HEAD_DIM
    low_k = lax.broadcasted_iota(jnp.int32, (NA_TK, LANES), 1) < HEAD_DIM
    tiles = NA_TK // LANES
    for sub in range(NA_GROUPS_PER_STEP):
        rg = pl.program_id(2) * NA_GROUPS_PER_STEP + sub
        key_row0 = jnp.clip(NA_QROWS * rg - NA_KH // 2, 0, GRID_W - NA_KROWS)
        start = pl.multiple_of(key_row0 * GRID_W, GRID_W)
        variant = jnp.where(rg == 0, 0, jnp.where(rg == n_groups - 1, 2, 1))
        k = k_ref[pl.ds(start, NA_TK), :]
        v = v_ref[pl.ds(start, NA_TK), :]
        q = q_ref[sub * NA_TQ:(sub + 1) * NA_TQ, :]
        outs = []
        for hh in range(2):
            mine, mine_k = (low, low_k) if hh == 0 else (jnp.logical_not(low), jnp.logical_not(low_k))
            qh = jnp.where(mine, q, jnp.zeros_like(q))
            vh = jnp.where(mine_k, v, jnp.ones_like(v))
            s = _dot_nt(qh, k) + bias_sc[variant, hh]
            st = [s[:, j * LANES:(j + 1) * LANES] for j in range(tiles)]
            part = st[0]
            for t in st[1:]:
                part = jnp.maximum(part, t)
            m = jnp.broadcast_to(jnp.max(part, axis=-1, keepdims=True), part.shape)
            p = jnp.concatenate([jnp.exp2(t - m).astype(BF16) for t in st], axis=-1)
            acc = _dot(p, vh)
            outs.append(acc / pltpu.roll(acc, HEAD_DIM, 1))
        o_ref[sub * NA_TQ:(sub + 1) * NA_TQ, :] = jnp.where(low, outs[0], outs[1]).astype(BF16)


def _na_attention(qna, kna, vna, tiles, which, batch, seq):
    n_groups = seq // NA_TQ
    steps = n_groups // NA_GROUPS_PER_STEP
    pairs = NA_HEADS // 2
    kv = pl.BlockSpec((seq, LANES), lambda p, b, g: (b, p))
    qo = pl.BlockSpec((NA_GROUPS_PER_STEP * NA_TQ, LANES), lambda p, b, g: (b * steps + g, p))
    return pl.pallas_call(
        functools.partial(_na_kernel, n_groups=n_groups, which=which),
        grid=(pairs, batch, steps),
        in_specs=[qo, kv, kv, pl.BlockSpec((2,) + tiles.shape[1:], lambda p, b, g: (p, 0, 0, 0))],
        out_specs=qo,
        out_shape=jax.ShapeDtypeStruct(qna.shape, BF16),
        scratch_shapes=[pltpu.VMEM((which.shape[0], 2, NA_TQ, NA_TK), F32)],
        compiler_params=_cparams(("arbitrary", "arbitrary", "arbitrary")),
        name="na_attn",
    )(qna, kna, vna, tiles)


GQA_TQ = 128
GQA_TK = 1024


def _gqa_kernel(q_ref, k_ref, v_ref, wgu_ref, wd_ref, o_ref, wgu_bf_ref, wd_bf_ref, s_sc, mx_sc, acc_sc, *, seq):
    wgu_bf_ref[...] = wgu_ref[...].astype(BF16)
    wd_bf_ref[...] = wd_ref[...].astype(BF16)
    rows = GQA_GROUP * GQA_TQ
    n_chunks = seq // GQA_TK
    tiles = GQA_TK // LANES
    groups = range(GQA_KV_HEADS)
    lane = lax.broadcasted_iota(jnp.int32, (GQA_TQ, LANES), 1)
    low = lane < HEAD_DIM
    qs = [q_ref[0, g * GQA_GROUP:(g + 1) * GQA_GROUP].reshape(rows, LANES) for g in groups]
    mx_sc[...] = jnp.full(mx_sc.shape, -jnp.inf, F32)

    for c in range(n_chunks):
        k = k_ref[c * GQA_TK:(c + 1) * GQA_TK, :]
        for g in groups:
            s = _dot_nt(qs[g], k)
            s_sc[g, c] = s
            part = s[:, :LANES]
            for j in range(1, tiles):
                part = jnp.maximum(part, s[:, j * LANES:(j + 1) * LANES])
            mx_sc[g] = jnp.maximum(mx_sc[g], part)
    ms = [jnp.broadcast_to(jnp.max(mx_sc[g], axis=-1, keepdims=True), (rows, LANES)) for g in groups]
    acc_sc[...] = jnp.zeros(acc_sc.shape, F32)

    for c in range(n_chunks):
        for g in groups:
            p = [jnp.exp2(s_sc[g, c, :, j * LANES:(j + 1) * LANES] - ms[g]).astype(BF16) for j in range(tiles)]
            acc_sc[g] += _dot(jnp.concatenate(p, axis=-1), v_ref[g, c * GQA_TK:(c + 1) * GQA_TK, :])
    for g in groups:
        acc = acc_sc[g]
        out = acc / pltpu.roll(acc, HEAD_DIM, 1)
        for pp in range(GQA_GROUP // 2):
            a = out[(2 * pp) * GQA_TQ:(2 * pp + 1) * GQA_TQ]
            b = out[(2 * pp + 1) * GQA_TQ:(2 * pp + 2) * GQA_TQ]
            if g == 0:
                pair = jnp.where(low, a, pltpu.roll(b, HEAD_DIM, 1))
            else:
                pair = jnp.where(low, pltpu.roll(a, HEAD_DIM, 1), b)
            col = (g * (GQA_GROUP // 2) + pp) * LANES
            o_ref[:, col:col + LANES] = pair.astype(BF16)


def _gqa_attention(qg, kg, vg, w_gate_up, w_down, batch, seq):
    nq = seq // GQA_TQ
    rows = GQA_GROUP * GQA_TQ
    stat = pltpu.VMEM((GQA_KV_HEADS, rows, LANES), F32)
    n_exp, w_rows = w_gate_up.shape[:2]
    slabs = batch * nq // n_exp
    slab = w_rows // slabs
    assert slabs * n_exp == batch * nq and slab * slabs == w_rows and w_down.shape[:2] == (n_exp, w_rows)
    wspec = lambda w: pl.BlockSpec((1, slab, w.shape[2]),
                                   lambda b, i: ((b * nq + i) // slabs, (b * nq + i) % slabs, 0))
    return pl.pallas_call(
        functools.partial(_gqa_kernel, seq=seq),
        grid=(batch, nq),
        in_specs=[pl.BlockSpec((1, GQA_HEADS, GQA_TQ, LANES), lambda b, i: (b, 0, i, 0)),
                  pl.BlockSpec((seq, LANES), lambda b, i: (b, 0)),
                  pl.BlockSpec((GQA_KV_HEADS, seq, LANES), lambda b, i: (0, b, 0)),
                  wspec(w_gate_up), wspec(w_down)],
        out_specs=[pl.BlockSpec((GQA_TQ, GQA_Q_W), lambda b, i: (b * nq + i, 0)), wspec(w_gate_up), wspec(w_down)],
        out_shape=[jax.ShapeDtypeStruct((batch * seq, GQA_Q_W), BF16),
                   jax.ShapeDtypeStruct(w_gate_up.shape, BF16), jax.ShapeDtypeStruct(w_down.shape, BF16)],
        scratch_shapes=[pltpu.VMEM((GQA_KV_HEADS, seq // GQA_TK, rows, GQA_TK), F32), stat, stat],
        compiler_params=_cparams(("parallel", "parallel")),
        name="gqa_attn",
    )(qg, kg, vg, w_gate_up, w_down)


MERGE_TM = 1024
MERGE_HALVES = 2


def _merge_kernel(x_ref, yna_ref, ygqa_ref, qm_ref, mk_ref, mv_ref, wg_ref, bg_ref, wb_ref, wo_ref,
                  g1_ref, b1_ref, wr_ref, x1_ref, aff_ref, *, alpha):
    rows_per = MERGE_TM // MERGE_HALVES
    mem_scale = 1.0 / math.sqrt(MEM_HEAD_DIM)
    w = wr_ref[...]
    w_hi = w.astype(BF16)
    w_hl = jnp.concatenate([w_hi, (w - w_hi.astype(F32)).astype(BF16)], axis=0)
    for part in range(MERGE_HALVES):
        rows = slice(part * rows_per, (part + 1) * rows_per)
        x = x_ref[rows, :]
        xb = x.astype(BF16)
        ymem = []
        for h in range(MEM_HEADS):
            sl = slice(h * MEM_HEAD_DIM, (h + 1) * MEM_HEAD_DIM)
            s = _dot_nt(qm_ref[rows, sl], mk_ref[:, sl]) * mem_scale
            m = jnp.max(s, axis=-1, keepdims=True)
            p = jnp.exp(s - m)
            l = jnp.sum(p, axis=-1, keepdims=True)
            ymem.append((_dot(p.astype(BF16), mv_ref[:, sl]) / l).astype(BF16))
        branches = (yna_ref[rows, :], ygqa_ref[rows, :], jnp.concatenate(ymem, axis=-1))
        merged = jnp.zeros((rows_per, D_MODEL), F32)
        for g in range(N_BRANCH):
            cols = slice(g * D_MODEL, (g + 1) * D_MODEL)
            gate = jax.nn.sigmoid(_dot(xb, wg_ref[:, cols]) + bg_ref[:, cols])
            merged = merged + gate * _dot(branches[g], wb_ref[g])
        mix = _dot(merged.astype(BF16), wo_ref[...])
        x1 = _layer_norm(alpha * x + mix, g1_ref[...], b1_ref[...])
        x1_ref[rows, :] = x1
        x1_hi = x1.astype(BF16)
        x1_lo = (x1 - x1_hi.astype(F32)).astype(BF16)
        by_hi = _dot_nt(w_hl, x1_hi)
        logits = by_hi[:N_EXPERTS] + by_hi[N_EXPERTS:] + _dot_nt(w_hi, x1_lo)
        m = jnp.max(logits, axis=0, keepdims=True)
        e = jnp.exp(logits - m)
        aff_ref[0, :, rows] = e / jnp.sum(e, axis=0, keepdims=True)


def _merge(x2, yna, ygqa, qm, mk, mv, w_gate, b_gate, w_branch, w_out, g1, b1, w_router, seq, mem_tokens, alpha):
    tokens = x2.shape[0]
    tm = MERGE_TM
    tpb = seq // tm
    row = lambda w: pl.BlockSpec((tm, w), lambda i: (i, 0))
    const = lambda shape: pl.BlockSpec(shape, lambda i: (0,) * len(shape), pipeline_mode=pl.Buffered(1))
    memb = pl.BlockSpec((mem_tokens, MEM_W), lambda i: (i // tpb, 0))
    return pl.pallas_call(
        functools.partial(_merge_kernel, alpha=alpha),
        grid=(tokens // tm,),
        in_specs=[row(D_MODEL), row(NA_W), row(GQA_Q_W), row(MEM_W), memb, memb,
                  const(w_gate.shape), const(b_gate.shape), const(w_branch.shape), const(w_out.shape),
                  const(g1.shape), const(b1.shape), const(w_router.shape)],
        out_specs=[row(D_MODEL), pl.BlockSpec((1, N_EXPERTS, tm), lambda i: (i // tpb, 0, i % tpb))],
        out_shape=[jax.ShapeDtypeStruct((tokens, D_MODEL), F32),
                   jax.ShapeDtypeStruct((tokens // seq, N_EXPERTS, seq), F32)],
        compiler_params=_cparams(("parallel",)),
        name="merge",
    )(x2, yna, ygqa, qm, mk, mv, w_gate, b_gate, w_branch, w_out, g1, b1, w_router)


SEL_BLK = 256
SLOT_HI = 16
SLOT_LO = 32
SEL_PARTS = 5


def _prefix_counts(mask, out_ref, seq):
    r = lax.broadcasted_iota(jnp.int32, (SEL_BLK, SEL_BLK), 0)
    c = lax.broadcasted_iota(jnp.int32, (SEL_BLK, SEL_BLK), 1)
    tri = (r <= c).astype(BF16)
    run = jnp.zeros((mask.shape[0], 1), F32)
    for j in range(seq // SEL_BLK):
        cs = _dot(mask[:, j * SEL_BLK:(j + 1) * SEL_BLK].astype(BF16), tri) + run
        out_ref[:, j * SEL_BLK:(j + 1) * SEL_BLK] = cs
        run = cs[:, SEL_BLK - 1:SEL_BLK]


def _select_kernel(aff_ref, idx_ref, gate_ref, cnt_sc, lhs_sc, rhs_sc, *, seq, cap):
    a = aff_ref[0]
    thr_bits = jnp.zeros((N_EXPERTS, 1), jnp.int32)
    for bit in range(30, -1, -1):
        cand = thr_bits | (1 << bit)
        cnt = jnp.sum((a >= pltpu.bitcast(cand, F32)).astype(F32), axis=-1, keepdims=True)
        thr_bits = jnp.where(cnt >= cap, cand, thr_bits)
    thr = pltpu.bitcast(thr_bits, F32)
    gt = a > thr
    eq = a == thr
    need = cap - jnp.sum(gt.astype(F32), axis=-1, keepdims=True)
    _prefix_counts(eq.astype(F32), cnt_sc, seq)
    sel = gt | (eq & (cnt_sc[...] <= need))
    _prefix_counts(sel.astype(F32), cnt_sc, seq)

    slot = cnt_sc[...] - 1.0
    s_hi = jnp.floor(slot * (1.0 / SLOT_LO))
    s_lo = slot - SLOT_LO * s_hi
    tok = lax.broadcasted_iota(jnp.int32, (1, seq), 1).astype(F32)
    t_hi = jnp.floor(tok * (1.0 / 64.0))
    t_lo = tok - 64.0 * t_hi
    ph = lax.broadcasted_iota(jnp.int32, (SLOT_HI, seq), 0).astype(F32)
    pl_ = lax.broadcasted_iota(jnp.int32, (SLOT_LO, seq), 0).astype(F32)
    for e in range(N_EXPERTS):
        row = slice(e, e + 1)
        onehot_hi = jnp.where((s_hi[row] == ph) & sel[row], 1.0, 0.0)
        g = a[row]
        g1 = g.astype(BF16).astype(F32)
        g2 = (g - g1).astype(BF16).astype(F32)
        g3 = g - g1 - g2
        for part, val in enumerate((t_hi, t_lo, g1, g2, g3)):
            r0 = (part * N_EXPERTS + e) * SLOT_HI
            lhs_sc[r0:r0 + SLOT_HI, :] = (onehot_hi * val).astype(BF16)
        rhs_sc[e * SLOT_LO:(e + 1) * SLOT_LO, :] = jnp.where(s_lo[row] == pl_, 1.0, 0.0).astype(BF16)
    res = _dot_nt(lhs_sc[...], rhs_sc[...])
    n = N_EXPERTS * SLOT_HI
    r = lax.broadcasted_iota(jnp.int32, (n, N_EXPERTS * SLOT_LO), 0) // SLOT_HI
    c = lax.broadcasted_iota(jnp.int32, (n, N_EXPERTS * SLOT_LO), 1) // SLOT_LO
    parts = []
    for part in range(SEL_PARTS):
        blk = jnp.where(r == c, res[part * n:(part + 1) * n], 0.0)
        parts.append(jnp.sum(blk.reshape(N_EXPERTS, SLOT_HI, N_EXPERTS * SLOT_LO), axis=0))
    idx_ref[0] = (64.0 * parts[0] + parts[1]).astype(jnp.int32)
    gate_ref[0] = (parts[2] + parts[3]) + parts[4]


def _ec_select(aff, batch, seq, cap):
    assert cap == SLOT_HI * SLOT_LO
    out = pl.BlockSpec((1, SLOT_HI, N_EXPERTS * SLOT_LO), lambda b: (b, 0, 0))
    idx_t, gate_t = pl.pallas_call(
        functools.partial(_select_kernel, seq=seq, cap=cap),
        grid=(batch,),
        in_specs=[pl.BlockSpec((1, N_EXPERTS, seq), lambda b: (b, 0, 0))],
        out_specs=[out, out],
        out_shape=[jax.ShapeDtypeStruct((batch, SLOT_HI, N_EXPERTS * SLOT_LO), jnp.int32),
                   jax.ShapeDtypeStruct((batch, SLOT_HI, N_EXPERTS * SLOT_LO), F32)],
        scratch_shapes=[pltpu.VMEM((N_EXPERTS, seq), F32),
                        pltpu.VMEM((SEL_PARTS * N_EXPERTS * SLOT_HI, seq), BF16),
                        pltpu.VMEM((N_EXPERTS * SLOT_LO, seq), BF16)],
        compiler_params=_cparams(("parallel",)),
        name="ec_select",
    )(aff)
    order = lambda v: v.reshape(batch, SLOT_HI, N_EXPERTS, SLOT_LO).transpose(0, 2, 1, 3).reshape(batch, N_EXPERTS, cap)
    return order(idx_t), order(gate_t)


FFN_COL_PHASES = 1
LN2_TM = 256


def _gather_rows(idx_ref, x_ref, dst, lo, hi):
    for p in range(lo, hi):
        dst[pl.ds(p, 1), :] = x_ref[0, pl.ds(idx_ref[0, 0, p], 1), :]


def _scatter_rows(idx_ref, gate_ref, src, acc_sc, lo, hi):
    for p0 in range(lo, hi, 8):
        ps = range(p0, p0 + 8)
        ts = [idx_ref[0, 0, p] for p in ps]
        rows = [acc_sc[pl.ds(t, 1), :] + src[pl.ds(p, 1), :] * gate_ref[0, 0, p] for p, t in zip(ps, ts)]
        for t, row in zip(ts, rows):
            acc_sc[pl.ds(t, 1), :] = row


def _ffn_kernel(idx_ref, idx_next_ref, idx_prev_ref, gate_ref, gate_prev_ref, x_ref, wgu_ref, wd_ref,
                g2_ref, b2_ref, out_hbm, u_sc, xs_sc, act_sc, ye_sc, acc_sc, out_sc, sem_out,
                *, cap, d_ff, seq, alpha):
    b = pl.program_id(0)
    e = pl.program_id(1)

    @pl.when(e == 0)
    def _():
        acc_sc[...] = jnp.zeros(acc_sc.shape, F32)
        _gather_rows(idx_ref, x_ref, u_sc, 0, cap)

    @pl.when((e == 0) & (b == 0))
    def _():
        ye_sc[...] = jnp.zeros(ye_sc.shape, F32)

    xs_sc[...] = u_sc[...].astype(BF16)

    width = d_ff // FFN_COL_PHASES
    out_w = D_MODEL // FFN_COL_PHASES
    share = cap // FFN_COL_PHASES
    for ph in range(2 * FFN_COL_PHASES):
        @pl.when(e + ph >= 0)
        def _(ph=ph):
            if ph < FFN_COL_PHASES:
                _scatter_rows(idx_prev_ref, gate_prev_ref, ye_sc, acc_sc, ph * share, (ph + 1) * share)
                xs = xs_sc[...]
                hg = _dot(xs, wgu_ref[0, :, ph * width:(ph + 1) * width])
                hu = _dot(xs, wgu_ref[0, :, d_ff + ph * width:d_ff + (ph + 1) * width])
                act_sc[:, ph * width:(ph + 1) * width] = (hg * jax.nn.sigmoid(hg) * hu).astype(BF16)
            else:
                c = ph - FFN_COL_PHASES
                _gather_rows(idx_next_ref, x_ref, u_sc, c * share, (c + 1) * share)
                ye_sc[:, c * out_w:(c + 1) * out_w] = _dot(act_sc[...], wd_ref[0, :, c * out_w:(c + 1) * out_w])

    @pl.when(e == N_EXPERTS - 1)
    def _():
        def o_copy(i, slot):
            return pltpu.make_async_copy(out_sc.at[slot], out_hbm.at[b, pl.ds(i * LN2_TM, LN2_TM)], sem_out.at[slot])

        _scatter_rows(idx_ref, gate_ref, ye_sc, acc_sc, 0, cap)
        n_tiles = seq // LN2_TM
        for i in range(n_tiles):
            slot = i % 2
            rows = slice(i * LN2_TM, (i + 1) * LN2_TM)
            if i >= 2:
                o_copy(i - 2, slot).wait()
            out_sc[slot] = _layer_norm(alpha * x_ref[0, rows, :] + acc_sc[rows, :], g2_ref[...], b2_ref[...])
            o_copy(i, slot).start()
        o_copy(n_tiles - 2, n_tiles % 2).wait()
        o_copy(n_tiles - 1, (n_tiles - 1) % 2).wait()


def _ec_ffn(idx, gate, x1, wgu, wd, g2, b2, batch, seq, cap, alpha):
    d_ff = wd.shape[1]
    slot = pl.BlockSpec((1, 1, cap), lambda b, e: (b * N_EXPERTS + e, 0, 0), memory_space=pltpu.SMEM)
    vec = pl.BlockSpec((1, D_MODEL), lambda b, e: (0, 0))
    flat = lambda v: v.reshape(batch * N_EXPERTS, 1, cap)
    idx_next = jnp.concatenate([idx[:, 1:], idx[:, -1:]], axis=1)
    idx_prev = jnp.concatenate([idx[:, :1], idx[:, :-1]], axis=1)
    gate_prev = jnp.concatenate([jnp.zeros_like(gate[:, :1]), gate[:, :-1]], axis=1)
    return pl.pallas_call(
        functools.partial(_ffn_kernel, cap=cap, d_ff=d_ff, seq=seq, alpha=alpha),
        grid=(batch, N_EXPERTS),
        in_specs=[slot, slot, slot, slot, slot,
                  pl.BlockSpec((1, seq, D_MODEL), lambda b, e: (b, 0, 0), pipeline_mode=pl.Buffered(1)),
                  pl.BlockSpec((1, D_MODEL, 2 * d_ff), lambda b, e: (e, 0, 0)),
                  pl.BlockSpec((1, d_ff, D_MODEL), lambda b, e: (e, 0, 0)),
                  vec, vec],
        out_specs=pl.BlockSpec(memory_space=pl.ANY),
        out_shape=jax.ShapeDtypeStruct((batch, seq, D_MODEL), F32),
        scratch_shapes=[pltpu.VMEM((cap, D_MODEL), F32), pltpu.VMEM((cap, D_MODEL), BF16),
                        pltpu.VMEM((cap, d_ff), BF16), pltpu.VMEM((cap, D_MODEL), F32),
                        pltpu.VMEM((seq, D_MODEL), F32), pltpu.VMEM((2, LN2_TM, D_MODEL), F32),
                        pltpu.SemaphoreType.DMA((2,))],
        compiler_params=_cparams(("arbitrary", "arbitrary"), FFN_VMEM_LIMIT),
        name="ec_ffn",
    )(flat(idx), flat(idx_next), flat(idx_prev), flat(gate), flat(gate_prev), x1, wgu, wd, g2, b2)


def kernel(x, mem, w_in, b_gate, na_rpb, gqa_q_gain, gqa_k_gain, w_mem_kv, w_branch, w_out,
           ln1_g, ln1_b, w_router, w_gate_up, w_down, ln2_g, ln2_b):
    batch, seq, d = x.shape
    mem_tokens = mem.shape[1]
    depth = w_in.shape[0]
    alpha = (2 * depth) ** 0.25
    cap = EC_CAPACITY_FACTOR * seq // N_EXPERTS
    assert d == D_MODEL and seq == GRID_W * GRID_W and w_in.shape[2] == QKV_W + N_BRANCH * D_MODEL
    row = lambda v: v.reshape(1, -1)
    for l in range(depth):
        x2 = x.reshape(batch * seq, d)
        w_qkv = w_in[l, :, :QKV_W].astype(BF16)
        w_gate = w_in[l, :, QKV_W:].astype(BF16)
        qna, kna, vna, qg, kg, vg, qm = _in_proj(
            x2, w_qkv, row(jnp.tile(gqa_q_gain[l], GQA_HEADS)), row(jnp.tile(gqa_k_gain[l], GQA_KV_HEADS)),
            batch, seq)
        mk, mv = _mem_kv(mem.reshape(batch * mem_tokens, d), w_mem_kv[l].astype(BF16))
        yna = _na_attention(qna, kna, vna, *_na_bias_tiles(na_rpb[l]), batch, seq)
        ygqa, wgu_bf, wd_bf = _gqa_attention(qg, kg, vg, w_gate_up[l], w_down[l], batch, seq)
        x1, aff = _merge(x2, yna, ygqa, qm, mk, mv, w_gate, row(b_gate[l]), w_branch[l].astype(BF16),
                              w_out[l].astype(BF16), row(ln1_g[l]), row(ln1_b[l]), w_router[l].T, seq,
                              mem_tokens, alpha)
        idx, gate = _ec_select(aff, batch, seq, cap)
        x = _ec_ffn(idx, gate, x1.reshape(batch, seq, d), wgu_bf, wd_bf, row(ln2_g[l]), row(ln2_b[l]),
                    batch, seq, cap, alpha)
    return x
```

```python
import functools
import math

import jax
import jax.numpy as jnp
import numpy as np
from jax import lax
from jax.experimental import pallas as pl
from jax.experimental.pallas import tpu as pltpu

F32 = jnp.float32
BF16 = jnp.bfloat16

D_MODEL = 1024
GRID_W = 64
HEAD_DIM = 64
LANES = 128
NA_HEADS = 8
NA_KH = 8
NA_KW = 16
NA_W = NA_HEADS * HEAD_DIM
NA_QROWS = 4
NA_KROWS = 12
GQA_HEADS = 8
GQA_KV_HEADS = 2
GQA_GROUP = GQA_HEADS // GQA_KV_HEADS
GQA_Q_W = GQA_HEADS * HEAD_DIM
GQA_KV_W = GQA_KV_HEADS * HEAD_DIM
ROPE_THETA = 10000.0
MEM_HEADS = 4
MEM_HEAD_DIM = 128
MEM_W = MEM_HEADS * MEM_HEAD_DIM
N_BRANCH = 3
N_EXPERTS = 16
EC_CAPACITY_FACTOR = 2
LN_EPS = 1e-5
RMS_EPS = 1e-6
NEG_INF = -1e30
QKV_W = 3 * NA_W + GQA_Q_W + 2 * GQA_KV_W + MEM_W

VMEM_LIMIT = 56 * 1024 * 1024
FFN_VMEM_LIMIT = 60 * 1024 * 1024


def _cparams(sem, vmem=VMEM_LIMIT):
    return pltpu.CompilerParams(dimension_semantics=sem, vmem_limit_bytes=vmem)


def _dot(a, b):
    return jnp.dot(a, b, preferred_element_type=F32)


def _dot_nt(a, b):
    return lax.dot_general(a, b, (((1,), (1,)), ((), ())), preferred_element_type=F32)


def _layer_norm(v, g, b):
    mu = jnp.mean(v, axis=-1, keepdims=True)
    var = jnp.mean(jnp.square(v - mu), axis=-1, keepdims=True)
    return (v - mu) * lax.rsqrt(var + LN_EPS) * g + b


def _na_bias_kernel(rpb_ref, out_ref):
    n = GRID_W * GRID_W
    col = lax.broadcasted_iota(jnp.int32, (32, n), 1)
    d = lax.broadcasted_iota(jnp.int32, (32, n), 0)
    c, kc = col // GRID_W, col % GRID_W
    onehot = (kc - c + (NA_KW - 1) == d).astype(F32)
    t = jnp.dot(rpb_ref[...], onehot, precision=lax.Precision.HIGHEST, preferred_element_type=F32)
    col1 = lax.broadcasted_iota(jnp.int32, (1, n), 1)
    c1, kc1 = col1 // GRID_W, col1 % GRID_W
    start = jnp.clip(c1 - NA_KW // 2, 0, GRID_W - NA_KW)
    in_win = (kc1 >= start) & (kc1 < start + NA_KW)
    out_ref[...] = jnp.where(in_win, t * math.log2(math.e), NEG_INF)


def _na_bias_tiles(rpb):
    h = rpb.shape[0]
    nd = 2 * NA_KH - 1
    rpb2 = jnp.pad(rpb.reshape(h * nd, 2 * NA_KW - 1), ((0, 0), (0, 1)))
    t = pl.pallas_call(
        _na_bias_kernel,
        out_shape=jax.ShapeDtypeStruct((h * nd, GRID_W * GRID_W), F32),
        name="na_bias",
    )(rpb2).reshape(h, nd, GRID_W, GRID_W)
    t = jnp.concatenate([t, jnp.full((h, 1, GRID_W, GRID_W), NEG_INF, F32)], axis=1)
    which = np.full((3, NA_QROWS, NA_KROWS), nd, np.int32)
    for variant in range(3):
        for i in range(NA_QROWS):
            for j in range(NA_KROWS):
                if variant == 0:
                    dr, ok = j - i, j < NA_KH
                elif variant == 1:
                    dr = j - i - NA_KH // 2
                    ok = -(NA_KH // 2) <= dr < NA_KH // 2
                else:
                    dr, ok = j - i - NA_KH, j >= NA_KROWS - NA_KH
                if ok:
                    which[variant, i, j] = dr + NA_KH - 1
    return t, which


def _fill_na_tiles(t_ref, bias_sc, which):
    for variant in range(which.shape[0]):
        for hh in range(2):
            for i in range(NA_QROWS):
                for j in range(0, NA_KROWS, 2):
                    pair = jnp.concatenate([t_ref[hh, int(which[variant, i, j])],
                                            t_ref[hh, int(which[variant, i, j + 1])]], axis=-1)
                    bias_sc[variant, hh, i * GRID_W:(i + 1) * GRID_W, j * GRID_W:(j + 2) * GRID_W] = pair


PROJ_TM = 1024


def _rope_tables(seq_len):
    t = jnp.arange(seq_len)
    row = (t // GRID_W).astype(F32)
    col = (t % GRID_W).astype(F32)
    half = HEAD_DIM // 2
    inv = ROPE_THETA ** (-jnp.arange(0, half, 2, dtype=F32) / half)
    ang = jnp.concatenate([row[:, None] * inv, col[:, None] * inv], axis=-1)
    cos, sin = jnp.cos(ang), jnp.sin(ang)
    cos_l = jnp.repeat(cos, 2, axis=-1)
    sin_l = jnp.stack([-sin, sin], axis=-1).reshape(seq_len, HEAD_DIM)
    return jnp.tile(cos_l, (1, 2)), jnp.tile(sin_l, (1, 2))


def _head_sums(sq, width):
    r = lax.broadcasted_iota(jnp.int32, (width, width), 0) // HEAD_DIM
    c = lax.broadcasted_iota(jnp.int32, (width, width), 1) // HEAD_DIM
    return _dot(sq.astype(BF16), (r == c).astype(BF16))


def _rope(v, cos, sin_signed):
    lane = lax.broadcasted_iota(jnp.int32, v.shape, 1)
    swapped = jnp.where(lane % 2 == 0, pltpu.roll(v, LANES - 1, 1), pltpu.roll(v, 1, 1))
    return v * cos + swapped * sin_signed


def _proj_kernel(x_ref, w_ref, qgain_ref, kgain_ref, cos_ref, sin_ref,
                 qna_ref, kna_ref, vna_ref, qg_ref, kg_ref, vg_ref, qm_ref):
    xb = x_ref[...].astype(BF16)
    qscale = math.log2(math.e) / math.sqrt(HEAD_DIM)
    o = 0
    qna_ref[...] = (_dot(xb, w_ref[:, o:o + NA_W]) * qscale).astype(BF16)
    o += NA_W
    kna_ref[...] = _dot(xb, w_ref[:, o:o + NA_W]).astype(BF16)
    o += NA_W
    vna_ref[...] = _dot(xb, w_ref[:, o:o + NA_W]).astype(BF16)
    o += NA_W
    cos, sin = cos_ref[...], sin_ref[...]
    lane = lax.broadcasted_iota(jnp.int32, (PROJ_TM, LANES), 1)
    low = lane < HEAD_DIM

    zq = _dot(xb, w_ref[:, o:o + GQA_Q_W])
    o += GQA_Q_W
    ms = _head_sums(zq * zq, GQA_Q_W) * (1.0 / HEAD_DIM)
    zq = zq * lax.rsqrt(ms + RMS_EPS) * qgain_ref[...]
    for p in range(GQA_HEADS // 2):
        r = _rope(zq[:, p * LANES:(p + 1) * LANES], cos, sin) * qscale
        rolled = pltpu.roll(r, HEAD_DIM, 1)
        if (2 * p) // GQA_GROUP == 0:
            even, odd = jnp.where(low, r, 0.0), jnp.where(low, rolled, 0.0)
        else:
            even, odd = jnp.where(low, 0.0, rolled), jnp.where(low, 0.0, r)
        qg_ref[0, 2 * p] = even.astype(BF16)
        qg_ref[0, 2 * p + 1] = odd.astype(BF16)

    zk = _dot(xb, w_ref[:, o:o + GQA_KV_W])
    o += GQA_KV_W
    ms = _head_sums(zk * zk, GQA_KV_W) * (1.0 / HEAD_DIM)
    zk = zk * lax.rsqrt(ms + RMS_EPS) * kgain_ref[...]
    kg_ref[...] = _rope(zk, cos, sin).astype(BF16)
    zv = _dot(xb, w_ref[:, o:o + GQA_KV_W])
    vg_ref[0] = jnp.where(low, zv, 1.0).astype(BF16)
    vg_ref[1] = jnp.where(low, 1.0, zv).astype(BF16)
    o += GQA_KV_W
    qm_ref[...] = _dot(xb, w_ref[:, o:o + MEM_W]).astype(BF16)


def _in_proj(x2, w_qkv, qgain, kgain, batch, seq):
    tokens = x2.shape[0]
    tm = PROJ_TM
    spb = seq // tm
    cos, sin = _rope_tables(seq)
    row = lambda w: pl.BlockSpec((tm, w), lambda i: (i, 0))
    const = lambda shape: pl.BlockSpec(shape, lambda i: (0,) * len(shape), pipeline_mode=pl.Buffered(1))
    tbl = pl.BlockSpec((tm, LANES), lambda i: (i % spb, 0))
    sds = lambda w: jax.ShapeDtypeStruct((tokens, w), BF16)
    return pl.pallas_call(
        _proj_kernel,
        grid=(tokens // tm,),
        in_specs=[row(D_MODEL), const((D_MODEL, QKV_W)), const((1, GQA_Q_W)), const((1, GQA_KV_W)), tbl, tbl],
        out_specs=[row(NA_W), row(NA_W), row(NA_W),
                   pl.BlockSpec((1, GQA_HEADS, tm, LANES), lambda i: (i // spb, 0, i % spb, 0)),
                   row(GQA_KV_W), pl.BlockSpec((GQA_KV_HEADS, tm, LANES), lambda i: (0, i, 0)), row(MEM_W)],
        out_shape=[sds(NA_W), sds(NA_W), sds(NA_W),
                   jax.ShapeDtypeStruct((batch, GQA_HEADS, seq, LANES), BF16),
                   sds(GQA_KV_W), jax.ShapeDtypeStruct((GQA_KV_HEADS, tokens, LANES), BF16), sds(MEM_W)],
        compiler_params=_cparams(("parallel",)),
        name="in_proj",
    )(x2, w_qkv, qgain, kgain, cos, sin)


def _memkv_kernel(m_ref, w_ref, k_ref, v_ref):
    z = _dot(m_ref[...].astype(BF16), w_ref[...])
    k_ref[...] = z[:, :MEM_W].astype(BF16)
    v_ref[...] = z[:, MEM_W:].astype(BF16)


def _mem_kv(mem2, w_kv):
    rows = mem2.shape[0]
    tm = 256
    return pl.pallas_call(
        _memkv_kernel,
        grid=(rows // tm,),
        in_specs=[pl.BlockSpec((tm, D_MODEL), lambda i: (i, 0)),
                  pl.BlockSpec((D_MODEL, 2 * MEM_W), lambda i: (0, 0))],
        out_specs=[pl.BlockSpec((tm, MEM_W), lambda i: (i, 0))] * 2,
        out_shape=[jax.ShapeDtypeStruct((rows, MEM_W), BF16)] * 2,
        compiler_params=_cparams(("parallel",)),
        name="mem_kv",
    )(mem2, w_kv)


NA_TQ = NA_QROWS * GRID_W
NA_TK = NA_KROWS * GRID_W


NA_GROUPS_PER_STEP = 8


def _na_kernel(q_ref, k_ref, v_ref, t_ref, o_ref, bias_sc, *, n_groups, which):
    @pl.when((pl.program_id(1) == 0) & (pl.program_id(2) == 0))
    def _():
        _fill_na_tiles(t_ref, bias_sc, which)

    low = lax.broadcasted_iota(jnp.int32, (NA_TQ, LANES), 1) < HEAD_DIM
    low_k = lax.broadcasted_iota(jnp.int32, (NA_TK, LANES), 1) < HEAD_DIM
    tiles = NA_TK // LANES
    for sub in range(NA_GROUPS_PER_STEP):
        rg = pl.program_id(2) * NA_GROUPS_PER_STEP + sub
        key_row0 = jnp.clip(NA_QROWS * rg - NA_KH // 2, 0, GRID_W - NA_KROWS)
        start = pl.multiple_of(key_row0 * GRID_W, GRID_W)
        variant = jnp.where(rg == 0, 0, jnp.where(rg == n_groups - 1, 2, 1))
        k = k_ref[pl.ds(start, NA_TK), :]
        v = v_ref[pl.ds(start, NA_TK), :]
        q = q_ref[sub * NA_TQ:(sub + 1) * NA_TQ, :]
        outs = []
        for hh in range(2):
            mine, mine_k = (low, low_k) if hh == 0 else (jnp.logical_not(low), jnp.logical_not(low_k))
            qh = jnp.where(mine, q, jnp.zeros_like(q))
            vh = jnp.where(mine_k, v, jnp.ones_like(v))
            s = _dot_nt(qh, k) + bias_sc[variant, hh]
            st = [s[:, j * LANES:(j + 1) * LANES] for j in range(tiles)]
            part = st[0]
            for t in st[1:]:
                part = jnp.maximum(part, t)
            m = jnp.broadcast_to(jnp.max(part, axis=-1, keepdims=True), part.shape)
            p = jnp.concatenate([jnp.exp2(t - m).astype(BF16) for t in st], axis=-1)
            acc = _dot(p, vh)
            outs.append(acc / pltpu.roll(acc, HEAD_DIM, 1))
        o_ref[sub * NA_TQ:(sub + 1) * NA_TQ, :] = jnp.where(low, outs[0], outs[1]).astype(BF16)


def _na_attention(qna, kna, vna, tiles, which, batch, seq):
    n_groups = seq // NA_TQ
    steps = n_groups // NA_GROUPS_PER_STEP
    pairs = NA_HEADS // 2
    kv = pl.BlockSpec((seq, LANES), lambda p, b, g: (b, p))
    qo = pl.BlockSpec((NA_GROUPS_PER_STEP * NA_TQ, LANES), lambda p, b, g: (b * steps + g, p))
    return pl.pallas_call(
        functools.partial(_na_kernel, n_groups=n_groups, which=which),
        grid=(pairs, batch, steps),
        in_specs=[qo, kv, kv, pl.BlockSpec((2,) + tiles.shape[1:], lambda p, b, g: (p, 0, 0, 0))],
        out_specs=qo,
        out_shape=jax.ShapeDtypeStruct(qna.shape, BF16),
        scratch_shapes=[pltpu.VMEM((which.shape[0], 2, NA_TQ, NA_TK), F32)],
        compiler_params=_cparams(("arbitrary", "arbitrary", "arbitrary")),
        name="na_attn",
    )(qna, kna, vna, tiles)


GQA_TQ = 128
GQA_TK = 1024


def _gqa_kernel(q_ref, k_ref, v_ref, wgu_ref, wd_ref, o_ref, wgu_bf_ref, wd_bf_ref, s_sc, mx_sc, acc_sc, *, seq):
    wgu_bf_ref[...] = wgu_ref[...].astype(BF16)
    wd_bf_ref[...] = wd_ref[...].astype(BF16)
    rows = GQA_GROUP * GQA_TQ
    n_chunks = seq // GQA_TK
    tiles = GQA_TK // LANES
    groups = range(GQA_KV_HEADS)
    lane = lax.broadcasted_iota(jnp.int32, (GQA_TQ, LANES), 1)
    low = lane < HEAD_DIM
    qs = [q_ref[0, g * GQA_GROUP:(g + 1) * GQA_GROUP].reshape(rows, LANES) for g in groups]
    mx_sc[...] = jnp.full(mx_sc.shape, -jnp.inf, F32)

    for c in range(n_chunks):
        k = k_ref[c * GQA_TK:(c + 1) * GQA_TK, :]
        for g in groups:
            s = _dot_nt(qs[g], k)
            s_sc[g, c] = s
            part = s[:, :LANES]
            for j in range(1, tiles):
                part = jnp.maximum(part, s[:, j * LANES:(j + 1) * LANES])
            mx_sc[g] = jnp.maximum(mx_sc[g], part)
    ms = [jnp.broadcast_to(jnp.max(mx_sc[g], axis=-1, keepdims=True), (rows, LANES)) for g in groups]
    acc_sc[...] = jnp.zeros(acc_sc.shape, F32)

    for c in range(n_chunks):
        for g in groups:
            p = [jnp.exp2(s_sc[g, c, :, j * LANES:(j + 1) * LANES] - ms[g]).astype(BF16) for j in range(tiles)]
            acc_sc[g] += _dot(jnp.concatenate(p, axis=-1), v_ref[g, c * GQA_TK:(c + 1) * GQA_TK, :])
    for g in groups:
        acc = acc_sc[g]
        out = acc / pltpu.roll(acc, HEAD_DIM, 1)
        for pp in range(GQA_GROUP // 2):
            a = out[(2 * pp) * GQA_TQ:(2 * pp + 1) * GQA_TQ]
            b = out[(2 * pp + 1) * GQA_TQ:(2 * pp + 2) * GQA_TQ]
            if g == 0:
                pair = jnp.where(low, a, pltpu.roll(b, HEAD_DIM, 1))
            else:
                pair = jnp.where(low, pltpu.roll(a, HEAD_DIM, 1), b)
            col = (g * (GQA_GROUP // 2) + pp) * LANES
            o_ref[:, col:col + LANES] = pair.astype(BF16)


def _gqa_attention(qg, kg, vg, w_gate_up, w_down, batch, seq):
    nq = seq // GQA_TQ
    rows = GQA_GROUP * GQA_TQ
    stat = pltpu.VMEM((GQA_KV_HEADS, rows, LANES), F32)
    n_exp, w_rows = w_gate_up.shape[:2]
    slabs = batch * nq // n_exp
    slab = w_rows // slabs
    assert slabs * n_exp == batch * nq and slab * slabs == w_rows and w_down.shape[:2] == (n_exp, w_rows)
    wspec = lambda w: pl.BlockSpec((1, slab, w.shape[2]),
                                   lambda b, i: ((b * nq + i) // slabs, (b * nq + i) % slabs, 0))
    return pl.pallas_call(
        functools.partial(_gqa_kernel, seq=seq),
        grid=(batch, nq),
        in_specs=[pl.BlockSpec((1, GQA_HEADS, GQA_TQ, LANES), lambda b, i: (b, 0, i, 0)),
                  pl.BlockSpec((seq, LANES), lambda b, i: (b, 0)),
                  pl.BlockSpec((GQA_KV_HEADS, seq, LANES), lambda b, i: (0, b, 0)),
                  wspec(w_gate_up), wspec(w_down)],
        out_specs=[pl.BlockSpec((GQA_TQ, GQA_Q_W), lambda b, i: (b * nq + i, 0)), wspec(w_gate_up), wspec(w_down)],
        out_shape=[jax.ShapeDtypeStruct((batch * seq, GQA_Q_W), BF16),
                   jax.ShapeDtypeStruct(w_gate_up.shape, BF16), jax.ShapeDtypeStruct(w_down.shape, BF16)],
        scratch_shapes=[pltpu.VMEM((GQA_KV_HEADS, seq // GQA_TK, rows, GQA_TK), F32), stat, stat],
        compiler_params=_cparams(("parallel", "parallel")),
        name="gqa_attn",
    )(qg, kg, vg, w_gate_up, w_down)


MERGE_TM = 1024
MERGE_HALVES = 2


def _merge_kernel(x_ref, yna_ref, ygqa_ref, qm_ref, mk_ref, mv_ref, wg_ref, bg_ref, wb_ref, wo_ref,
                  g1_ref, b1_ref, wr_ref, x1_ref, aff_ref, *, alpha):
    rows_per = MERGE_TM // MERGE_HALVES
    mem_scale = 1.0 / math.sqrt(MEM_HEAD_DIM)
    w = wr_ref[...]
    w_hi = w.astype(BF16)
    w_hl = jnp.concatenate([w_hi, (w - w_hi.astype(F32)).astype(BF16)], axis=0)
    for part in range(MERGE_HALVES):
        rows = slice(part * rows_per, (part + 1) * rows_per)
        x = x_ref[rows, :]
        xb = x.astype(BF16)
        ymem = []
        for h in range(MEM_HEADS):
            sl = slice(h * MEM_HEAD_DIM, (h + 1) * MEM_HEAD_DIM)
            s = _dot_nt(qm_ref[rows, sl], mk_ref[:, sl]) * mem_scale
            m = jnp.max(s, axis=-1, keepdims=True)
            p = jnp.exp(s - m)
            l = jnp.sum(p, axis=-1, keepdims=True)
            ymem.append((_dot(p.astype(BF16), mv_ref[:, sl]) / l).astype(BF16))
        branches = (yna_ref[rows, :], ygqa_ref[rows, :], jnp.concatenate(ymem, axis=-1))
        merged = jnp.zeros((rows_per, D_MODEL), F32)
        for g in range(N_BRANCH):
            cols = slice(g * D_MODEL, (g + 1) * D_MODEL)
            gate = jax.nn.sigmoid(_dot(xb, wg_ref[:, cols]) + bg_ref[:, cols])
            merged = merged + gate * _dot(branches[g], wb_ref[g])
        mix = _dot(merged.astype(BF16), wo_ref[...])
        x1 = _layer_norm(alpha * x + mix, g1_ref[...], b1_ref[...])
        x1_ref[rows, :] = x1
        x1_hi = x1.astype(BF16)
        x1_lo = (x1 - x1_hi.astype(F32)).astype(BF16)
        by_hi = _dot_nt(w_hl, x1_hi)
        logits = by_hi[:N_EXPERTS] + by_hi[N_EXPERTS:] + _dot_nt(w_hi, x1_lo)
        m = jnp.max(logits, axis=0, keepdims=True)
        e = jnp.exp(logits - m)
        aff_ref[0, :, rows] = e / jnp.sum(e, axis=0, keepdims=True)


def _merge(x2, yna, ygqa, qm, mk, mv, w_gate, b_gate, w_branch, w_out, g1, b1, w_router, seq, mem_tokens, alpha):
    tokens = x2.shape[0]
    tm = MERGE_TM
    tpb = seq // tm
    row = lambda w: pl.BlockSpec((tm, w), lambda i: (i, 0))
    const = lambda shape: pl.BlockSpec(shape, lambda i: (0,) * len(shape), pipeline_mode=pl.Buffered(1))
    memb = pl.BlockSpec((mem_tokens, MEM_W), lambda i: (i // tpb, 0))
    return pl.pallas_call(
        functools.partial(_merge_kernel, alpha=alpha),
        grid=(tokens // tm,),
        in_specs=[row(D_MODEL), row(NA_W), row(GQA_Q_W), row(MEM_W), memb, memb,
                  const(w_gate.shape), const(b_gate.shape), const(w_branch.shape), const(w_out.shape),
                  const(g1.shape), const(b1.shape), const(w_router.shape)],
        out_specs=[row(D_MODEL), pl.BlockSpec((1, N_EXPERTS, tm), lambda i: (i // tpb, 0, i % tpb))],
        out_shape=[jax.ShapeDtypeStruct((tokens, D_MODEL), F32),
                   jax.ShapeDtypeStruct((tokens // seq, N_EXPERTS, seq), F32)],
        compiler_params=_cparams(("parallel",)),
        name="merge",
    )(x2, yna, ygqa, qm, mk, mv, w_gate, b_gate, w_branch, w_out, g1, b1, w_router)


SEL_BLK = 256
SLOT_HI = 16
SLOT_LO = 32
SEL_PARTS = 5


def _prefix_counts(mask, out_ref, seq):
    r = lax.broadcasted_iota(jnp.int32, (SEL_BLK, SEL_BLK), 0)
    c = lax.broadcasted_iota(jnp.int32, (SEL_BLK, SEL_BLK), 1)
    tri = (r <= c).astype(BF16)
    run = jnp.zeros((mask.shape[0], 1), F32)
    for j in range(seq // SEL_BLK):
        cs = _dot(mask[:, j * SEL_BLK:(j + 1) * SEL_BLK].astype(BF16), tri) + run
        out_ref[:, j * SEL_BLK:(j + 1) * SEL_BLK] = cs
        run = cs[:, SEL_BLK - 1:SEL_BLK]


def _select_kernel(aff_ref, idx_ref, gate_ref, cnt_sc, lhs_sc, rhs_sc, *, seq, cap):
    a = aff_ref[0]
    thr_bits = jnp.zeros((N_EXPERTS, 1), jnp.int32)
    for bit in range(30, -1, -1):
        cand = thr_bits | (1 << bit)
        cnt = jnp.sum((a >= pltpu.bitcast(cand, F32)).astype(F32), axis=-1, keepdims=True)
        thr_bits = jnp.where(cnt >= cap, cand, thr_bits)
    thr = pltpu.bitcast(thr_bits, F32)
    gt = a > thr
    eq = a == thr
    need = cap - jnp.sum(gt.astype(F32), axis=-1, keepdims=True)
    _prefix_counts(eq.astype(F32), cnt_sc, seq)
    sel = gt | (eq & (cnt_sc[...] <= need))
    _prefix_counts(sel.astype(F32), cnt_sc, seq)

    slot = cnt_sc[...] - 1.0
    s_hi = jnp.floor(slot * (1.0 / SLOT_LO))
    s_lo = slot - SLOT_LO * s_hi
    tok = lax.broadcasted_iota(jnp.int32, (1, seq), 1).astype(F32)
    t_hi = jnp.floor(tok * (1.0 / 64.0))
    t_lo = tok - 64.0 * t_hi
    ph = lax.broadcasted_iota(jnp.int32, (SLOT_HI, seq), 0).astype(F32)
    pl_ = lax.broadcasted_iota(jnp.int32, (SLOT_LO, seq), 0).astype(F32)
    for e in range(N_EXPERTS):
        row = slice(e, e + 1)
        onehot_hi = jnp.where((s_hi[row] == ph) & sel[row], 1.0, 0.0)
        g = a[row]
        g1 = g.astype(BF16).astype(F32)
        g2 = (g - g1).astype(BF16).astype(F32)
        g3 = g - g1 - g2
        for part, val in enumerate((t_hi, t_lo, g1, g2, g3)):
            r0 = (part * N_EXPERTS + e) * SLOT_HI
            lhs_sc[r0:r0 + SLOT_HI, :] = (onehot_hi * val).astype(BF16)
        rhs_sc[e * SLOT_LO:(e + 1) * SLOT_LO, :] = jnp.where(s_lo[row] == pl_, 1.0, 0.0).astype(BF16)
    res = _dot_nt(lhs_sc[...], rhs_sc[...])
    n = N_EXPERTS * SLOT_HI
    r = lax.broadcasted_iota(jnp.int32, (n, N_EXPERTS * SLOT_LO), 0) // SLOT_HI
    c = lax.broadcasted_iota(jnp.int32, (n, N_EXPERTS * SLOT_LO), 1) // SLOT_LO
    parts = []
    for part in range(SEL_PARTS):
        blk = jnp.where(r == c, res[part * n:(part + 1) * n], 0.0)
        parts.append(jnp.sum(blk.reshape(N_EXPERTS, SLOT_HI, N_EXPERTS * SLOT_LO), axis=0))
    idx_ref[0] = (64.0 * parts[0] + parts[1]).astype(jnp.int32)
    gate_ref[0] = (parts[2] + parts[3]) + parts[4]


def _ec_select(aff, batch, seq, cap):
    assert cap == SLOT_HI * SLOT_LO
    out = pl.BlockSpec((1, SLOT_HI, N_EXPERTS * SLOT_LO), lambda b: (b, 0, 0))
    idx_t, gate_t = pl.pallas_call(
        functools.partial(_select_kernel, seq=seq, cap=cap),
        grid=(batch,),
        in_specs=[pl.BlockSpec((1, N_EXPERTS, seq), lambda b: (b, 0, 0))],
        out_specs=[out, out],
        out_shape=[jax.ShapeDtypeStruct((batch, SLOT_HI, N_EXPERTS * SLOT_LO), jnp.int32),
                   jax.ShapeDtypeStruct((batch, SLOT_HI, N_EXPERTS * SLOT_LO), F32)],
        scratch_shapes=[pltpu.VMEM((N_EXPERTS, seq), F32),
                        pltpu.VMEM((SEL_PARTS * N_EXPERTS * SLOT_HI, seq), BF16),
                        pltpu.VMEM((N_EXPERTS * SLOT_LO, seq), BF16)],
        compiler_params=_cparams(("parallel",)),
        name="ec_select",
    )(aff)
    order = lambda v: v.reshape(batch, SLOT_HI, N_EXPERTS, SLOT_LO).transpose(0, 2, 1, 3).reshape(batch, N_EXPERTS, cap)
    return order(idx_t), order(gate_t)


FFN_COL_PHASES = 1
LN2_TM = 256


def _gather_rows(idx_ref, src, dst, lo, hi):
    for p in range(lo, hi):
        dst[pl.ds(p, 1), :] = src[pl.ds(idx_ref[0, 0, p], 1), :]


def _scatter_rows(idx_ref, gate_ref, src, acc_sc, lo, hi):
    for p0 in range(lo, hi, 8):
        ps = range(p0, p0 + 8)
        ts = [idx_ref[0, 0, p] for p in ps]
        rows = [acc_sc[pl.ds(t, 1), :] + src[pl.ds(p, 1), :] * gate_ref[0, 0, p] for p, t in zip(ps, ts)]
        for t, row in zip(ts, rows):
            acc_sc[pl.ds(t, 1), :] = row


def _ffn_kernel(idx_ref, idx_next_ref, idx_prev_ref, gate_ref, gate_prev_ref, x_hbm, wgu_ref, wd_ref,
                g2_ref, b2_ref, out_hbm, x_sc, u_sc, xs_sc, act_sc, ye_sc, acc_sc, out_sc, sem_x, sem_out,
                *, cap, d_ff, seq, batch, alpha):
    b = pl.program_id(0)
    e = pl.program_id(1)
    n_tiles = seq // LN2_TM

    def x_copy(seq_id, i):
        rows = pl.ds(i * LN2_TM, LN2_TM)
        return pltpu.make_async_copy(x_hbm.at[seq_id, rows], x_sc.at[rows], sem_x.at[i])

    @pl.when((e == 0) & (b == 0))
    def _():
        for i in range(n_tiles):
            x_copy(0, i).start()
        ye_sc[...] = jnp.zeros(ye_sc.shape, F32)

    @pl.when(e == 0)
    def _():
        for i in range(n_tiles):
            x_copy(b, i).wait()
        acc_sc[...] = jnp.zeros(acc_sc.shape, F32)
        _gather_rows(idx_ref, x_sc, u_sc, 0, cap)

    xs_sc[...] = u_sc[...].astype(BF16)

    width = d_ff // FFN_COL_PHASES
    out_w = D_MODEL // FFN_COL_PHASES
    share = cap // FFN_COL_PHASES
    for ph in range(2 * FFN_COL_PHASES):
        @pl.when(e + ph >= 0)
        def _(ph=ph):
            if ph < FFN_COL_PHASES:
                _scatter_rows(idx_prev_ref, gate_prev_ref, ye_sc, acc_sc, ph * share, (ph + 1) * share)
                xs = xs_sc[...]
                hg = _dot(xs, wgu_ref[0, :, ph * width:(ph + 1) * width])
                hu = _dot(xs, wgu_ref[0, :, d_ff + ph * width:d_ff + (ph + 1) * width])
                act_sc[:, ph * width:(ph + 1) * width] = (hg * jax.nn.sigmoid(hg) * hu).astype(BF16)
            else:
                c = ph - FFN_COL_PHASES
                _gather_rows(idx_next_ref, x_sc, u_sc, c * share, (c + 1) * share)
                ye_sc[:, c * out_w:(c + 1) * out_w] = _dot(act_sc[...], wd_ref[0, :, c * out_w:(c + 1) * out_w])

    @pl.when(e == N_EXPERTS - 1)
    def _():
        def o_copy(i, slot):
            return pltpu.make_async_copy(out_sc.at[slot], out_hbm.at[b, pl.ds(i * LN2_TM, LN2_TM)], sem_out.at[slot])

        _scatter_rows(idx_ref, gate_ref, ye_sc, acc_sc, 0, cap)
        more = b + 1 < batch
        for i in range(n_tiles):
            slot = i % 2
            rows = slice(i * LN2_TM, (i + 1) * LN2_TM)
            if i >= 2:
                o_copy(i - 2, slot).wait()
            out_sc[slot] = _layer_norm(alpha * x_sc[rows, :] + acc_sc[rows, :], g2_ref[...], b2_ref[...])
            o_copy(i, slot).start()

            @pl.when(more)
            def _(i=i):
                x_copy(b + 1, i).start()
        o_copy(n_tiles - 2, n_tiles % 2).wait()
        o_copy(n_tiles - 1, (n_tiles - 1) % 2).wait()


def _ec_ffn(idx, gate, x1, wgu, wd, g2, b2, batch, seq, cap, alpha):
    d_ff = wd.shape[1]
    slot = pl.BlockSpec((1, 1, cap), lambda b, e: (b * N_EXPERTS + e, 0, 0), memory_space=pltpu.SMEM)
    vec = pl.BlockSpec((1, D_MODEL), lambda b, e: (0, 0))
    flat = lambda v: v.reshape(batch * N_EXPERTS, 1, cap)
    idx_next = jnp.concatenate([idx[:, 1:], idx[:, -1:]], axis=1)
    idx_prev = jnp.concatenate([idx[:, :1], idx[:, :-1]], axis=1)
    gate_prev = jnp.concatenate([jnp.zeros_like(gate[:, :1]), gate[:, :-1]], axis=1)
    return pl.pallas_call(
        functools.partial(_ffn_kernel, cap=cap, d_ff=d_ff, seq=seq, batch=batch, alpha=alpha),
        grid=(batch, N_EXPERTS),
        in_specs=[slot, slot, slot, slot, slot,
                  pl.BlockSpec(memory_space=pl.ANY),
                  pl.BlockSpec((1, D_MODEL, 2 * d_ff), lambda b, e: (e, 0, 0)),
                  pl.BlockSpec((1, d_ff, D_MODEL), lambda b, e: (e, 0, 0)),
                  vec, vec],
        out_specs=pl.BlockSpec(memory_space=pl.ANY),
        out_shape=jax.ShapeDtypeStruct((batch, seq, D_MODEL), F32),
        scratch_shapes=[pltpu.VMEM((seq, D_MODEL), F32),
                        pltpu.VMEM((cap, D_MODEL), F32), pltpu.VMEM((cap, D_MODEL), BF16),
                        pltpu.VMEM((cap, d_ff), BF16), pltpu.VMEM((cap, D_MODEL), F32),
                        pltpu.VMEM((seq, D_MODEL), F32), pltpu.VMEM((2, LN2_TM, D_MODEL), F32),
                        pltpu.SemaphoreType.DMA((seq // LN2_TM,)), pltpu.SemaphoreType.DMA((2,))],
        compiler_params=_cparams(("arbitrary", "arbitrary"), FFN_VMEM_LIMIT),
        name="ec_ffn",
    )(flat(idx), flat(idx_next), flat(idx_prev), flat(gate), flat(gate_prev), x1, wgu, wd, g2, b2)


def kernel(x, mem, w_in, b_gate, na_rpb, gqa_q_gain, gqa_k_gain, w_mem_kv, w_branch, w_out,
           ln1_g, ln1_b, w_router, w_gate_up, w_down, ln2_g, ln2_b):
    batch, seq, d = x.shape
    mem_tokens = mem.shape[1]
    depth = w_in.shape[0]
    alpha = (2 * depth) ** 0.25
    cap = EC_CAPACITY_FACTOR * seq // N_EXPERTS
    assert d == D_MODEL and seq == GRID_W * GRID_W and w_in.shape[2] == QKV_W + N_BRANCH * D_MODEL
    row = lambda v: v.reshape(1, -1)
    for l in range(depth):
        x2 = x.reshape(batch * seq, d)
        w_qkv = w_in[l, :, :QKV_W].astype(BF16)
        w_gate = w_in[l, :, QKV_W:].astype(BF16)
        qna, kna, vna, qg, kg, vg, qm = _in_proj(
            x2, w_qkv, row(jnp.tile(gqa_q_gain[l], GQA_HEADS)), row(jnp.tile(gqa_k_gain[l], GQA_KV_HEADS)),
            batch, seq)
        mk, mv = _mem_kv(mem.reshape(batch * mem_tokens, d), w_mem_kv[l].astype(BF16))
        yna = _na_attention(qna, kna, vna, *_na_bias_tiles(na_rpb[l]), batch, seq)
        ygqa, wgu_bf, wd_bf = _gqa_attention(qg, kg, vg, w_gate_up[l], w_down[l], batch, seq)
        x1, aff = _merge(x2, yna, ygqa, qm, mk, mv, w_gate, row(b_gate[l]), w_branch[l].astype(BF16),
                              w_out[l].astype(BF16), row(ln1_g[l]), row(ln1_b[l]), w_router[l].T, seq,
                              mem_tokens, alpha)
        idx, gate = _ec_select(aff, batch, seq, cap)
        x = _ec_ffn(idx, gate, x1.reshape(batch, seq, d), wgu_bf, wd_bf, row(ln2_g[l]), row(ln2_b[l]),
                    batch, seq, cap, alpha)
    return x
```

```python
import functools
import math

import jax
import jax.numpy as jnp
import numpy as np
from jax import lax
from jax.experimental import pallas as pl
from jax.experimental.pallas import tpu as pltpu

F32 = jnp.float32
BF16 = jnp.bfloat16

D_MODEL = 1024
GRID_W = 64
HEAD_DIM = 64
LANES = 128
NA_HEADS = 8
NA_KH = 8
NA_KW = 16
NA_W = NA_HEADS * HEAD_DIM
NA_QROWS = 4
NA_KROWS = 12
GQA_HEADS = 8
GQA_KV_HEADS = 2
GQA_GROUP = GQA_HEADS // GQA_KV_HEADS
GQA_Q_W = GQA_HEADS * HEAD_DIM
GQA_KV_W = GQA_KV_HEADS * HEAD_DIM
ROPE_THETA = 10000.0
MEM_HEADS = 4
MEM_HEAD_DIM = 128
MEM_W = MEM_HEADS * MEM_HEAD_DIM
N_BRANCH = 3
N_EXPERTS = 16
EC_CAPACITY_FACTOR = 2
LN_EPS = 1e-5
RMS_EPS = 1e-6
NEG_INF = -1e30
QKV_W = 3 * NA_W + GQA_Q_W + 2 * GQA_KV_W + MEM_W

VMEM_LIMIT = 56 * 1024 * 1024
FFN_VMEM_LIMIT = 60 * 1024 * 1024


def _cparams(sem, vmem=VMEM_LIMIT):
    return pltpu.CompilerParams(dimension_semantics=sem, vmem_limit_bytes=vmem)


def _dot(a, b):
    return jnp.dot(a, b, preferred_element_type=F32)


def _dot_nt(a, b):
    return lax.dot_general(a, b, (((1,), (1,)), ((), ())), preferred_element_type=F32)


def _layer_norm(v, g, b):
    mu = jnp.mean(v, axis=-1, keepdims=True)
    var = jnp.mean(jnp.square(v - mu), axis=-1, keepdims=True)
    return (v - mu) * lax.rsqrt(var + LN_EPS) * g + b


def _na_bias_kernel(rpb_ref, out_ref):
    n = GRID_W * GRID_W
    col = lax.broadcasted_iota(jnp.int32, (32, n), 1)
    d = lax.broadcasted_iota(jnp.int32, (32, n), 0)
    c, kc = col // GRID_W, col % GRID_W
    onehot = (kc - c + (NA_KW - 1) == d).astype(F32)
    t = jnp.dot(rpb_ref[...], onehot, precision=lax.Precision.HIGHEST, preferred_element_type=F32)
    col1 = lax.broadcasted_iota(jnp.int32, (1, n), 1)
    c1, kc1 = col1 // GRID_W, col1 % GRID_W
    start = jnp.clip(c1 - NA_KW // 2, 0, GRID_W - NA_KW)
    in_win = (kc1 >= start) & (kc1 < start + NA_KW)
    out_ref[...] = jnp.where(in_win, t * math.log2(math.e), NEG_INF)


def _na_bias_tiles(rpb):
    h = rpb.shape[0]
    nd = 2 * NA_KH - 1
    rpb2 = jnp.pad(rpb.reshape(h * nd, 2 * NA_KW - 1), ((0, 0), (0, 1)))
    t = pl.pallas_call(
        _na_bias_kernel,
        out_shape=jax.ShapeDtypeStruct((h * nd, GRID_W * GRID_W), F32),
        name="na_bias",
    )(rpb2).reshape(h, nd, GRID_W, GRID_W)
    t = jnp.concatenate([t, jnp.full((h, 1, GRID_W, GRID_W), NEG_INF, F32)], axis=1)
    which = np.full((3, NA_QROWS, NA_KROWS), nd, np.int32)
    for variant in range(3):
        for i in range(NA_QROWS):
            for j in range(NA_KROWS):
                if variant == 0:
                    dr, ok = j - i, j < NA_KH
                elif variant == 1:
                    dr = j - i - NA_KH // 2
                    ok = -(NA_KH // 2) <= dr < NA_KH // 2
                else:
                    dr, ok = j - i - NA_KH, j >= NA_KROWS - NA_KH
                if ok:
                    which[variant, i, j] = dr + NA_KH - 1
    return t, which


def _fill_na_tiles(t_ref, bias_sc, which):
    for variant in range(which.shape[0]):
        for hh in range(2):
            for i in range(NA_QROWS):
                for j in range(0, NA_KROWS, 2):
                    pair = jnp.concatenate([t_ref[hh, int(which[variant, i, j])],
                                            t_ref[hh, int(which[variant, i, j + 1])]], axis=-1)
                    bias_sc[variant, hh, i * GRID_W:(i + 1) * GRID_W, j * GRID_W:(j + 2) * GRID_W] = pair


PROJ_TM = 1024


def _rope_tables(seq_len):
    t = jnp.arange(seq_len)
    row = (t // GRID_W).astype(F32)
    col = (t % GRID_W).astype(F32)
    half = HEAD_DIM // 2
    inv = ROPE_THETA ** (-jnp.arange(0, half, 2, dtype=F32) / half)
    ang = jnp.concatenate([row[:, None] * inv, col[:, None] * inv], axis=-1)
    cos, sin = jnp.cos(ang), jnp.sin(ang)
    cos_l = jnp.repeat(cos, 2, axis=-1)
    sin_l = jnp.stack([-sin, sin], axis=-1).reshape(seq_len, HEAD_DIM)
    return jnp.tile(cos_l, (1, 2)), jnp.tile(sin_l, (1, 2))


def _head_sums(sq, width):
    r = lax.broadcasted_iota(jnp.int32, (width, width), 0) // HEAD_DIM
    c = lax.broadcasted_iota(jnp.int32, (width, width), 1) // HEAD_DIM
    return _dot(sq.astype(BF16), (r == c).astype(BF16))


def _rope(v, cos, sin_signed):
    lane = lax.broadcasted_iota(jnp.int32, v.shape, 1)
    swapped = jnp.where(lane % 2 == 0, pltpu.roll(v, LANES - 1, 1), pltpu.roll(v, 1, 1))
    return v * cos + swapped * sin_signed


def _proj_kernel(x_ref, w_ref, qgain_ref, kgain_ref, cos_ref, sin_ref,
                 qna_ref, kna_ref, vna_ref, qg_ref, kg_ref, vg_ref, qm_ref):
    xb = x_ref[...].astype(BF16)
    qscale = math.log2(math.e) / math.sqrt(HEAD_DIM)
    o = 0
    qna_ref[...] = (_dot(xb, w_ref[:, o:o + NA_W]) * qscale).astype(BF16)
    o += NA_W
    kna_ref[...] = _dot(xb, w_ref[:, o:o + NA_W]).astype(BF16)
    o += NA_W
    vna_ref[...] = _dot(xb, w_ref[:, o:o + NA_W]).astype(BF16)
    o += NA_W
    cos, sin = cos_ref[...], sin_ref[...]
    lane = lax.broadcasted_iota(jnp.int32, (PROJ_TM, LANES), 1)
    low = lane < HEAD_DIM

    zq = _dot(xb, w_ref[:, o:o + GQA_Q_W])
    o += GQA_Q_W
    ms = _head_sums(zq * zq, GQA_Q_W) * (1.0 / HEAD_DIM)
    zq = zq * lax.rsqrt(ms + RMS_EPS) * qgain_ref[...]
    for p in range(GQA_HEADS // 2):
        r = _rope(zq[:, p * LANES:(p + 1) * LANES], cos, sin) * qscale
        rolled = pltpu.roll(r, HEAD_DIM, 1)
        if (2 * p) // GQA_GROUP == 0:
            even, odd = jnp.where(low, r, 0.0), jnp.where(low, rolled, 0.0)
        else:
            even, odd = jnp.where(low, 0.0, rolled), jnp.where(low, 0.0, r)
        qg_ref[0, 2 * p] = even.astype(BF16)
        qg_ref[0, 2 * p + 1] = odd.astype(BF16)

    zk = _dot(xb, w_ref[:, o:o + GQA_KV_W])
    o += GQA_KV_W
    ms = _head_sums(zk * zk, GQA_KV_W) * (1.0 / HEAD_DIM)
    zk = zk * lax.rsqrt(ms + RMS_EPS) * kgain_ref[...]
    kg_ref[...] = _rope(zk, cos, sin).astype(BF16)
    zv = _dot(xb, w_ref[:, o:o + GQA_KV_W])
    vg_ref[0] = jnp.where(low, zv, 1.0).astype(BF16)
    vg_ref[1] = jnp.where(low, 1.0, zv).astype(BF16)
    o += GQA_KV_W
    qm_ref[...] = _dot(xb, w_ref[:, o:o + MEM_W]).astype(BF16)


def _in_proj(x2, w_qkv, qgain, kgain, batch, seq):
    tokens = x2.shape[0]
    tm = PROJ_TM
    spb = seq // tm
    cos, sin = _rope_tables(seq)
    row = lambda w: pl.BlockSpec((tm, w), lambda i: (i, 0))
    const = lambda shape: pl.BlockSpec(shape, lambda i: (0,) * len(shape), pipeline_mode=pl.Buffered(1))
    tbl = pl.BlockSpec((tm, LANES), lambda i: (i % spb, 0))
    sds = lambda w: jax.ShapeDtypeStruct((tokens, w), BF16)
    return pl.pallas_call(
        _proj_kernel,
        grid=(tokens // tm,),
        in_specs=[row(D_MODEL), const((D_MODEL, QKV_W)), const((1, GQA_Q_W)), const((1, GQA_KV_W)), tbl, tbl],
        out_specs=[row(NA_W), row(NA_W), row(NA_W),
                   pl.BlockSpec((1, GQA_HEADS, tm, LANES), lambda i: (i // spb, 0, i % spb, 0)),
                   row(GQA_KV_W), pl.BlockSpec((GQA_KV_HEADS, tm, LANES), lambda i: (0, i, 0)), row(MEM_W)],
        out_shape=[sds(NA_W), sds(NA_W), sds(NA_W),
                   jax.ShapeDtypeStruct((batch, GQA_HEADS, seq, LANES), BF16),
                   sds(GQA_KV_W), jax.ShapeDtypeStruct((GQA_KV_HEADS, tokens, LANES), BF16), sds(MEM_W)],
        compiler_params=_cparams(("parallel",)),
        name="in_proj",
    )(x2, w_qkv, qgain, kgain, cos, sin)


def _memkv_kernel(m_ref, w_ref, k_ref, v_ref):
    z = _dot(m_ref[...].astype(BF16), w_ref[...])
    k_ref[...] = z[:, :MEM_W].astype(BF16)
    v_ref[...] = z[:, MEM_W:].astype(BF16)


def _mem_kv(mem2, w_kv):
    rows = mem2.shape[0]
    tm = 256
    return pl.pallas_call(
        _memkv_kernel,
        grid=(rows // tm,),
        in_specs=[pl.BlockSpec((tm, D_MODEL), lambda i: (i, 0)),
                  pl.BlockSpec((D_MODEL, 2 * MEM_W), lambda i: (0, 0))],
        out_specs=[pl.BlockSpec((tm, MEM_W), lambda i: (i, 0))] * 2,
        out_shape=[jax.ShapeDtypeStruct((rows, MEM_W), BF16)] * 2,
        compiler_params=_cparams(("parallel",)),
        name="mem_kv",
    )(mem2, w_kv)


NA_TQ = NA_QROWS * GRID_W
NA_TK = NA_KROWS * GRID_W


NA_GROUPS_PER_STEP = 8


def _na_kernel(q_ref, k_ref, v_ref, t_ref, o_ref, bias_sc, *, n_groups, which):
    @pl.when((pl.program_id(1) == 0) & (pl.program_id(2) == 0))
    def _():
        _fill_na_tiles(t_ref, bias_sc, which)

    low = lax.broadcasted_iota(jnp.int32, (NA_TQ, LANES), 1) < HEAD_DIM
    tiles = NA_TK // LANES
    for sub in range(NA_GROUPS_PER_STEP):
        rg = pl.program_id(2) * NA_GROUPS_PER_STEP + sub
        key_row0 = jnp.clip(NA_QROWS * rg - NA_KH // 2, 0, GRID_W - NA_KROWS)
        start = pl.multiple_of(key_row0 * GRID_W, GRID_W)
        variant = jnp.where(rg == 0, 0, jnp.where(rg == n_groups - 1, 2, 1))
        k = k_ref[pl.ds(start, NA_TK), :]
        v = v_ref[pl.ds(start, NA_TK), :]
        q = q_ref[sub * NA_TQ:(sub + 1) * NA_TQ, :]
        qq = jnp.concatenate([jnp.where(low, q, jnp.zeros_like(q)), jnp.where(low, jnp.zeros_like(q), q)], axis=0)
        s = _dot_nt(qq, k) + bias_sc[variant].reshape(2 * NA_TQ, NA_TK)
        st = [s[:, j * LANES:(j + 1) * LANES] for j in range(tiles)]
        part = st[0]
        for t in st[1:]:
            part = jnp.maximum(part, t)
        m = jnp.broadcast_to(jnp.max(part, axis=-1, keepdims=True), part.shape)
        ps = [jnp.exp2(t - m) for t in st]
        lpart = ps[0]
        for t in ps[1:]:
            lpart = lpart + t
        l = jnp.broadcast_to(jnp.sum(lpart, axis=-1, keepdims=True), lpart.shape)
        out = _dot(jnp.concatenate([t.astype(BF16) for t in ps], axis=-1), v) / l
        o_ref[sub * NA_TQ:(sub + 1) * NA_TQ, :] = jnp.where(low, out[:NA_TQ], out[NA_TQ:]).astype(BF16)


def _na_attention(qna, kna, vna, tiles, which, batch, seq):
    n_groups = seq // NA_TQ
    steps = n_groups // NA_GROUPS_PER_STEP
    pairs = NA_HEADS // 2
    kv = pl.BlockSpec((seq, LANES), lambda p, b, g: (b, p))
    qo = pl.BlockSpec((NA_GROUPS_PER_STEP * NA_TQ, LANES), lambda p, b, g: (b * steps + g, p))
    return pl.pallas_call(
        functools.partial(_na_kernel, n_groups=n_groups, which=which),
        grid=(pairs, batch, steps),
        in_specs=[qo, kv, kv, pl.BlockSpec((2,) + tiles.shape[1:], lambda p, b, g: (p, 0, 0, 0))],
        out_specs=qo,
        out_shape=jax.ShapeDtypeStruct(qna.shape, BF16),
        scratch_shapes=[pltpu.VMEM((which.shape[0], 2, NA_TQ, NA_TK), F32)],
        compiler_params=_cparams(("arbitrary", "arbitrary", "arbitrary")),
        name="na_attn",
    )(qna, kna, vna, tiles)


GQA_TQ = 128
GQA_TK = 1024


def _gqa_kernel(q_ref, k_ref, v_ref, wgu_ref, wd_ref, o_ref, wgu_bf_ref, wd_bf_ref, s_sc, mx_sc, acc_sc, *, seq):
    wgu_bf_ref[...] = wgu_ref[...].astype(BF16)
    wd_bf_ref[...] = wd_ref[...].astype(BF16)
    rows = GQA_GROUP * GQA_TQ
    n_chunks = seq // GQA_TK
    tiles = GQA_TK // LANES
    groups = range(GQA_KV_HEADS)
    lane = lax.broadcasted_iota(jnp.int32, (GQA_TQ, LANES), 1)
    low = lane < HEAD_DIM
    qs = [q_ref[0, g * GQA_GROUP:(g + 1) * GQA_GROUP].reshape(rows, LANES) for g in groups]
    mx_sc[...] = jnp.full(mx_sc.shape, -jnp.inf, F32)

    for c in range(n_chunks):
        k = k_ref[c * GQA_TK:(c + 1) * GQA_TK, :]
        for g in groups:
            s = _dot_nt(qs[g], k)
            s_sc[g, c] = s
            part = s[:, :LANES]
            for j in range(1, tiles):
                part = jnp.maximum(part, s[:, j * LANES:(j + 1) * LANES])
            mx_sc[g] = jnp.maximum(mx_sc[g], part)
    ms = [jnp.broadcast_to(jnp.max(mx_sc[g], axis=-1, keepdims=True), (rows, LANES)) for g in groups]
    acc_sc[...] = jnp.zeros(acc_sc.shape, F32)

    for c in range(n_chunks):
        for g in groups:
            p = [jnp.exp2(s_sc[g, c, :, j * LANES:(j + 1) * LANES] - ms[g]).astype(BF16) for j in range(tiles)]
            acc_sc[g] += _dot(jnp.concatenate(p, axis=-1), v_ref[g, c * GQA_TK:(c + 1) * GQA_TK, :])
    for g in groups:
        acc = acc_sc[g]
        out = acc / pltpu.roll(acc, HEAD_DIM, 1)
        for pp in range(GQA_GROUP // 2):
            a = out[(2 * pp) * GQA_TQ:(2 * pp + 1) * GQA_TQ]
            b = out[(2 * pp + 1) * GQA_TQ:(2 * pp + 2) * GQA_TQ]
            if g == 0:
                pair = jnp.where(low, a, pltpu.roll(b, HEAD_DIM, 1))
            else:
                pair = jnp.where(low, pltpu.roll(a, HEAD_DIM, 1), b)
            col = (g * (GQA_GROUP // 2) + pp) * LANES
            o_ref[:, col:col + LANES] = pair.astype(BF16)


def _gqa_attention(qg, kg, vg, w_gate_up, w_down, batch, seq):
    nq = seq // GQA_TQ
    rows = GQA_GROUP * GQA_TQ
    stat = pltpu.VMEM((GQA_KV_HEADS, rows, LANES), F32)
    n_exp, w_rows = w_gate_up.shape[:2]
    slabs = batch * nq // n_exp
    slab = w_rows // slabs
    assert slabs * n_exp == batch * nq and slab * slabs == w_rows and w_down.shape[:2] == (n_exp, w_rows)
    wspec = lambda w: pl.BlockSpec((1, slab, w.shape[2]),
                                   lambda b, i: ((b * nq + i) // slabs, (b * nq + i) % slabs, 0))
    return pl.pallas_call(
        functools.partial(_gqa_kernel, seq=seq),
        grid=(batch, nq),
        in_specs=[pl.BlockSpec((1, GQA_HEADS, GQA_TQ, LANES), lambda b, i: (b, 0, i, 0)),
                  pl.BlockSpec((seq, LANES), lambda b, i: (b, 0)),
                  pl.BlockSpec((GQA_KV_HEADS, seq, LANES), lambda b, i: (0, b, 0)),
                  wspec(w_gate_up), wspec(w_down)],
        out_specs=[pl.BlockSpec((GQA_TQ, GQA_Q_W), lambda b, i: (b * nq + i, 0)), wspec(w_gate_up), wspec(w_down)],
        out_shape=[jax.ShapeDtypeStruct((batch * seq, GQA_Q_W), BF16),
                   jax.ShapeDtypeStruct(w_gate_up.shape, BF16), jax.ShapeDtypeStruct(w_down.shape, BF16)],
        scratch_shapes=[pltpu.VMEM((GQA_KV_HEADS, seq // GQA_TK, rows, GQA_TK), F32), stat, stat],
        compiler_params=_cparams(("parallel", "parallel")),
        name="gqa_attn",
    )(qg, kg, vg, w_gate_up, w_down)


MERGE_TM = 1024
MERGE_HALVES = 2


def _merge_kernel(x_ref, yna_ref, ygqa_ref, qm_ref, mk_ref, mv_ref, wg_ref, bg_ref, wb_ref, wo_ref,
                  g1_ref, b1_ref, wr_ref, x1_ref, aff_ref, *, alpha):
    rows_per = MERGE_TM // MERGE_HALVES
    mem_scale = 1.0 / math.sqrt(MEM_HEAD_DIM)
    w = wr_ref[...]
    w_hi = w.astype(BF16)
    w_hl = jnp.concatenate([w_hi, (w - w_hi.astype(F32)).astype(BF16)], axis=0)
    for part in range(MERGE_HALVES):
        rows = slice(part * rows_per, (part + 1) * rows_per)
        x = x_ref[rows, :]
        xb = x.astype(BF16)
        ymem = []
        for h in range(MEM_HEADS):
            sl = slice(h * MEM_HEAD_DIM, (h + 1) * MEM_HEAD_DIM)
            s = _dot_nt(qm_ref[rows, sl], mk_ref[:, sl]) * mem_scale
            m = jnp.max(s, axis=-1, keepdims=True)
            p = jnp.exp(s - m)
            l = jnp.sum(p, axis=-1, keepdims=True)
            ymem.append((_dot(p.astype(BF16), mv_ref[:, sl]) / l).astype(BF16))
        branches = (yna_ref[rows, :], ygqa_ref[rows, :], jnp.concatenate(ymem, axis=-1))
        merged = jnp.zeros((rows_per, D_MODEL), F32)
        for g in range(N_BRANCH):
            cols = slice(g * D_MODEL, (g + 1) * D_MODEL)
            gate = jax.nn.sigmoid(_dot(xb, wg_ref[:, cols]) + bg_ref[:, cols])
            merged = merged + gate * _dot(branches[g], wb_ref[g])
        mix = _dot(merged.astype(BF16), wo_ref[...])
        x1 = _layer_norm(alpha * x + mix, g1_ref[...], b1_ref[...])
        x1_ref[rows, :] = x1
        x1_hi = x1.astype(BF16)
        x1_lo = (x1 - x1_hi.astype(F32)).astype(BF16)
        by_hi = _dot_nt(w_hl, x1_hi)
        logits = by_hi[:N_EXPERTS] + by_hi[N_EXPERTS:] + _dot_nt(w_hi, x1_lo)
        m = jnp.max(logits, axis=0, keepdims=True)
        e = jnp.exp(logits - m)
        aff_ref[0, :, rows] = e / jnp.sum(e, axis=0, keepdims=True)


def _merge(x2, yna, ygqa, qm, mk, mv, w_gate, b_gate, w_branch, w_out, g1, b1, w_router, seq, mem_tokens, alpha):
    tokens = x2.shape[0]
    tm = MERGE_TM
    tpb = seq // tm
    row = lambda w: pl.BlockSpec((tm, w), lambda i: (i, 0))
    const = lambda shape: pl.BlockSpec(shape, lambda i: (0,) * len(shape), pipeline_mode=pl.Buffered(1))
    memb = pl.BlockSpec((mem_tokens, MEM_W), lambda i: (i // tpb, 0))
    return pl.pallas_call(
        functools.partial(_merge_kernel, alpha=alpha),
        grid=(tokens // tm,),
        in_specs=[row(D_MODEL), row(NA_W), row(GQA_Q_W), row(MEM_W), memb, memb,
                  const(w_gate.shape), const(b_gate.shape), const(w_branch.shape), const(w_out.shape),
                  const(g1.shape), const(b1.shape), const(w_router.shape)],
        out_specs=[row(D_MODEL), pl.BlockSpec((1, N_EXPERTS, tm), lambda i: (i // tpb, 0, i % tpb))],
        out_shape=[jax.ShapeDtypeStruct((tokens, D_MODEL), F32),
                   jax.ShapeDtypeStruct((tokens // seq, N_EXPERTS, seq), F32)],
        compiler_params=_cparams(("parallel",)),
        name="merge",
    )(x2, yna, ygqa, qm, mk, mv, w_gate, b_gate, w_branch, w_out, g1, b1, w_router)


SEL_BLK = 256
SLOT_HI = 16
SLOT_LO = 32
SEL_PARTS = 5


def _prefix_counts(mask, out_ref, seq):
    r = lax.broadcasted_iota(jnp.int32, (SEL_BLK, SEL_BLK), 0)
    c = lax.broadcasted_iota(jnp.int32, (SEL_BLK, SEL_BLK), 1)
    tri = (r <= c).astype(BF16)
    run = jnp.zeros((mask.shape[0], 1), F32)
    for j in range(seq // SEL_BLK):
        cs = _dot(mask[:, j * SEL_BLK:(j + 1) * SEL_BLK].astype(BF16), tri) + run
        out_ref[:, j * SEL_BLK:(j + 1) * SEL_BLK] = cs
        run = cs[:, SEL_BLK - 1:SEL_BLK]


def _select_kernel(aff_ref, idx_ref, gate_ref, cnt_sc, lhs_sc, rhs_sc, *, seq, cap):
    a = aff_ref[0]
    thr_bits = jnp.zeros((N_EXPERTS, 1), jnp.int32)
    for bit in range(30, -1, -1):
        cand = thr_bits | (1 << bit)
        cnt = jnp.sum((a >= pltpu.bitcast(cand, F32)).astype(F32), axis=-1, keepdims=True)
        thr_bits = jnp.where(cnt >= cap, cand, thr_bits)
    thr = pltpu.bitcast(thr_bits, F32)
    gt = a > thr
    eq = a == thr
    need = cap - jnp.sum(gt.astype(F32), axis=-1, keepdims=True)
    _prefix_counts(eq.astype(F32), cnt_sc, seq)
    sel = gt | (eq & (cnt_sc[...] <= need))
    _prefix_counts(sel.astype(F32), cnt_sc, seq)

    slot = cnt_sc[...] - 1.0
    s_hi = jnp.floor(slot * (1.0 / SLOT_LO))
    s_lo = slot - SLOT_LO * s_hi
    tok = lax.broadcasted_iota(jnp.int32, (1, seq), 1).astype(F32)
    t_hi = jnp.floor(tok * (1.0 / 64.0))
    t_lo = tok - 64.0 * t_hi
    ph = lax.broadcasted_iota(jnp.int32, (SLOT_HI, seq), 0).astype(F32)
    pl_ = lax.broadcasted_iota(jnp.int32, (SLOT_LO, seq), 0).astype(F32)
    for e in range(N_EXPERTS):
        row = slice(e, e + 1)
        onehot_hi = jnp.where((s_hi[row] == ph) & sel[row], 1.0, 0.0)
        g = a[row]
        g1 = g.astype(BF16).astype(F32)
        g2 = (g - g1).astype(BF16).astype(F32)
        g3 = g - g1 - g2
        for part, val in enumerate((t_hi, t_lo, g1, g2, g3)):
            r0 = (part * N_EXPERTS + e) * SLOT_HI
            lhs_sc[r0:r0 + SLOT_HI, :] = (onehot_hi * val).astype(BF16)
        rhs_sc[e * SLOT_LO:(e + 1) * SLOT_LO, :] = jnp.where(s_lo[row] == pl_, 1.0, 0.0).astype(BF16)
    res = _dot_nt(lhs_sc[...], rhs_sc[...])
    n = N_EXPERTS * SLOT_HI
    r = lax.broadcasted_iota(jnp.int32, (n, N_EXPERTS * SLOT_LO), 0) // SLOT_HI
    c = lax.broadcasted_iota(jnp.int32, (n, N_EXPERTS * SLOT_LO), 1) // SLOT_LO
    parts = []
    for part in range(SEL_PARTS):
        blk = jnp.where(r == c, res[part * n:(part + 1) * n], 0.0)
        parts.append(jnp.sum(blk.reshape(N_EXPERTS, SLOT_HI, N_EXPERTS * SLOT_LO), axis=0))
    idx_ref[0] = (64.0 * parts[0] + parts[1]).astype(jnp.int32)
    gate_ref[0] = (parts[2] + parts[3]) + parts[4]


def _ec_select(aff, batch, seq, cap):
    assert cap == SLOT_HI * SLOT_LO
    out = pl.BlockSpec((1, SLOT_HI, N_EXPERTS * SLOT_LO), lambda b: (b, 0, 0))
    idx_t, gate_t = pl.pallas_call(
        functools.partial(_select_kernel, seq=seq, cap=cap),
        grid=(batch,),
        in_specs=[pl.BlockSpec((1, N_EXPERTS, seq), lambda b: (b, 0, 0))],
        out_specs=[out, out],
        out_shape=[jax.ShapeDtypeStruct((batch, SLOT_HI, N_EXPERTS * SLOT_LO), jnp.int32),
                   jax.ShapeDtypeStruct((batch, SLOT_HI, N_EXPERTS * SLOT_LO), F32)],
        scratch_shapes=[pltpu.VMEM((N_EXPERTS, seq), F32),
                        pltpu.VMEM((SEL_PARTS * N_EXPERTS * SLOT_HI, seq), BF16),
                        pltpu.VMEM((N_EXPERTS * SLOT_LO, seq), BF16)],
        compiler_params=_cparams(("parallel",)),
        name="ec_select",
    )(aff)
    order = lambda v: v.reshape(batch, SLOT_HI, N_EXPERTS, SLOT_LO).transpose(0, 2, 1, 3).reshape(batch, N_EXPERTS, cap)
    return order(idx_t), order(gate_t)


FFN_COL_PHASES = 1
LN2_TM = 256


def _gather_rows(idx_ref, src, dst, lo, hi):
    for p in range(lo, hi):
        dst[pl.ds(p, 1), :] = src[pl.ds(idx_ref[0, 0, p], 1), :]


def _scatter_rows(idx_ref, gate_ref, src, acc_sc, lo, hi):
    for p0 in range(lo, hi, 8):
        ps = range(p0, p0 + 8)
        ts = [idx_ref[0, 0, p] for p in ps]
        rows = [acc_sc[pl.ds(t, 1), :] + src[pl.ds(p, 1), :] * gate_ref[0, 0, p] for p, t in zip(ps, ts)]
        for t, row in zip(ts, rows):
            acc_sc[pl.ds(t, 1), :] = row


def _ffn_kernel(idx_ref, idx_next_ref, idx_prev_ref, gate_ref, gate_prev_ref, x_hbm, wgu_ref, wd_ref,
                g2_ref, b2_ref, out_hbm, x_sc, u_sc, xs_sc, act_sc, ye_sc, acc_sc, out_sc, sem_x, sem_out,
                *, cap, d_ff, seq, batch, alpha):
    b = pl.program_id(0)
    e = pl.program_id(1)
    n_tiles = seq // LN2_TM

    def x_copy(seq_id, i):
        rows = pl.ds(i * LN2_TM, LN2_TM)
        return pltpu.make_async_copy(x_hbm.at[seq_id, rows], x_sc.at[rows], sem_x.at[i])

    @pl.when((e == 0) & (b == 0))
    def _():
        for i in range(n_tiles):
            x_copy(0, i).start()
        ye_sc[...] = jnp.zeros(ye_sc.shape, F32)

    @pl.when(e == 0)
    def _():
        for i in range(n_tiles):
            x_copy(b, i).wait()
        acc_sc[...] = jnp.zeros(acc_sc.shape, F32)
        _gather_rows(idx_ref, x_sc, u_sc, 0, cap)

    xs_sc[...] = u_sc[...].astype(BF16)

    width = d_ff // FFN_COL_PHASES
    out_w = D_MODEL // FFN_COL_PHASES
    share = cap // FFN_COL_PHASES
    for ph in range(2 * FFN_COL_PHASES):
        @pl.when(e + ph >= 0)
        def _(ph=ph):
            if ph < FFN_COL_PHASES:
                _scatter_rows(idx_prev_ref, gate_prev_ref, ye_sc, acc_sc, ph * share, (ph + 1) * share)
                xs = xs_sc[...]
                hg = _dot(xs, wgu_ref[0, :, ph * width:(ph + 1) * width])
                hu = _dot(xs, wgu_ref[0, :, d_ff + ph * width:d_ff + (ph + 1) * width])
                act_sc[:, ph * width:(ph + 1) * width] = (hg * jax.nn.sigmoid(hg) * hu).astype(BF16)
            else:
                c = ph - FFN_COL_PHASES
                _gather_rows(idx_next_ref, x_sc, u_sc, c * share, (c + 1) * share)
                ye_sc[:, c * out_w:(c + 1) * out_w] = _dot(act_sc[...], wd_ref[0, :, c * out_w:(c + 1) * out_w])

    @pl.when(e == N_EXPERTS - 1)
    def _():
        def o_copy(i, slot):
            return pltpu.make_async_copy(out_sc.at[slot], out_hbm.at[b, pl.ds(i * LN2_TM, LN2_TM)], sem_out.at[slot])

        _scatter_rows(idx_ref, gate_ref, ye_sc, acc_sc, 0, cap)
        more = b + 1 < batch
        for i in range(n_tiles):
            slot = i % 2
            rows = slice(i * LN2_TM, (i + 1) * LN2_TM)
            if i >= 2:
                o_copy(i - 2, slot).wait()
            out_sc[slot] = _layer_norm(alpha * x_sc[rows, :] + acc_sc[rows, :], g2_ref[...], b2_ref[...])
            o_copy(i, slot).start()

            @pl.when(more)
            def _(i=i):
                x_copy(b + 1, i).start()
        o_copy(n_tiles - 2, n_tiles % 2).wait()
        o_copy(n_tiles - 1, (n_tiles - 1) % 2).wait()


def _ec_ffn(idx, gate, x1, wgu, wd, g2, b2, batch, seq, cap, alpha):
    d_ff = wd.shape[1]
    slot = pl.BlockSpec((1, 1, cap), lambda b, e: (b * N_EXPERTS + e, 0, 0), memory_space=pltpu.SMEM)
    vec = pl.BlockSpec((1, D_MODEL), lambda b, e: (0, 0))
    flat = lambda v: v.reshape(batch * N_EXPERTS, 1, cap)
    idx_next = jnp.concatenate([idx[:, 1:], idx[:, -1:]], axis=1)
    idx_prev = jnp.concatenate([idx[:, :1], idx[:, :-1]], axis=1)
    gate_prev = jnp.concatenate([jnp.zeros_like(gate[:, :1]), gate[:, :-1]], axis=1)
    return pl.pallas_call(
        functools.partial(_ffn_kernel, cap=cap, d_ff=d_ff, seq=seq, batch=batch, alpha=alpha),
        grid=(batch, N_EXPERTS),
        in_specs=[slot, slot, slot, slot, slot,
                  pl.BlockSpec(memory_space=pl.ANY),
                  pl.BlockSpec((1, D_MODEL, 2 * d_ff), lambda b, e: (e, 0, 0)),
                  pl.BlockSpec((1, d_ff, D_MODEL), lambda b, e: (e, 0, 0)),
                  vec, vec],
        out_specs=pl.BlockSpec(memory_space=pl.ANY),
        out_shape=jax.ShapeDtypeStruct((batch, seq, D_MODEL), F32),
        scratch_shapes=[pltpu.VMEM((seq, D_MODEL), F32),
                        pltpu.VMEM((cap, D_MODEL), F32), pltpu.VMEM((cap, D_MODEL), BF16),
                        pltpu.VMEM((cap, d_ff), BF16), pltpu.VMEM((cap, D_MODEL), F32),
                        pltpu.VMEM((seq, D_MODEL), F32), pltpu.VMEM((2, LN2_TM, D_MODEL), F32),
                        pltpu.SemaphoreType.DMA((seq // LN2_TM,)), pltpu.SemaphoreType.DMA((2,))],
        compiler_params=_cparams(("arbitrary", "arbitrary"), FFN_VMEM_LIMIT),
        name="ec_ffn",
    )(flat(idx), flat(idx_next), flat(idx_prev), flat(gate), flat(gate_prev), x1, wgu, wd, g2, b2)


def kernel(x, mem, w_in, b_gate, na_rpb, gqa_q_gain, gqa_k_gain, w_mem_kv, w_branch, w_out,
           ln1_g, ln1_b, w_router, w_gate_up, w_down, ln2_g, ln2_b):
    batch, seq, d = x.shape
    mem_tokens = mem.shape[1]
    depth = w_in.shape[0]
    alpha = (2 * depth) ** 0.25
    cap = EC_CAPACITY_FACTOR * seq // N_EXPERTS
    assert d == D_MODEL and seq == GRID_W * GRID_W and w_in.shape[2] == QKV_W + N_BRANCH * D_MODEL
    row = lambda v: v.reshape(1, -1)
    for l in range(depth):
        x2 = x.reshape(batch * seq, d)
        w_qkv = w_in[l, :, :QKV_W].astype(BF16)
        w_gate = w_in[l, :, QKV_W:].astype(BF16)
        qna, kna, vna, qg, kg, vg, qm = _in_proj(
            x2, w_qkv, row(jnp.tile(gqa_q_gain[l], GQA_HEADS)), row(jnp.tile(gqa_k_gain[l], GQA_KV_HEADS)),
            batch, seq)
        mk, mv = _mem_kv(mem.reshape(batch * mem_tokens, d), w_mem_kv[l].astype(BF16))
        yna = _na_attention(qna, kna, vna, *_na_bias_tiles(na_rpb[l]), batch, seq)
        ygqa, wgu_bf, wd_bf = _gqa_attention(qg, kg, vg, w_gate_up[l], w_down[l], batch, seq)
        x1, aff = _merge(x2, yna, ygqa, qm, mk, mv, w_gate, row(b_gate[l]), w_branch[l].astype(BF16),
                              w_out[l].astype(BF16), row(ln1_g[l]), row(ln1_b[l]), w_router[l].T, seq,
                              mem_tokens, alpha)
        idx, gate = _ec_select(aff, batch, seq, cap)
        x = _ec_ffn(idx, gate, x1.reshape(batch, seq, d), wgu_bf, wd_bf, row(ln2_g[l]), row(ln2_b[l]),
                    batch, seq, cap, alpha)
    return x
```

```python
import functools
import math

import jax
import jax.numpy as jnp
import numpy as np
from jax import lax
from jax.experimental import pallas as pl
from jax.experimental.pallas import tpu as pltpu

F32 = jnp.float32
BF16 = jnp.bfloat16

D_MODEL = 1024
GRID_W = 64
HEAD_DIM = 64
LANES = 128
NA_HEADS = 8
NA_KH = 8
NA_KW = 16
NA_W = NA_HEADS * HEAD_DIM
NA_QROWS = 4
NA_KROWS = 12
GQA_HEADS = 8
GQA_KV_HEADS = 2
GQA_GROUP = GQA_HEADS // GQA_KV_HEADS
GQA_Q_W = GQA_HEADS * HEAD_DIM
GQA_KV_W = GQA_KV_HEADS * HEAD_DIM
ROPE_THETA = 10000.0
MEM_HEADS = 4
MEM_HEAD_DIM = 128
MEM_W = MEM_HEADS * MEM_HEAD_DIM
N_BRANCH = 3
N_EXPERTS = 16
EC_CAPACITY_FACTOR = 2
LN_EPS = 1e-5
RMS_EPS = 1e-6
NEG_INF = -1e30
QKV_W = 3 * NA_W + GQA_Q_W + 2 * GQA_KV_W + MEM_W

VMEM_LIMIT = 56 * 1024 * 1024
FFN_VMEM_LIMIT = 60 * 1024 * 1024


def _cparams(sem, vmem=VMEM_LIMIT):
    return pltpu.CompilerParams(dimension_semantics=sem, vmem_limit_bytes=vmem)


def _dot(a, b):
    return jnp.dot(a, b, preferred_element_type=F32)


def _dot_nt(a, b):
    return lax.dot_general(a, b, (((1,), (1,)), ((), ())), preferred_element_type=F32)


def _layer_norm(v, g, b):
    mu = jnp.mean(v, axis=-1, keepdims=True)
    var = jnp.mean(jnp.square(v - mu), axis=-1, keepdims=True)
    return (v - mu) * lax.rsqrt(var + LN_EPS) * g + b


def _na_bias_kernel(rpb_ref, out_ref):
    n = GRID_W * GRID_W
    col = lax.broadcasted_iota(jnp.int32, (32, n), 1)
    d = lax.broadcasted_iota(jnp.int32, (32, n), 0)
    c, kc = col // GRID_W, col % GRID_W
    onehot = (kc - c + (NA_KW - 1) == d).astype(F32)
    t = jnp.dot(rpb_ref[...], onehot, precision=lax.Precision.HIGHEST, preferred_element_type=F32)
    col1 = lax.broadcasted_iota(jnp.int32, (1, n), 1)
    c1, kc1 = col1 // GRID_W, col1 % GRID_W
    start = jnp.clip(c1 - NA_KW // 2, 0, GRID_W - NA_KW)
    in_win = (kc1 >= start) & (kc1 < start + NA_KW)
    out_ref[...] = jnp.where(in_win, t * math.log2(math.e), NEG_INF)


def _na_bias_tiles(rpb):
    h = rpb.shape[0]
    nd = 2 * NA_KH - 1
    rpb2 = jnp.pad(rpb.reshape(h * nd, 2 * NA_KW - 1), ((0, 0), (0, 1)))
    t = pl.pallas_call(
        _na_bias_kernel,
        out_shape=jax.ShapeDtypeStruct((h * nd, GRID_W * GRID_W), F32),
        name="na_bias",
    )(rpb2).reshape(h, nd, GRID_W, GRID_W)
    t = jnp.concatenate([t, jnp.full((h, 1, GRID_W, GRID_W), NEG_INF, F32)], axis=1)
    which = np.full((3, NA_QROWS, NA_KROWS), nd, np.int32)
    for variant in range(3):
        for i in range(NA_QROWS):
            for j in range(NA_KROWS):
                if variant == 0:
                    dr, ok = j - i, j < NA_KH
                elif variant == 1:
                    dr = j - i - NA_KH // 2
                    ok = -(NA_KH // 2) <= dr < NA_KH // 2
                else:
                    dr, ok = j - i - NA_KH, j >= NA_KROWS - NA_KH
                if ok:
                    which[variant, i, j] = dr + NA_KH - 1
    return t, which


def _fill_na_tiles(t_ref, bias_sc, which):
    for variant in range(which.shape[0]):
        for hh in range(2):
            for i in range(NA_QROWS):
                for j in range(0, NA_KROWS, 2):
                    pair = jnp.concatenate([t_ref[hh, int(which[variant, i, j])],
                                            t_ref[hh, int(which[variant, i, j + 1])]], axis=-1)
                    bias_sc[variant, hh, i * GRID_W:(i + 1) * GRID_W, j * GRID_W:(j + 2) * GRID_W] = pair


PROJ_TM = 1024


def _rope_tables(seq_len):
    t = jnp.arange(seq_len)
    row = (t // GRID_W).astype(F32)
    col = (t % GRID_W).astype(F32)
    half = HEAD_DIM // 2
    inv = ROPE_THETA ** (-jnp.arange(0, half, 2, dtype=F32) / half)
    ang = jnp.concatenate([row[:, None] * inv, col[:, None] * inv], axis=-1)
    cos, sin = jnp.cos(ang), jnp.sin(ang)
    cos_l = jnp.repeat(cos, 2, axis=-1)
    sin_l = jnp.stack([-sin, sin], axis=-1).reshape(seq_len, HEAD_DIM)
    return jnp.tile(cos_l, (1, 2)), jnp.tile(sin_l, (1, 2))


def _head_sums(sq, width):
    r = lax.broadcasted_iota(jnp.int32, (width, width), 0) // HEAD_DIM
    c = lax.broadcasted_iota(jnp.int32, (width, width), 1) // HEAD_DIM
    return _dot(sq.astype(BF16), (r == c).astype(BF16))


def _rope(v, cos, sin_signed):
    lane = lax.broadcasted_iota(jnp.int32, v.shape, 1)
    swapped = jnp.where(lane % 2 == 0, pltpu.roll(v, LANES - 1, 1), pltpu.roll(v, 1, 1))
    return v * cos + swapped * sin_signed


def _proj_kernel(x_ref, w_ref, qgain_ref, kgain_ref, cos_ref, sin_ref,
                 qna_ref, kna_ref, vna_ref, qg_ref, kg_ref, vg_ref, qm_ref):
    xb = x_ref[...].astype(BF16)
    qscale = math.log2(math.e) / math.sqrt(HEAD_DIM)
    o = 0
    qna_ref[...] = (_dot(xb, w_ref[:, o:o + NA_W]) * qscale).astype(BF16)
    o += NA_W
    kna_ref[...] = _dot(xb, w_ref[:, o:o + NA_W]).astype(BF16)
    o += NA_W
    vna_ref[...] = _dot(xb, w_ref[:, o:o + NA_W]).astype(BF16)
    o += NA_W
    cos, sin = cos_ref[...], sin_ref[...]
    lane = lax.broadcasted_iota(jnp.int32, (PROJ_TM, LANES), 1)
    low = lane < HEAD_DIM

    zq = _dot(xb, w_ref[:, o:o + GQA_Q_W])
    o += GQA_Q_W
    ms = _head_sums(zq * zq, GQA_Q_W) * (1.0 / HEAD_DIM)
    zq = zq * lax.rsqrt(ms + RMS_EPS) * qgain_ref[...]
    for p in range(GQA_HEADS // 2):
        r = _rope(zq[:, p * LANES:(p + 1) * LANES], cos, sin) * qscale
        rolled = pltpu.roll(r, HEAD_DIM, 1)
        if (2 * p) // GQA_GROUP == 0:
            even, odd = jnp.where(low, r, 0.0), jnp.where(low, rolled, 0.0)
        else:
            even, odd = jnp.where(low, 0.0, rolled), jnp.where(low, 0.0, r)
        qg_ref[0, 2 * p] = even.astype(BF16)
        qg_ref[0, 2 * p + 1] = odd.astype(BF16)

    zk = _dot(xb, w_ref[:, o:o + GQA_KV_W])
    o += GQA_KV_W
    ms = _head_sums(zk * zk, GQA_KV_W) * (1.0 / HEAD_DIM)
    zk = zk * lax.rsqrt(ms + RMS_EPS) * kgain_ref[...]
    kg_ref[...] = _rope(zk, cos, sin).astype(BF16)
    zv = _dot(xb, w_ref[:, o:o + GQA_KV_W])
    vg_ref[0] = jnp.where(low, zv, 1.0).astype(BF16)
    vg_ref[1] = jnp.where(low, 1.0, zv).astype(BF16)
    o += GQA_KV_W
    qm_ref[...] = _dot(xb, w_ref[:, o:o + MEM_W]).astype(BF16)


def _in_proj(x2, w_qkv, qgain, kgain, batch, seq):
    tokens = x2.shape[0]
    tm = PROJ_TM
    spb = seq // tm
    cos, sin = _rope_tables(seq)
    row = lambda w: pl.BlockSpec((tm, w), lambda i: (i, 0))
    const = lambda shape: pl.BlockSpec(shape, lambda i: (0,) * len(shape), pipeline_mode=pl.Buffered(1))
    tbl = pl.BlockSpec((tm, LANES), lambda i: (i % spb, 0))
    sds = lambda w: jax.ShapeDtypeStruct((tokens, w), BF16)
    return pl.pallas_call(
        _proj_kernel,
        grid=(tokens // tm,),
        in_specs=[row(D_MODEL), const((D_MODEL, QKV_W)), const((1, GQA_Q_W)), const((1, GQA_KV_W)), tbl, tbl],
        out_specs=[row(NA_W), row(NA_W), row(NA_W),
                   pl.BlockSpec((1, GQA_HEADS, tm, LANES), lambda i: (i // spb, 0, i % spb, 0)),
                   row(GQA_KV_W), pl.BlockSpec((GQA_KV_HEADS, tm, LANES), lambda i: (0, i, 0)), row(MEM_W)],
        out_shape=[sds(NA_W), sds(NA_W), sds(NA_W),
                   jax.ShapeDtypeStruct((batch, GQA_HEADS, seq, LANES), BF16),
                   sds(GQA_KV_W), jax.ShapeDtypeStruct((GQA_KV_HEADS, tokens, LANES), BF16), sds(MEM_W)],
        compiler_params=_cparams(("parallel",)),
        name="in_proj",
    )(x2, w_qkv, qgain, kgain, cos, sin)


def _memkv_kernel(m_ref, w_ref, k_ref, v_ref):
    z = _dot(m_ref[...].astype(BF16), w_ref[...])
    k_ref[...] = z[:, :MEM_W].astype(BF16)
    v_ref[...] = z[:, MEM_W:].astype(BF16)


def _mem_kv(mem2, w_kv):
    rows = mem2.shape[0]
    tm = 256
    return pl.pallas_call(
        _memkv_kernel,
        grid=(rows // tm,),
        in_specs=[pl.BlockSpec((tm, D_MODEL), lambda i: (i, 0)),
                  pl.BlockSpec((D_MODEL, 2 * MEM_W), lambda i: (0, 0))],
        out_specs=[pl.BlockSpec((tm, MEM_W), lambda i: (i, 0))] * 2,
        out_shape=[jax.ShapeDtypeStruct((rows, MEM_W), BF16)] * 2,
        compiler_params=_cparams(("parallel",)),
        name="mem_kv",
    )(mem2, w_kv)


NA_TQ = NA_QROWS * GRID_W
NA_TK = NA_KROWS * GRID_W


NA_GROUPS_PER_STEP = 8


def _na_kernel(q_ref, k_ref, v_ref, t_ref, o_ref, bias_sc, *, n_groups, which):
    @pl.when((pl.program_id(1) == 0) & (pl.program_id(2) == 0))
    def _():
        _fill_na_tiles(t_ref, bias_sc, which)

    low = lax.broadcasted_iota(jnp.int32, (NA_TQ, LANES), 1) < HEAD_DIM
    tiles = NA_TK // LANES
    for sub in range(NA_GROUPS_PER_STEP):
        rg = pl.program_id(2) * NA_GROUPS_PER_STEP + sub
        key_row0 = jnp.clip(NA_QROWS * rg - NA_KH // 2, 0, GRID_W - NA_KROWS)
        start = pl.multiple_of(key_row0 * GRID_W, GRID_W)
        variant = jnp.where(rg == 0, 0, jnp.where(rg == n_groups - 1, 2, 1))
        k = k_ref[pl.ds(start, NA_TK), :]
        v = v_ref[pl.ds(start, NA_TK), :]
        q = q_ref[sub * NA_TQ:(sub + 1) * NA_TQ, :]
        qq = jnp.concatenate([jnp.where(low, q, jnp.zeros_like(q)), jnp.where(low, jnp.zeros_like(q), q)], axis=0)
        s = _dot_nt(qq, k) + bias_sc[variant].reshape(2 * NA_TQ, NA_TK)
        st = [s[:, j * LANES:(j + 1) * LANES] for j in range(tiles)]
        part = st[0]
        for t in st[1:]:
            part = jnp.maximum(part, t)
        m = jnp.broadcast_to(jnp.max(part, axis=-1, keepdims=True), part.shape)
        ps = [jnp.exp2(t - m) for t in st]
        lpart = ps[0]
        for t in ps[1:]:
            lpart = lpart + t
        l = jnp.broadcast_to(jnp.sum(lpart, axis=-1, keepdims=True), lpart.shape)
        out = _dot(jnp.concatenate([t.astype(BF16) for t in ps], axis=-1), v) / l
        o_ref[sub * NA_TQ:(sub + 1) * NA_TQ, :] = jnp.where(low, out[:NA_TQ], out[NA_TQ:]).astype(BF16)


def _na_attention(qna, kna, vna, tiles, which, batch, seq):
    n_groups = seq // NA_TQ
    steps = n_groups // NA_GROUPS_PER_STEP
    pairs = NA_HEADS // 2
    kv = pl.BlockSpec((seq, LANES), lambda p, b, g: (b, p))
    qo = pl.BlockSpec((NA_GROUPS_PER_STEP * NA_TQ, LANES), lambda p, b, g: (b * steps + g, p))
    return pl.pallas_call(
        functools.partial(_na_kernel, n_groups=n_groups, which=which),
        grid=(pairs, batch, steps),
        in_specs=[qo, kv, kv, pl.BlockSpec((2,) + tiles.shape[1:], lambda p, b, g: (p, 0, 0, 0))],
        out_specs=qo,
        out_shape=jax.ShapeDtypeStruct(qna.shape, BF16),
        scratch_shapes=[pltpu.VMEM((which.shape[0], 2, NA_TQ, NA_TK), F32)],
        compiler_params=_cparams(("arbitrary", "arbitrary", "arbitrary")),
        name="na_attn",
    )(qna, kna, vna, tiles)


GQA_TQ = 128
GQA_TK = 1024


def _gqa_kernel(q_ref, k_ref, v_ref, wgu_ref, wd_ref, o_ref, wgu_bf_ref, wd_bf_ref, s_sc, mx_sc, acc_sc, *, seq):
    wgu_bf_ref[...] = wgu_ref[...].astype(BF16)
    wd_bf_ref[...] = wd_ref[...].astype(BF16)
    rows = GQA_GROUP * GQA_TQ
    n_chunks = seq // GQA_TK
    tiles = GQA_TK // LANES
    groups = range(GQA_KV_HEADS)
    lane = lax.broadcasted_iota(jnp.int32, (GQA_TQ, LANES), 1)
    low = lane < HEAD_DIM
    qs = [q_ref[0, g * GQA_GROUP:(g + 1) * GQA_GROUP].reshape(rows, LANES) for g in groups]
    mx_sc[...] = jnp.full(mx_sc.shape, -jnp.inf, F32)

    for c in range(n_chunks):
        k = k_ref[c * GQA_TK:(c + 1) * GQA_TK, :]
        for g in groups:
            s = _dot_nt(qs[g], k)
            s_sc[g, c] = s
            part = s[:, :LANES]
            for j in range(1, tiles):
                part = jnp.maximum(part, s[:, j * LANES:(j + 1) * LANES])
            mx_sc[g] = jnp.maximum(mx_sc[g], part)
    ms = [jnp.broadcast_to(jnp.max(mx_sc[g], axis=-1, keepdims=True), (rows, LANES)) for g in groups]
    acc_sc[...] = jnp.zeros(acc_sc.shape, F32)

    for c in range(n_chunks):
        for g in groups:
            p = [jnp.exp2(s_sc[g, c, :, j * LANES:(j + 1) * LANES] - ms[g]).astype(BF16) for j in range(tiles)]
            acc_sc[g] += _dot(jnp.concatenate(p, axis=-1), v_ref[g, c * GQA_TK:(c + 1) * GQA_TK, :])
    for g in groups:
        acc = acc_sc[g]
        out = acc / pltpu.roll(acc, HEAD_DIM, 1)
        for pp in range(GQA_GROUP // 2):
            a = out[(2 * pp) * GQA_TQ:(2 * pp + 1) * GQA_TQ]
            b = out[(2 * pp + 1) * GQA_TQ:(2 * pp + 2) * GQA_TQ]
            if g == 0:
                pair = jnp.where(low, a, pltpu.roll(b, HEAD_DIM, 1))
            else:
                pair = jnp.where(low, pltpu.roll(a, HEAD_DIM, 1), b)
            col = (g * (GQA_GROUP // 2) + pp) * LANES
            o_ref[:, col:col + LANES] = pair.astype(BF16)


def _gqa_attention(qg, kg, vg, w_gate_up, w_down, batch, seq):
    nq = seq // GQA_TQ
    rows = GQA_GROUP * GQA_TQ
    stat = pltpu.VMEM((GQA_KV_HEADS, rows, LANES), F32)
    n_exp, w_rows = w_gate_up.shape[:2]
    slabs = batch * nq // n_exp
    slab = w_rows // slabs
    assert slabs * n_exp == batch * nq and slab * slabs == w_rows and w_down.shape[:2] == (n_exp, w_rows)
    wspec = lambda w: pl.BlockSpec((1, slab, w.shape[2]),
                                   lambda b, i: ((b * nq + i) // slabs, (b * nq + i) % slabs, 0))
    return pl.pallas_call(
        functools.partial(_gqa_kernel, seq=seq),
        grid=(batch, nq),
        in_specs=[pl.BlockSpec((1, GQA_HEADS, GQA_TQ, LANES), lambda b, i: (b, 0, i, 0)),
                  pl.BlockSpec((seq, LANES), lambda b, i: (b, 0)),
                  pl.BlockSpec((GQA_KV_HEADS, seq, LANES), lambda b, i: (0, b, 0)),
                  wspec(w_gate_up), wspec(w_down)],
        out_specs=[pl.BlockSpec((GQA_TQ, GQA_Q_W), lambda b, i: (b * nq + i, 0)), wspec(w_gate_up), wspec(w_down)],
        out_shape=[jax.ShapeDtypeStruct((batch * seq, GQA_Q_W), BF16),
                   jax.ShapeDtypeStruct(w_gate_up.shape, BF16), jax.ShapeDtypeStruct(w_down.shape, BF16)],
        scratch_shapes=[pltpu.VMEM((GQA_KV_HEADS, seq // GQA_TK, rows, GQA_TK), F32), stat, stat],
        compiler_params=_cparams(("parallel", "parallel")),
        name="gqa_attn",
    )(qg, kg, vg, w_gate_up, w_down)


MERGE_TM = 1024
MERGE_HALVES = 1


def _merge_kernel(x_ref, yna_ref, ygqa_ref, qm_ref, mk_ref, mv_ref, wg_ref, bg_ref, wb_ref, wo_ref,
                  g1_ref, b1_ref, wr_ref, x1_ref, aff_ref, *, alpha):
    rows_per = MERGE_TM // MERGE_HALVES
    mem_scale = 1.0 / math.sqrt(MEM_HEAD_DIM)
    w = wr_ref[...]
    w_hi = w.astype(BF16)
    w_hl = jnp.concatenate([w_hi, (w - w_hi.astype(F32)).astype(BF16)], axis=0)
    for part in range(MERGE_HALVES):
        rows = slice(part * rows_per, (part + 1) * rows_per)
        x = x_ref[rows, :]
        xb = x.astype(BF16)
        ymem = []
        for h in range(MEM_HEADS):
            sl = slice(h * MEM_HEAD_DIM, (h + 1) * MEM_HEAD_DIM)
            s = _dot_nt(qm_ref[rows, sl], mk_ref[:, sl]) * mem_scale
            m = jnp.max(s, axis=-1, keepdims=True)
            p = jnp.exp(s - m)
            l = jnp.sum(p, axis=-1, keepdims=True)
            ymem.append((_dot(p.astype(BF16), mv_ref[:, sl]) / l).astype(BF16))
        branches = (yna_ref[rows, :], ygqa_ref[rows, :], jnp.concatenate(ymem, axis=-1))
        merged = jnp.zeros((rows_per, D_MODEL), F32)
        for g in range(N_BRANCH):
            cols = slice(g * D_MODEL, (g + 1) * D_MODEL)
            gate = jax.nn.sigmoid(_dot(xb, wg_ref[:, cols]) + bg_ref[:, cols])
            merged = merged + gate * _dot(branches[g], wb_ref[g])
        mix = _dot(merged.astype(BF16), wo_ref[...])
        x1 = _layer_norm(alpha * x + mix, g1_ref[...], b1_ref[...])
        x1_ref[rows, :] = x1
        x1_hi = x1.astype(BF16)
        x1_lo = (x1 - x1_hi.astype(F32)).astype(BF16)
        by_hi = _dot_nt(w_hl, x1_hi)
        logits = by_hi[:N_EXPERTS] + by_hi[N_EXPERTS:] + _dot_nt(w_hi, x1_lo)
        m = jnp.max(logits, axis=0, keepdims=True)
        e = jnp.exp(logits - m)
        aff_ref[0, :, rows] = e / jnp.sum(e, axis=0, keepdims=True)


def _merge(x2, yna, ygqa, qm, mk, mv, w_gate, b_gate, w_branch, w_out, g1, b1, w_router, seq, mem_tokens, alpha):
    tokens = x2.shape[0]
    tm = MERGE_TM
    tpb = seq // tm
    row = lambda w: pl.BlockSpec((tm, w), lambda i: (i, 0))
    const = lambda shape: pl.BlockSpec(shape, lambda i: (0,) * len(shape), pipeline_mode=pl.Buffered(1))
    memb = pl.BlockSpec((mem_tokens, MEM_W), lambda i: (i // tpb, 0))
    return pl.pallas_call(
        functools.partial(_merge_kernel, alpha=alpha),
        grid=(tokens // tm,),
        in_specs=[row(D_MODEL), row(NA_W), row(GQA_Q_W), row(MEM_W), memb, memb,
                  const(w_gate.shape), const(b_gate.shape), const(w_branch.shape), const(w_out.shape),
                  const(g1.shape), const(b1.shape), const(w_router.shape)],
        out_specs=[row(D_MODEL), pl.BlockSpec((1, N_EXPERTS, tm), lambda i: (i // tpb, 0, i % tpb))],
        out_shape=[jax.ShapeDtypeStruct((tokens, D_MODEL), F32),
                   jax.ShapeDtypeStruct((tokens // seq, N_EXPERTS, seq), F32)],
        compiler_params=_cparams(("parallel",)),
        name="merge",
    )(x2, yna, ygqa, qm, mk, mv, w_gate, b_gate, w_branch, w_out, g1, b1, w_router)


SEL_BLK = 256
SLOT_HI = 16
SLOT_LO = 32
SEL_PARTS = 5


def _prefix_counts(mask, out_ref, seq):
    r = lax.broadcasted_iota(jnp.int32, (SEL_BLK, SEL_BLK), 0)
    c = lax.broadcasted_iota(jnp.int32, (SEL_BLK, SEL_BLK), 1)
    tri = (r <= c).astype(BF16)
    run = jnp.zeros((mask.shape[0], 1), F32)
    for j in range(seq // SEL_BLK):
        cs = _dot(mask[:, j * SEL_BLK:(j + 1) * SEL_BLK].astype(BF16), tri) + run
        out_ref[:, j * SEL_BLK:(j + 1) * SEL_BLK] = cs
        run = cs[:, SEL_BLK - 1:SEL_BLK]


def _select_kernel(aff_ref, idx_ref, gate_ref, cnt_sc, lhs_sc, rhs_sc, *, seq, cap):
    a = aff_ref[0]
    thr_bits = jnp.zeros((N_EXPERTS, 1), jnp.int32)
    for bit in range(30, -1, -1):
        cand = thr_bits | (1 << bit)
        cnt = jnp.sum((a >= pltpu.bitcast(cand, F32)).astype(F32), axis=-1, keepdims=True)
        thr_bits = jnp.where(cnt >= cap, cand, thr_bits)
    thr = pltpu.bitcast(thr_bits, F32)
    gt = a > thr
    eq = a == thr
    need = cap - jnp.sum(gt.astype(F32), axis=-1, keepdims=True)
    _prefix_counts(eq.astype(F32), cnt_sc, seq)
    sel = gt | (eq & (cnt_sc[...] <= need))
    _prefix_counts(sel.astype(F32), cnt_sc, seq)

    slot = cnt_sc[...] - 1.0
    s_hi = jnp.floor(slot * (1.0 / SLOT_LO))
    s_lo = slot - SLOT_LO * s_hi
    tok = lax.broadcasted_iota(jnp.int32, (1, seq), 1).astype(F32)
    t_hi = jnp.floor(tok * (1.0 / 64.0))
    t_lo = tok - 64.0 * t_hi
    ph = lax.broadcasted_iota(jnp.int32, (SLOT_HI, seq), 0).astype(F32)
    pl_ = lax.broadcasted_iota(jnp.int32, (SLOT_LO, seq), 0).astype(F32)
    for e in range(N_EXPERTS):
        row = slice(e, e + 1)
        onehot_hi = jnp.where((s_hi[row] == ph) & sel[row], 1.0, 0.0)
        g = a[row]
        g1 = g.astype(BF16).astype(F32)
        g2 = (g - g1).astype(BF16).astype(F32)
        g3 = g - g1 - g2
        for part, val in enumerate((t_hi, t_lo, g1, g2, g3)):
            r0 = (part * N_EXPERTS + e) * SLOT_HI
            lhs_sc[r0:r0 + SLOT_HI, :] = (onehot_hi * val).astype(BF16)
        rhs_sc[e * SLOT_LO:(e + 1) * SLOT_LO, :] = jnp.where(s_lo[row] == pl_, 1.0, 0.0).astype(BF16)
    res = _dot_nt(lhs_sc[...], rhs_sc[...])
    n = N_EXPERTS * SLOT_HI
    r = lax.broadcasted_iota(jnp.int32, (n, N_EXPERTS * SLOT_LO), 0) // SLOT_HI
    c = lax.broadcasted_iota(jnp.int32, (n, N_EXPERTS * SLOT_LO), 1) // SLOT_LO
    parts = []
    for part in range(SEL_PARTS):
        blk = jnp.where(r == c, res[part * n:(part + 1) * n], 0.0)
        parts.append(jnp.sum(blk.reshape(N_EXPERTS, SLOT_HI, N_EXPERTS * SLOT_LO), axis=0))
    idx_ref[0] = (64.0 * parts[0] + parts[1]).astype(jnp.int32)
    gate_ref[0] = (parts[2] + parts[3]) + parts[4]


def _ec_select(aff, batch, seq, cap):
    assert cap == SLOT_HI * SLOT_LO
    out = pl.BlockSpec((1, SLOT_HI, N_EXPERTS * SLOT_LO), lambda b: (b, 0, 0))
    idx_t, gate_t = pl.pallas_call(
        functools.partial(_select_kernel, seq=seq, cap=cap),
        grid=(batch,),
        in_specs=[pl.BlockSpec((1, N_EXPERTS, seq), lambda b: (b, 0, 0))],
        out_specs=[out, out],
        out_shape=[jax.ShapeDtypeStruct((batch, SLOT_HI, N_EXPERTS * SLOT_LO), jnp.int32),
                   jax.ShapeDtypeStruct((batch, SLOT_HI, N_EXPERTS * SLOT_LO), F32)],
        scratch_shapes=[pltpu.VMEM((N_EXPERTS, seq), F32),
                        pltpu.VMEM((SEL_PARTS * N_EXPERTS * SLOT_HI, seq), BF16),
                        pltpu.VMEM((N_EXPERTS * SLOT_LO, seq), BF16)],
        compiler_params=_cparams(("parallel",)),
        name="ec_select",
    )(aff)
    order = lambda v: v.reshape(batch, SLOT_HI, N_EXPERTS, SLOT_LO).transpose(0, 2, 1, 3).reshape(batch, N_EXPERTS, cap)
    return order(idx_t), order(gate_t)


FFN_COL_PHASES = 1
LN2_TM = 256


def _gather_rows(idx_ref, src, dst, lo, hi):
    for p in range(lo, hi):
        dst[pl.ds(p, 1), :] = src[pl.ds(idx_ref[0, 0, p], 1), :]


def _scatter_rows(idx_ref, gate_ref, src, acc_sc, lo, hi):
    for p0 in range(lo, hi, 8):
        ps = range(p0, p0 + 8)
        ts = [idx_ref[0, 0, p] for p in ps]
        rows = [acc_sc[pl.ds(t, 1), :] + src[pl.ds(p, 1), :] * gate_ref[0, 0, p] for p, t in zip(ps, ts)]
        for t, row in zip(ts, rows):
            acc_sc[pl.ds(t, 1), :] = row


def _ffn_kernel(idx_ref, idx_next_ref, idx_prev_ref, gate_ref, gate_prev_ref, x_hbm, wgu_ref, wd_ref,
                g2_ref, b2_ref, out_hbm, x_sc, u_sc, xs_sc, act_sc, ye_sc, acc_sc, out_sc, sem_x, sem_out,
                *, cap, d_ff, seq, batch, alpha):
    b = pl.program_id(0)
    e = pl.program_id(1)
    n_tiles = seq // LN2_TM

    def x_copy(seq_id, i):
        rows = pl.ds(i * LN2_TM, LN2_TM)
        return pltpu.make_async_copy(x_hbm.at[seq_id, rows], x_sc.at[rows], sem_x.at[i])

    @pl.when((e == 0) & (b == 0))
    def _():
        for i in range(n_tiles):
            x_copy(0, i).start()
        ye_sc[...] = jnp.zeros(ye_sc.shape, F32)

    @pl.when(e == 0)
    def _():
        for i in range(n_tiles):
            x_copy(b, i).wait()
        acc_sc[...] = jnp.zeros(acc_sc.shape, F32)
        _gather_rows(idx_ref, x_sc, u_sc, 0, cap)

    xs_sc[...] = u_sc[...].astype(BF16)

    width = d_ff // FFN_COL_PHASES
    out_w = D_MODEL // FFN_COL_PHASES
    share = cap // FFN_COL_PHASES
    for ph in range(2 * FFN_COL_PHASES):
        @pl.when(e + ph >= 0)
        def _(ph=ph):
            if ph < FFN_COL_PHASES:
                _scatter_rows(idx_prev_ref, gate_prev_ref, ye_sc, acc_sc, ph * share, (ph + 1) * share)
                xs = xs_sc[...]
                hg = _dot(xs, wgu_ref[0, :, ph * width:(ph + 1) * width])
                hu = _dot(xs, wgu_ref[0, :, d_ff + ph * width:d_ff + (ph + 1) * width])
                act_sc[:, ph * width:(ph + 1) * width] = (hg * jax.nn.sigmoid(hg) * hu).astype(BF16)
            else:
                c = ph - FFN_COL_PHASES
                _gather_rows(idx_next_ref, x_sc, u_sc, c * share, (c + 1) * share)
                ye_sc[:, c * out_w:(c + 1) * out_w] = _dot(act_sc[...], wd_ref[0, :, c * out_w:(c + 1) * out_w])

    @pl.when(e == N_EXPERTS - 1)
    def _():
        def o_copy(i, slot):
            return pltpu.make_async_copy(out_sc.at[slot], out_hbm.at[b, pl.ds(i * LN2_TM, LN2_TM)], sem_out.at[slot])

        _scatter_rows(idx_ref, gate_ref, ye_sc, acc_sc, 0, cap)
        more = b + 1 < batch
        for i in range(n_tiles):
            slot = i % 2
            rows = slice(i * LN2_TM, (i + 1) * LN2_TM)
            if i >= 2:
                o_copy(i - 2, slot).wait()
            out_sc[slot] = _layer_norm(alpha * x_sc[rows, :] + acc_sc[rows, :], g2_ref[...], b2_ref[...])
            o_copy(i, slot).start()

            @pl.when(more)
            def _(i=i):
                x_copy(b + 1, i).start()
        o_copy(n_tiles - 2, n_tiles % 2).wait()
        o_copy(n_tiles - 1, (n_tiles - 1) % 2).wait()


def _ec_ffn(idx, gate, x1, wgu, wd, g2, b2, batch, seq, cap, alpha):
    d_ff = wd.shape[1]
    slot = pl.BlockSpec((1, 1, cap), lambda b, e: (b * N_EXPERTS + e, 0, 0), memory_space=pltpu.SMEM)
    vec = pl.BlockSpec((1, D_MODEL), lambda b, e: (0, 0))
    flat = lambda v: v.reshape(batch * N_EXPERTS, 1, cap)
    idx_next = jnp.concatenate([idx[:, 1:], idx[:, -1:]], axis=1)
    idx_prev = jnp.concatenate([idx[:, :1], idx[:, :-1]], axis=1)
    gate_prev = jnp.concatenate([jnp.zeros_like(gate[:, :1]), gate[:, :-1]], axis=1)
    return pl.pallas_call(
        functools.partial(_ffn_kernel, cap=cap, d_ff=d_ff, seq=seq, batch=batch, alpha=alpha),
        grid=(batch, N_EXPERTS),
        in_specs=[slot, slot, slot, slot, slot,
                  pl.BlockSpec(memory_space=pl.ANY),
                  pl.BlockSpec((1, D_MODEL, 2 * d_ff), lambda b, e: (e, 0, 0)),
                  pl.BlockSpec((1, d_ff, D_MODEL), lambda b, e: (e, 0, 0)),
                  vec, vec],
        out_specs=pl.BlockSpec(memory_space=pl.ANY),
        out_shape=jax.ShapeDtypeStruct((batch, seq, D_MODEL), F32),
        scratch_shapes=[pltpu.VMEM((seq, D_MODEL), F32),
                        pltpu.VMEM((cap, D_MODEL), F32), pltpu.VMEM((cap, D_MODEL), BF16),
                        pltpu.VMEM((cap, d_ff), BF16), pltpu.VMEM((cap, D_MODEL), F32),
                        pltpu.VMEM((seq, D_MODEL), F32), pltpu.VMEM((2, LN2_TM, D_MODEL), F32),
                        pltpu.SemaphoreType.DMA((seq // LN2_TM,)), pltpu.SemaphoreType.DMA((2,))],
        compiler_params=_cparams(("arbitrary", "arbitrary"), FFN_VMEM_LIMIT),
        name="ec_ffn",
    )(flat(idx), flat(idx_next), flat(idx_prev), flat(gate), flat(gate_prev), x1, wgu, wd, g2, b2)


def kernel(x, mem, w_in, b_gate, na_rpb, gqa_q_gain, gqa_k_gain, w_mem_kv, w_branch, w_out,
           ln1_g, ln1_b, w_router, w_gate_up, w_down, ln2_g, ln2_b):
    batch, seq, d = x.shape
    mem_tokens = mem.shape[1]
    depth = w_in.shape[0]
    alpha = (2 * depth) ** 0.25
    cap = EC_CAPACITY_FACTOR * seq // N_EXPERTS
    assert d == D_MODEL and seq == GRID_W * GRID_W and w_in.shape[2] == QKV_W + N_BRANCH * D_MODEL
    row = lambda v: v.reshape(1, -1)
    for l in range(depth):
        x2 = x.reshape(batch * seq, d)
        w_qkv = w_in[l, :, :QKV_W].astype(BF16)
        w_gate = w_in[l, :, QKV_W:].astype(BF16)
        qna, kna, vna, qg, kg, vg, qm = _in_proj(
            x2, w_qkv, row(jnp.tile(gqa_q_gain[l], GQA_HEADS)), row(jnp.tile(gqa_k_gain[l], GQA_KV_HEADS)),
            batch, seq)
        mk, mv = _mem_kv(mem.reshape(batch * mem_tokens, d), w_mem_kv[l].astype(BF16))
        yna = _na_attention(qna, kna, vna, *_na_bias_tiles(na_rpb[l]), batch, seq)
        ygqa, wgu_bf, wd_bf = _gqa_attention(qg, kg, vg, w_gate_up[l], w_down[l], batch, seq)
        x1, aff = _merge(x2, yna, ygqa, qm, mk, mv, w_gate, row(b_gate[l]), w_branch[l].astype(BF16),
                              w_out[l].astype(BF16), row(ln1_g[l]), row(ln1_b[l]), w_router[l].T, seq,
                              mem_tokens, alpha)
        idx, gate = _ec_select(aff, batch, seq, cap)
        x = _ec_ffn(idx, gate, x1.reshape(batch, seq, d), wgu_bf, wd_bf, row(ln2_g[l]), row(ln2_b[l]),
                    batch, seq, cap, alpha)
    return x
```

```python
import functools
import math

import jax
import jax.numpy as jnp
import numpy as np
from jax import lax
from jax.experimental import pallas as pl
from jax.experimental.pallas import tpu as pltpu

F32 = jnp.float32
BF16 = jnp.bfloat16

D_MODEL = 1024
GRID_W = 64
HEAD_DIM = 64
LANES = 128
NA_HEADS = 8
NA_KH = 8
NA_KW = 16
NA_W = NA_HEADS * HEAD_DIM
NA_QROWS = 4
NA_KROWS = 12
GQA_HEADS = 8
GQA_KV_HEADS = 2
GQA_GROUP = GQA_HEADS // GQA_KV_HEADS
GQA_Q_W = GQA_HEADS * HEAD_DIM
GQA_KV_W = GQA_KV_HEADS * HEAD_DIM
ROPE_THETA = 10000.0
MEM_HEADS = 4
MEM_HEAD_DIM = 128
MEM_W = MEM_HEADS * MEM_HEAD_DIM
N_BRANCH = 3
N_EXPERTS = 16
EC_CAPACITY_FACTOR = 2
LN_EPS = 1e-5
RMS_EPS = 1e-6
NEG_INF = -1e30
QKV_W = 3 * NA_W + GQA_Q_W + 2 * GQA_KV_W + MEM_W

VMEM_LIMIT = 56 * 1024 * 1024
FFN_VMEM_LIMIT = 60 * 1024 * 1024


def _cparams(sem, vmem=VMEM_LIMIT):
    return pltpu.CompilerParams(dimension_semantics=sem, vmem_limit_bytes=vmem)


def _dot(a, b):
    return jnp.dot(a, b, preferred_element_type=F32)


def _dot_nt(a, b):
    return lax.dot_general(a, b, (((1,), (1,)), ((), ())), preferred_element_type=F32)


def _layer_norm(v, g, b):
    mu = jnp.mean(v, axis=-1, keepdims=True)
    var = jnp.mean(jnp.square(v - mu), axis=-1, keepdims=True)
    return (v - mu) * lax.rsqrt(var + LN_EPS) * g + b


def _na_bias_kernel(rpb_ref, out_ref):
    n = GRID_W * GRID_W
    col = lax.broadcasted_iota(jnp.int32, (32, n), 1)
    d = lax.broadcasted_iota(jnp.int32, (32, n), 0)
    c, kc = col // GRID_W, col % GRID_W
    onehot = (kc - c + (NA_KW - 1) == d).astype(F32)
    t = jnp.dot(rpb_ref[...], onehot, precision=lax.Precision.HIGHEST, preferred_element_type=F32)
    col1 = lax.broadcasted_iota(jnp.int32, (1, n), 1)
    c1, kc1 = col1 // GRID_W, col1 % GRID_W
    start = jnp.clip(c1 - NA_KW // 2, 0, GRID_W - NA_KW)
    in_win = (kc1 >= start) & (kc1 < start + NA_KW)
    out_ref[...] = jnp.where(in_win, t * math.log2(math.e), NEG_INF)


def _na_bias_tiles(rpb):
    h = rpb.shape[0]
    nd = 2 * NA_KH - 1
    rpb2 = jnp.pad(rpb.reshape(h * nd, 2 * NA_KW - 1), ((0, 0), (0, 1)))
    t = pl.pallas_call(
        _na_bias_kernel,
        out_shape=jax.ShapeDtypeStruct((h * nd, GRID_W * GRID_W), F32),
        name="na_bias",
    )(rpb2).reshape(h, nd, GRID_W, GRID_W)
    t = jnp.concatenate([t, jnp.full((h, 1, GRID_W, GRID_W), NEG_INF, F32)], axis=1)
    which = np.full((3, NA_QROWS, NA_KROWS), nd, np.int32)
    for variant in range(3):
        for i in range(NA_QROWS):
            for j in range(NA_KROWS):
                if variant == 0:
                    dr, ok = j - i, j < NA_KH
                elif variant == 1:
                    dr = j - i - NA_KH // 2
                    ok = -(NA_KH // 2) <= dr < NA_KH // 2
                else:
                    dr, ok = j - i - NA_KH, j >= NA_KROWS - NA_KH
                if ok:
                    which[variant, i, j] = dr + NA_KH - 1
    return t, which


def _fill_na_tiles(t_ref, bias_sc, which):
    for variant in range(which.shape[0]):
        for hh in range(2):
            for i in range(NA_QROWS):
                for j in range(0, NA_KROWS, 2):
                    pair = jnp.concatenate([t_ref[hh, int(which[variant, i, j])],
                                            t_ref[hh, int(which[variant, i, j + 1])]], axis=-1)
                    bias_sc[variant, hh, i * GRID_W:(i + 1) * GRID_W, j * GRID_W:(j + 2) * GRID_W] = pair


PROJ_TM = 1024


def _rope_tables(seq_len):
    t = jnp.arange(seq_len)
    row = (t // GRID_W).astype(F32)
    col = (t % GRID_W).astype(F32)
    half = HEAD_DIM // 2
    inv = ROPE_THETA ** (-jnp.arange(0, half, 2, dtype=F32) / half)
    ang = jnp.concatenate([row[:, None] * inv, col[:, None] * inv], axis=-1)
    cos, sin = jnp.cos(ang), jnp.sin(ang)
    cos_l = jnp.repeat(cos, 2, axis=-1)
    sin_l = jnp.stack([-sin, sin], axis=-1).reshape(seq_len, HEAD_DIM)
    return jnp.tile(cos_l, (1, 2)), jnp.tile(sin_l, (1, 2))


def _head_sums(sq, width):
    r = lax.broadcasted_iota(jnp.int32, (width, width), 0) // HEAD_DIM
    c = lax.broadcasted_iota(jnp.int32, (width, width), 1) // HEAD_DIM
    return _dot(sq.astype(BF16), (r == c).astype(BF16))


def _rope(v, cos, sin_signed):
    lane = lax.broadcasted_iota(jnp.int32, v.shape, 1)
    swapped = jnp.where(lane % 2 == 0, pltpu.roll(v, LANES - 1, 1), pltpu.roll(v, 1, 1))
    return v * cos + swapped * sin_signed


def _proj_kernel(x_ref, w_ref, qgain_ref, kgain_ref, cos_ref, sin_ref,
                 qna_ref, kna_ref, vna_ref, qg_ref, kg_ref, vg_ref, qm_ref):
    xb = x_ref[...].astype(BF16)
    qscale = math.log2(math.e) / math.sqrt(HEAD_DIM)
    o = 0
    qna_ref[...] = (_dot(xb, w_ref[:, o:o + NA_W]) * qscale).astype(BF16)
    o += NA_W
    kna_ref[...] = _dot(xb, w_ref[:, o:o + NA_W]).astype(BF16)
    o += NA_W
    vna_ref[...] = _dot(xb, w_ref[:, o:o + NA_W]).astype(BF16)
    o += NA_W
    cos, sin = cos_ref[...], sin_ref[...]
    lane = lax.broadcasted_iota(jnp.int32, (PROJ_TM, LANES), 1)
    low = lane < HEAD_DIM

    zq = _dot(xb, w_ref[:, o:o + GQA_Q_W])
    o += GQA_Q_W
    ms = _head_sums(zq * zq, GQA_Q_W) * (1.0 / HEAD_DIM)
    zq = zq * lax.rsqrt(ms + RMS_EPS) * qgain_ref[...]
    for p in range(GQA_HEADS // 2):
        r = _rope(zq[:, p * LANES:(p + 1) * LANES], cos, sin) * qscale
        rolled = pltpu.roll(r, HEAD_DIM, 1)
        if (2 * p) // GQA_GROUP == 0:
            even, odd = jnp.where(low, r, 0.0), jnp.where(low, rolled, 0.0)
        else:
            even, odd = jnp.where(low, 0.0, rolled), jnp.where(low, 0.0, r)
        qg_ref[0, 2 * p] = even.astype(BF16)
        qg_ref[0, 2 * p + 1] = odd.astype(BF16)

    zk = _dot(xb, w_ref[:, o:o + GQA_KV_W])
    o += GQA_KV_W
    ms = _head_sums(zk * zk, GQA_KV_W) * (1.0 / HEAD_DIM)
    zk = zk * lax.rsqrt(ms + RMS_EPS) * kgain_ref[...]
    kg_ref[...] = _rope(zk, cos, sin).astype(BF16)
    zv = _dot(xb, w_ref[:, o:o + GQA_KV_W])
    vg_ref[0] = jnp.where(low, zv, 1.0).astype(BF16)
    vg_ref[1] = jnp.where(low, 1.0, zv).astype(BF16)
    o += GQA_KV_W
    qm_ref[...] = _dot(xb, w_ref[:, o:o + MEM_W]).astype(BF16)


def _in_proj(x2, w_qkv, qgain, kgain, batch, seq):
    tokens = x2.shape[0]
    tm = PROJ_TM
    spb = seq // tm
    cos, sin = _rope_tables(seq)
    row = lambda w: pl.BlockSpec((tm, w), lambda i: (i, 0))
    const = lambda shape: pl.BlockSpec(shape, lambda i: (0,) * len(shape), pipeline_mode=pl.Buffered(1))
    tbl = pl.BlockSpec((tm, LANES), lambda i: (i % spb, 0))
    sds = lambda w: jax.ShapeDtypeStruct((tokens, w), BF16)
    return pl.pallas_call(
        _proj_kernel,
        grid=(tokens // tm,),
        in_specs=[row(D_MODEL), const((D_MODEL, QKV_W)), const((1, GQA_Q_W)), const((1, GQA_KV_W)), tbl, tbl],
        out_specs=[row(NA_W), row(NA_W), row(NA_W),
                   pl.BlockSpec((1, GQA_HEADS, tm, LANES), lambda i: (i // spb, 0, i % spb, 0)),
                   row(GQA_KV_W), pl.BlockSpec((GQA_KV_HEADS, tm, LANES), lambda i: (0, i, 0)), row(MEM_W)],
        out_shape=[sds(NA_W), sds(NA_W), sds(NA_W),
                   jax.ShapeDtypeStruct((batch, GQA_HEADS, seq, LANES), BF16),
                   sds(GQA_KV_W), jax.ShapeDtypeStruct((GQA_KV_HEADS, tokens, LANES), BF16), sds(MEM_W)],
        compiler_params=_cparams(("parallel",)),
        name="in_proj",
    )(x2, w_qkv, qgain, kgain, cos, sin)


def _memkv_kernel(m_ref, w_ref, k_ref, v_ref):
    z = _dot(m_ref[...].astype(BF16), w_ref[...])
    k_ref[...] = z[:, :MEM_W].astype(BF16)
    v_ref[...] = z[:, MEM_W:].astype(BF16)


def _mem_kv(mem2, w_kv):
    rows = mem2.shape[0]
    tm = 256
    return pl.pallas_call(
        _memkv_kernel,
        grid=(rows // tm,),
        in_specs=[pl.BlockSpec((tm, D_MODEL), lambda i: (i, 0)),
                  pl.BlockSpec((D_MODEL, 2 * MEM_W), lambda i: (0, 0))],
        out_specs=[pl.BlockSpec((tm, MEM_W), lambda i: (i, 0))] * 2,
        out_shape=[jax.ShapeDtypeStruct((rows, MEM_W), BF16)] * 2,
        compiler_params=_cparams(("parallel",)),
        name="mem_kv",
    )(mem2, w_kv)


NA_TQ = NA_QROWS * GRID_W
NA_TK = NA_KROWS * GRID_W


NA_GROUPS_PER_STEP = 8


def _na_kernel(q_ref, k_ref, v_ref, t_ref, o_ref, bias_sc, *, n_groups, which):
    @pl.when((pl.program_id(1) == 0) & (pl.program_id(2) == 0))
    def _():
        _fill_na_tiles(t_ref, bias_sc, which)

    low = lax.broadcasted_iota(jnp.int32, (NA_TQ, LANES), 1) < HEAD_DIM
    tiles = NA_TK // LANES
    for sub in range(NA_GROUPS_PER_STEP):
        rg = pl.program_id(2) * NA_GROUPS_PER_STEP + sub
        key_row0 = jnp.clip(NA_QROWS * rg - NA_KH // 2, 0, GRID_W - NA_KROWS)
        start = pl.multiple_of(key_row0 * GRID_W, GRID_W)
        variant = jnp.where(rg == 0, 0, jnp.where(rg == n_groups - 1, 2, 1))
        k = k_ref[pl.ds(start, NA_TK), :]
        v = v_ref[pl.ds(start, NA_TK), :]
        q = q_ref[sub * NA_TQ:(sub + 1) * NA_TQ, :]
        qq = jnp.concatenate([jnp.where(low, q, jnp.zeros_like(q)), jnp.where(low, jnp.zeros_like(q), q)], axis=0)
        s = _dot_nt(qq, k) + bias_sc[variant].reshape(2 * NA_TQ, NA_TK)
        st = [s[:, j * LANES:(j + 1) * LANES] for j in range(tiles)]
        part = st[0]
        for t in st[1:]:
            part = jnp.maximum(part, t)
        m = jnp.broadcast_to(jnp.max(part, axis=-1, keepdims=True), part.shape)
        ps = [jnp.exp2(t - m) for t in st]
        lpart = ps[0]
        for t in ps[1:]:
            lpart = lpart + t
        l = jnp.broadcast_to(jnp.sum(lpart, axis=-1, keepdims=True), lpart.shape)
        out = _dot(jnp.concatenate([t.astype(BF16) for t in ps], axis=-1), v) / l
        o_ref[sub * NA_TQ:(sub + 1) * NA_TQ, :] = jnp.where(low, out[:NA_TQ], out[NA_TQ:]).astype(BF16)


def _na_attention(qna, kna, vna, tiles, which, batch, seq):
    n_groups = seq // NA_TQ
    steps = n_groups // NA_GROUPS_PER_STEP
    pairs = NA_HEADS // 2
    kv = pl.BlockSpec((seq, LANES), lambda p, b, g: (b, p))
    qo = pl.BlockSpec((NA_GROUPS_PER_STEP * NA_TQ, LANES), lambda p, b, g: (b * steps + g, p))
    return pl.pallas_call(
        functools.partial(_na_kernel, n_groups=n_groups, which=which),
        grid=(pairs, batch, steps),
        in_specs=[qo, kv, kv, pl.BlockSpec((2,) + tiles.shape[1:], lambda p, b, g: (p, 0, 0, 0))],
        out_specs=qo,
        out_shape=jax.ShapeDtypeStruct(qna.shape, BF16),
        scratch_shapes=[pltpu.VMEM((which.shape[0], 2, NA_TQ, NA_TK), F32)],
        compiler_params=_cparams(("arbitrary", "arbitrary", "arbitrary")),
        name="na_attn",
    )(qna, kna, vna, tiles)


GQA_TQ = 128
GQA_TK = 1024


def _gqa_kernel(q_ref, k_ref, v_ref, wgu_ref, wd_ref, o_ref, wgu_bf_ref, wd_bf_ref, s_sc, mx_sc, acc_sc, *, seq):
    wgu_bf_ref[...] = wgu_ref[...].astype(BF16)
    wd_bf_ref[...] = wd_ref[...].astype(BF16)
    rows = GQA_GROUP * GQA_TQ
    n_chunks = seq // GQA_TK
    tiles = GQA_TK // LANES
    groups = range(GQA_KV_HEADS)
    lane = lax.broadcasted_iota(jnp.int32, (GQA_TQ, LANES), 1)
    low = lane < HEAD_DIM
    qs = [q_ref[0, g * GQA_GROUP:(g + 1) * GQA_GROUP].reshape(rows, LANES) for g in groups]
    mx_sc[...] = jnp.full(mx_sc.shape, -jnp.inf, F32)

    for c in range(n_chunks):
        k = k_ref[c * GQA_TK:(c + 1) * GQA_TK, :]
        for g in groups:
            s = _dot_nt(qs[g], k)
            s_sc[g, c] = s
            part = s[:, :LANES]
            for j in range(1, tiles):
                part = jnp.maximum(part, s[:, j * LANES:(j + 1) * LANES])
            mx_sc[g] = jnp.maximum(mx_sc[g], part)
    ms = [jnp.broadcast_to(jnp.max(mx_sc[g], axis=-1, keepdims=True), (rows, LANES)) for g in groups]
    acc_sc[...] = jnp.zeros(acc_sc.shape, F32)

    for c in range(n_chunks):
        for g in groups:
            p = [jnp.exp2(s_sc[g, c, :, j * LANES:(j + 1) * LANES] - ms[g]).astype(BF16) for j in range(tiles)]
            acc_sc[g] += _dot(jnp.concatenate(p, axis=-1), v_ref[g, c * GQA_TK:(c + 1) * GQA_TK, :])
    for g in groups:
        acc = acc_sc[g]
        out = acc / pltpu.roll(acc, HEAD_DIM, 1)
        for pp in range(GQA_GROUP // 2):
            a = out[(2 * pp) * GQA_TQ:(2 * pp + 1) * GQA_TQ]
            b = out[(2 * pp + 1) * GQA_TQ:(2 * pp + 2) * GQA_TQ]
            if g == 0:
                pair = jnp.where(low, a, pltpu.roll(b, HEAD_DIM, 1))
            else:
                pair = jnp.where(low, pltpu.roll(a, HEAD_DIM, 1), b)
            col = (g * (GQA_GROUP // 2) + pp) * LANES
            o_ref[:, col:col + LANES] = pair.astype(BF16)


def _gqa_attention(qg, kg, vg, w_gate_up, w_down, batch, seq):
    nq = seq // GQA_TQ
    rows = GQA_GROUP * GQA_TQ
    stat = pltpu.VMEM((GQA_KV_HEADS, rows, LANES), F32)
    n_exp, w_rows = w_gate_up.shape[:2]
    slabs = batch * nq // n_exp
    slab = w_rows // slabs
    assert slabs * n_exp == batch * nq and slab * slabs == w_rows and w_down.shape[:2] == (n_exp, w_rows)
    wspec = lambda w: pl.BlockSpec((1, slab, w.shape[2]),
                                   lambda b, i: ((b * nq + i) // slabs, (b * nq + i) % slabs, 0))
    return pl.pallas_call(
        functools.partial(_gqa_kernel, seq=seq),
        grid=(batch, nq),
        in_specs=[pl.BlockSpec((1, GQA_HEADS, GQA_TQ, LANES), lambda b, i: (b, 0, i, 0)),
                  pl.BlockSpec((seq, LANES), lambda b, i: (b, 0)),
                  pl.BlockSpec((GQA_KV_HEADS, seq, LANES), lambda b, i: (0, b, 0)),
                  wspec(w_gate_up), wspec(w_down)],
        out_specs=[pl.BlockSpec((GQA_TQ, GQA_Q_W), lambda b, i: (b * nq + i, 0)), wspec(w_gate_up), wspec(w_down)],
        out_shape=[jax.ShapeDtypeStruct((batch * seq, GQA_Q_W), BF16),
                   jax.ShapeDtypeStruct(w_gate_up.shape, BF16), jax.ShapeDtypeStruct(w_down.shape, BF16)],
        scratch_shapes=[pltpu.VMEM((GQA_KV_HEADS, seq // GQA_TK, rows, GQA_TK), F32), stat, stat],
        compiler_params=_cparams(("parallel", "parallel")),
        name="gqa_attn",
    )(qg, kg, vg, w_gate_up, w_down)


MERGE_TM = 1024
MERGE_HALVES = 1


def _merge_kernel(x_ref, yna_ref, ygqa_ref, qm_ref, mk_ref, mv_ref, wg_ref, bg_ref, wb_ref, wo_ref,
                  g1_ref, b1_ref, wr_ref, x1_ref, aff_ref, *, alpha):
    rows_per = MERGE_TM // MERGE_HALVES
    mem_scale = 1.0 / math.sqrt(MEM_HEAD_DIM)
    w = wr_ref[...]
    w_hi = w.astype(BF16)
    w_hl = jnp.concatenate([w_hi, (w - w_hi.astype(F32)).astype(BF16)], axis=0)
    for part in range(MERGE_HALVES):
        rows = slice(part * rows_per, (part + 1) * rows_per)
        x = x_ref[rows, :]
        xb = x.astype(BF16)
        ymem = []
        for h in range(MEM_HEADS):
            sl = slice(h * MEM_HEAD_DIM, (h + 1) * MEM_HEAD_DIM)
            s = _dot_nt(qm_ref[rows, sl], mk_ref[:, sl]) * mem_scale
            m = jnp.max(s, axis=-1, keepdims=True)
            p = jnp.exp(s - m)
            l = jnp.sum(p, axis=-1, keepdims=True)
            ymem.append((_dot(p.astype(BF16), mv_ref[:, sl]) / l).astype(BF16))
        branches = (yna_ref[rows, :], ygqa_ref[rows, :], jnp.concatenate(ymem, axis=-1))
        merged = jnp.zeros((rows_per, D_MODEL), F32)
        for g in range(N_BRANCH):
            cols = slice(g * D_MODEL, (g + 1) * D_MODEL)
            gate = jax.nn.sigmoid(_dot(xb, wg_ref[:, cols]) + bg_ref[:, cols])
            merged = merged + gate * _dot(branches[g], wb_ref[g])
        mix = _dot(merged.astype(BF16), wo_ref[...])
        x1 = _layer_norm(alpha * x + mix, g1_ref[...], b1_ref[...])
        x1_ref[rows, :] = x1
        x1_hi = x1.astype(BF16)
        x1_lo = (x1 - x1_hi.astype(F32)).astype(BF16)
        by_hi = _dot_nt(w_hl, x1_hi)
        logits = by_hi[:N_EXPERTS] + by_hi[N_EXPERTS:] + _dot_nt(w_hi, x1_lo)
        m = jnp.max(logits, axis=0, keepdims=True)
        e = jnp.exp(logits - m)
        aff_ref[0, :, rows] = e / jnp.sum(e, axis=0, keepdims=True)


def _merge(x2, yna, ygqa, qm, mk, mv, w_gate, b_gate, w_branch, w_out, g1, b1, w_router, seq, mem_tokens, alpha):
    tokens = x2.shape[0]
    tm = MERGE_TM
    tpb = seq // tm
    row = lambda w: pl.BlockSpec((tm, w), lambda i: (i, 0))
    const = lambda shape: pl.BlockSpec(shape, lambda i: (0,) * len(shape), pipeline_mode=pl.Buffered(1))
    memb = pl.BlockSpec((mem_tokens, MEM_W), lambda i: (i // tpb, 0))
    return pl.pallas_call(
        functools.partial(_merge_kernel, alpha=alpha),
        grid=(tokens // tm,),
        in_specs=[row(D_MODEL), row(NA_W), row(GQA_Q_W), row(MEM_W), memb, memb,
                  const(w_gate.shape), const(b_gate.shape), const(w_branch.shape), const(w_out.shape),
                  const(g1.shape), const(b1.shape), const(w_router.shape)],
        out_specs=[row(D_MODEL), pl.BlockSpec((1, N_EXPERTS, tm), lambda i: (i // tpb, 0, i % tpb))],
        out_shape=[jax.ShapeDtypeStruct((tokens, D_MODEL), F32),
                   jax.ShapeDtypeStruct((tokens // seq, N_EXPERTS, seq), F32)],
        compiler_params=_cparams(("parallel",)),
        name="merge",
    )(x2, yna, ygqa, qm, mk, mv, w_gate, b_gate, w_branch, w_out, g1, b1, w_router)


SEL_BLK = 256
SLOT_HI = 16
SLOT_LO = 32
SEL_PARTS = 5


def _prefix_counts(mask, out_ref, seq):
    r = lax.broadcasted_iota(jnp.int32, (SEL_BLK, SEL_BLK), 0)
    c = lax.broadcasted_iota(jnp.int32, (SEL_BLK, SEL_BLK), 1)
    tri = (r <= c).astype(BF16)
    run = jnp.zeros((mask.shape[0], 1), F32)
    for j in range(seq // SEL_BLK):
        cs = _dot(mask[:, j * SEL_BLK:(j + 1) * SEL_BLK].astype(BF16), tri) + run
        out_ref[:, j * SEL_BLK:(j + 1) * SEL_BLK] = cs
        run = cs[:, SEL_BLK - 1:SEL_BLK]


def _select_kernel(aff_ref, idx_ref, gate_ref, cnt_sc, lhs_sc, rhs_sc, *, seq, cap):
    a = aff_ref[0]
    thr_bits = jnp.zeros((N_EXPERTS, 1), jnp.int32)
    for bit in range(30, -1, -1):
        cand = thr_bits | (1 << bit)
        cnt = jnp.sum((a >= pltpu.bitcast(cand, F32)).astype(F32), axis=-1, keepdims=True)
        thr_bits = jnp.where(cnt >= cap, cand, thr_bits)
    thr = pltpu.bitcast(thr_bits, F32)
    gt = a > thr
    eq = a == thr
    need = cap - jnp.sum(gt.astype(F32), axis=-1, keepdims=True)
    _prefix_counts(eq.astype(F32), cnt_sc, seq)
    sel = gt | (eq & (cnt_sc[...] <= need))
    _prefix_counts(sel.astype(F32), cnt_sc, seq)

    slot = cnt_sc[...] - 1.0
    s_hi = jnp.floor(slot * (1.0 / SLOT_LO))
    s_lo = slot - SLOT_LO * s_hi
    tok = lax.broadcasted_iota(jnp.int32, (1, seq), 1).astype(F32)
    t_hi = jnp.floor(tok * (1.0 / 64.0))
    t_lo = tok - 64.0 * t_hi
    ph = lax.broadcasted_iota(jnp.int32, (SLOT_HI, seq), 0).astype(F32)
    pl_ = lax.broadcasted_iota(jnp.int32, (SLOT_LO, seq), 0).astype(F32)
    for e in range(N_EXPERTS):
        row = slice(e, e + 1)
        onehot_hi = jnp.where((s_hi[row] == ph) & sel[row], 1.0, 0.0)
        g = a[row]
        g1 = g.astype(BF16).astype(F32)
        g2 = (g - g1).astype(BF16).astype(F32)
        g3 = g - g1 - g2
        for part, val in enumerate((t_hi, t_lo, g1, g2, g3)):
            r0 = (part * N_EXPERTS + e) * SLOT_HI
            lhs_sc[r0:r0 + SLOT_HI, :] = (onehot_hi * val).astype(BF16)
        rhs_sc[e * SLOT_LO:(e + 1) * SLOT_LO, :] = jnp.where(s_lo[row] == pl_, 1.0, 0.0).astype(BF16)
    res = _dot_nt(lhs_sc[...], rhs_sc[...])
    n = N_EXPERTS * SLOT_HI
    r = lax.broadcasted_iota(jnp.int32, (n, N_EXPERTS * SLOT_LO), 0) // SLOT_HI
    c = lax.broadcasted_iota(jnp.int32, (n, N_EXPERTS * SLOT_LO), 1) // SLOT_LO
    parts = []
    for part in range(SEL_PARTS):
        blk = jnp.where(r == c, res[part * n:(part + 1) * n], 0.0)
        parts.append(jnp.sum(blk.reshape(N_EXPERTS, SLOT_HI, N_EXPERTS * SLOT_LO), axis=0))
    idx_ref[0] = (64.0 * parts[0] + parts[1]).astype(jnp.int32)
    gate_ref[0] = (parts[2] + parts[3]) + parts[4]


def _ec_select(aff, batch, seq, cap):
    assert cap == SLOT_HI * SLOT_LO
    out = pl.BlockSpec((1, SLOT_HI, N_EXPERTS * SLOT_LO), lambda b: (b, 0, 0))
    idx_t, gate_t = pl.pallas_call(
        functools.partial(_select_kernel, seq=seq, cap=cap),
        grid=(batch,),
        in_specs=[pl.BlockSpec((1, N_EXPERTS, seq), lambda b: (b, 0, 0))],
        out_specs=[out, out],
        out_shape=[jax.ShapeDtypeStruct((batch, SLOT_HI, N_EXPERTS * SLOT_LO), jnp.int32),
                   jax.ShapeDtypeStruct((batch, SLOT_HI, N_EXPERTS * SLOT_LO), F32)],
        scratch_shapes=[pltpu.VMEM((N_EXPERTS, seq), F32),
                        pltpu.VMEM((SEL_PARTS * N_EXPERTS * SLOT_HI, seq), BF16),
                        pltpu.VMEM((N_EXPERTS * SLOT_LO, seq), BF16)],
        compiler_params=_cparams(("parallel",)),
        name="ec_select",
    )(aff)
    order = lambda v: v.reshape(batch, SLOT_HI, N_EXPERTS, SLOT_LO).transpose(0, 2, 1, 3).reshape(batch, N_EXPERTS, cap)
    return order(idx_t), order(gate_t)


FFN_COL_PHASES = 1
LN2_TM = 256


def _gather_rows(idx_ref, src, dst, lo, hi):
    for p in range(lo, hi):
        dst[pl.ds(p, 1), :] = src[pl.ds(idx_ref[0, 0, p], 1), :]


def _scatter_rows(idx_ref, gate_ref, src, acc_sc, lo, hi):
    for p0 in range(lo, hi, 8):
        ps = range(p0, p0 + 8)
        ts = [idx_ref[0, 0, p] for p in ps]
        rows = [acc_sc[pl.ds(t, 1), :] + src[pl.ds(p, 1), :] * gate_ref[0, 0, p] for p, t in zip(ps, ts)]
        for t, row in zip(ts, rows):
            acc_sc[pl.ds(t, 1), :] = row


def _ffn_kernel(idx_ref, idx_next_ref, idx_prev_ref, gate_ref, gate_prev_ref, x_hbm, wgu_ref, wd_ref,
                g2_ref, b2_ref, out_hbm, x_sc, u_sc, xs_sc, act_sc, ye_sc, acc_sc, out_sc, sem_x, sem_out,
                *, cap, d_ff, seq, batch, alpha):
    b = pl.program_id(0)
    e = pl.program_id(1)
    n_tiles = seq // LN2_TM

    def x_copy(seq_id, i):
        rows = pl.ds(i * LN2_TM, LN2_TM)
        return pltpu.make_async_copy(x_hbm.at[seq_id, rows], x_sc.at[rows], sem_x.at[i])

    @pl.when((e == 0) & (b == 0))
    def _():
        for i in range(n_tiles):
            x_copy(0, i).start()
        ye_sc[...] = jnp.zeros(ye_sc.shape, F32)

    @pl.when(e == 0)
    def _():
        for i in range(n_tiles):
            x_copy(b, i).wait()
        acc_sc[...] = alpha * x_sc[...]
        _gather_rows(idx_ref, x_sc, u_sc, 0, cap)
        xs_sc[...] = u_sc[...].astype(BF16)

    width = d_ff // FFN_COL_PHASES
    out_w = D_MODEL // FFN_COL_PHASES
    share = cap // FFN_COL_PHASES
    for ph in range(2 * FFN_COL_PHASES):
        @pl.when(e + ph >= 0)
        def _(ph=ph):
            if ph < FFN_COL_PHASES:
                _scatter_rows(idx_prev_ref, gate_prev_ref, ye_sc, acc_sc, ph * share, (ph + 1) * share)
                xs = xs_sc[...]
                hg = _dot(xs, wgu_ref[0, :, ph * width:(ph + 1) * width])
                hu = _dot(xs, wgu_ref[0, :, d_ff + ph * width:d_ff + (ph + 1) * width])
                act_sc[:, ph * width:(ph + 1) * width] = (hg * jax.nn.sigmoid(hg) * hu).astype(BF16)
            else:
                c = ph - FFN_COL_PHASES
                _gather_rows(idx_next_ref, x_sc, u_sc, c * share, (c + 1) * share)
                ye_sc[:, c * out_w:(c + 1) * out_w] = _dot(act_sc[...], wd_ref[0, :, c * out_w:(c + 1) * out_w])
                if c == FFN_COL_PHASES - 1:
                    xs_sc[...] = u_sc[...].astype(BF16)

    @pl.when(e == N_EXPERTS - 1)
    def _():
        def o_copy(i, slot):
            return pltpu.make_async_copy(out_sc.at[slot], out_hbm.at[b, pl.ds(i * LN2_TM, LN2_TM)], sem_out.at[slot])

        @pl.when(b + 1 < batch)
        def _():
            for i in range(n_tiles):
                x_copy(b + 1, i).start()

        _scatter_rows(idx_ref, gate_ref, ye_sc, acc_sc, 0, cap)
        for i in range(n_tiles):
            slot = i % 2
            rows = slice(i * LN2_TM, (i + 1) * LN2_TM)
            if i >= 2:
                o_copy(i - 2, slot).wait()
            out_sc[slot] = _layer_norm(acc_sc[rows, :], g2_ref[...], b2_ref[...])
            o_copy(i, slot).start()
        o_copy(n_tiles - 2, n_tiles % 2).wait()
        o_copy(n_tiles - 1, (n_tiles - 1) % 2).wait()


def _ec_ffn(idx, gate, x1, wgu, wd, g2, b2, batch, seq, cap, alpha):
    d_ff = wd.shape[1]
    slot = pl.BlockSpec((1, 1, cap), lambda b, e: (b * N_EXPERTS + e, 0, 0), memory_space=pltpu.SMEM)
    vec = pl.BlockSpec((1, D_MODEL), lambda b, e: (0, 0))
    flat = lambda v: v.reshape(batch * N_EXPERTS, 1, cap)
    idx_next = jnp.concatenate([idx[:, 1:], idx[:, -1:]], axis=1)
    idx_prev = jnp.concatenate([idx[:, :1], idx[:, :-1]], axis=1)
    gate_prev = jnp.concatenate([jnp.zeros_like(gate[:, :1]), gate[:, :-1]], axis=1)
    return pl.pallas_call(
        functools.partial(_ffn_kernel, cap=cap, d_ff=d_ff, seq=seq, batch=batch, alpha=alpha),
        grid=(batch, N_EXPERTS),
        in_specs=[slot, slot, slot, slot, slot,
                  pl.BlockSpec(memory_space=pl.ANY),
                  pl.BlockSpec((1, D_MODEL, 2 * d_ff), lambda b, e: (e, 0, 0)),
                  pl.BlockSpec((1, d_ff, D_MODEL), lambda b, e: (e, 0, 0)),
                  vec, vec],
        out_specs=pl.BlockSpec(memory_space=pl.ANY),
        out_shape=jax.ShapeDtypeStruct((batch, seq, D_MODEL), F32),
        scratch_shapes=[pltpu.VMEM((seq, D_MODEL), F32),
                        pltpu.VMEM((cap, D_MODEL), F32), pltpu.VMEM((cap, D_MODEL), BF16),
                        pltpu.VMEM((cap, d_ff), BF16), pltpu.VMEM((cap, D_MODEL), F32),
                        pltpu.VMEM((seq, D_MODEL), F32), pltpu.VMEM((2, LN2_TM, D_MODEL), F32),
                        pltpu.SemaphoreType.DMA((seq // LN2_TM,)), pltpu.SemaphoreType.DMA((2,))],
        compiler_params=_cparams(("arbitrary", "arbitrary"), FFN_VMEM_LIMIT),
        name="ec_ffn",
    )(flat(idx), flat(idx_next), flat(idx_prev), flat(gate), flat(gate_prev), x1, wgu, wd, g2, b2)


def kernel(x, mem, w_in, b_gate, na_rpb, gqa_q_gain, gqa_k_gain, w_mem_kv, w_branch, w_out,
           ln1_g, ln1_b, w_router, w_gate_up, w_down, ln2_g, ln2_b):
    batch, seq, d = x.shape
    mem_tokens = mem.shape[1]
    depth = w_in.shape[0]
    alpha = (2 * depth) ** 0.25
    cap = EC_CAPACITY_FACTOR * seq // N_EXPERTS
    assert d == D_MODEL and seq == GRID_W * GRID_W and w_in.shape[2] == QKV_W + N_BRANCH * D_MODEL
    row = lambda v: v.reshape(1, -1)
    for l in range(depth):
        x2 = x.reshape(batch * seq, d)
        w_qkv = w_in[l, :, :QKV_W].astype(BF16)
        w_gate = w_in[l, :, QKV_W:].astype(BF16)
        qna, kna, vna, qg, kg, vg, qm = _in_proj(
            x2, w_qkv, row(jnp.tile(gqa_q_gain[l], GQA_HEADS)), row(jnp.tile(gqa_k_gain[l], GQA_KV_HEADS)),
            batch, seq)
        mk, mv = _mem_kv(mem.reshape(batch * mem_tokens, d), w_mem_kv[l].astype(BF16))
        yna = _na_attention(qna, kna, vna, *_na_bias_tiles(na_rpb[l]), batch, seq)
        ygqa, wgu_bf, wd_bf = _gqa_attention(qg, kg, vg, w_gate_up[l], w_down[l], batch, seq)
        x1, aff = _merge(x2, yna, ygqa, qm, mk, mv, w_gate, row(b_gate[l]), w_branch[l].astype(BF16),
                              w_out[l].astype(BF16), row(ln1_g[l]), row(ln1_b[l]), w_router[l].T, seq,
                              mem_tokens, alpha)
        idx, gate = _ec_select(aff, batch, seq, cap)
        x = _ec_ffn(idx, gate, x1.reshape(batch, seq, d), wgu_bf, wd_bf, row(ln2_g[l]), row(ln2_b[l]),
                    batch, seq, cap, alpha)
    return x
```
